```python
import jax, jax.numpy as jnp
from jax import lax
import numpy as np


D_MODEL = 1024
BATCH = 16
SEQ = 2048
DEPTH = 2

GRID_W = 64
CTX_LEN = 256
EPS = 1e-6
N_REC = (DEPTH + 1) // 2
N_ATT = DEPTH // 2

A_HEADS = 4
A_KEY = 128
A_VAL = D_MODEL // (2 * A_HEADS)
A_QK = A_HEADS * A_KEY
A_V = A_HEADS * A_VAL
B_HEADS = 4
B_VAL = D_MODEL // (2 * B_HEADS)
B_KEY = B_VAL // 2
B_QK = B_HEADS * B_KEY
B_V = B_HEADS * B_VAL
B_GATE_RANK = 16
GLA_GATE_NORM = 16.0
LA_CHUNK = 32
REC_WIDTHS = (A_QK, A_QK, A_QK, A_V, A_V, B_QK, B_QK, B_V, B_GATE_RANK, B_GATE_RANK, B_V)
REC_IN = sum(REC_WIDTHS)
REC_OUT = A_V + B_V

C_HEAD_DIM = 64
C_HEADS = D_MODEL // C_HEAD_DIM
C_KV_HEADS = 4
C_Q = C_HEADS * C_HEAD_DIM
C_KV = C_KV_HEADS * C_HEAD_DIM
WINDOW = 128
ATT_BLOCK = 128
ROPE_BASE = 10000.0

FFN_HIDDEN = -(-(8 * D_MODEL) // (3 * 256)) * 256

kernel_name = 'hybrid_hgrn2_gla_swa_dit_block'


def rms_norm(x, g):
    xf = x.astype(jnp.float32)
    y = xf * lax.rsqrt(jnp.mean(jnp.square(xf), axis=-1, keepdims=True) + EPS)
    return (y * g.astype(jnp.float32)).astype(x.dtype)


def modulate(x, g, shift, scale):
    return rms_norm(x, g) * (1.0 + scale) + shift


def split_heads(a, n_heads):
    b_, l_, _ = a.shape
    return a.reshape(b_, l_, n_heads, -1).transpose(0, 2, 1, 3)


def merge_heads(a):
    b_, h_, l_, e_ = a.shape
    return a.transpose(0, 2, 1, 3).reshape(b_, l_, h_ * e_)


def swiglu(u, w_in, w_out):
    gt, up = jnp.split(u @ w_in, 2, axis=-1)
    return (jax.nn.silu(gt) * up) @ w_out


def hgrn_lower_bounds(lb_logits):
    p = jax.nn.softmax(lb_logits.astype(jnp.float32), axis=0)
    return jnp.cumsum(p, axis=0)[:-1]


def chunk_gla(q, k, v, g, s0):
    b_, h_, t_, _ = q.shape
    dv = v.shape[-1]
    n = t_ // LA_CHUNK

    def chunks(a):
        return a.reshape(b_, h_, n, LA_CHUNK, a.shape[-1]).astype(jnp.float32)

    qc, kc, vc, gc = chunks(q), chunks(k), chunks(v), chunks(g)
    bc = jnp.cumsum(gc, axis=3)
    b_last = bc[:, :, :, -1:, :]
    q_dec = qc * jnp.exp(bc)
    k_inv = kc * jnp.exp(-bc)
    k_end = kc * jnp.exp(b_last - bc)
    scores = jnp.einsum('bhnik,bhnjk->bhnij', q_dec, k_inv)
    causal_in_chunk = jnp.tril(jnp.ones((LA_CHUNK, LA_CHUNK), dtype=bool))
    scores = jnp.where(causal_in_chunk, scores, 0.0)
    o_intra = jnp.einsum('bhnij,bhnjv->bhniv', scores, vc)

    def step(s, xs):
        q_n, k_n, v_n, d_n = xs
        o_n = jnp.einsum('bhck,bhkv->bhcv', q_n, s)
        s = s * d_n[..., None] + jnp.einsum('bhck,bhcv->bhkv', k_n, v_n)
        return s, o_n

    xs = (jnp.moveaxis(q_dec, 2, 0), jnp.moveaxis(k_end, 2, 0), jnp.moveaxis(vc, 2, 0),
          jnp.moveaxis(jnp.exp(b_last[:, :, :, 0, :]), 2, 0))
    s_fin, o_inter = lax.scan(step, s0.astype(jnp.float32), xs)
    o = o_intra + jnp.moveaxis(o_inter, 0, 2)
    return o.reshape(b_, h_, t_, dv), s_fin


def prefix_scan(lat, ctx, reverse):
    if reverse:
        flip = lambda a: jnp.flip(a, axis=2)
    else:
        flip = lambda a: a
    q_c, k_c, v_c, g_c = (flip(a) for a in ctx)
    s0 = jnp.zeros((q_c.shape[0], q_c.shape[1], q_c.shape[3], v_c.shape[3]), jnp.float32)
    o_c, s_c = chunk_gla(q_c, k_c, v_c, g_c, s0)
    q, k, v, g = (flip(a) for a in lat)
    o, _ = chunk_gla(q, k, v, g, s_c)
    return flip(o), flip(o_c)


def rec_features(u, w_in, lb, w_g2, b_g2):
    offsets = np.cumsum(REC_WIDTHS)[:-1].tolist()
    (a_q, a_zf, a_zb, a_i, a_og, b_q, b_k, b_v, b_lf, b_lb, b_r) = jnp.split(u @ w_in, offsets, axis=-1)
    qa = split_heads(jax.nn.silu(a_q) * A_KEY ** -0.5, A_HEADS)
    va = split_heads(a_i, A_HEADS)
    a_dirs = []
    for d, z in enumerate((a_zf, a_zb)):
        f = lb[d] + (1.0 - lb[d]) * jax.nn.sigmoid(z.astype(jnp.float32))
        a_dirs.append((qa, split_heads(1.0 - f, A_HEADS), va, split_heads(jnp.log(f), A_HEADS)))
    qb = split_heads(b_q * B_KEY ** -0.5, B_HEADS)
    kb = split_heads(b_k, B_HEADS)
    vb = split_heads(b_v, B_HEADS)
    b_dirs = []
    for d, lr in enumerate((b_lf, b_lb)):
        gk = jax.nn.log_sigmoid((lr @ w_g2[d] + b_g2[d]).astype(jnp.float32)) / GLA_GATE_NORM
        b_dirs.append((qb, kb, vb, split_heads(gk, B_HEADS)))
    return a_dirs, b_dirs, a_og, b_r


def bidir_group(dirs_lat, dirs_ctx, gn, gate_lat, gate_ctx, need_ctx):
    o_lat, o_ctx = None, None
    for d in range(2):
        ol, oc = prefix_scan(dirs_lat[d], dirs_ctx[d], reverse=(d == 1))
        o_lat = ol if o_lat is None else o_lat + ol
        o_ctx = oc if o_ctx is None else o_ctx + oc
    y_lat = merge_heads(rms_norm(o_lat, gn)).astype(gate_lat.dtype) * jax.nn.silu(gate_lat)
    y_ctx = None
    if need_ctx:
        y_ctx = merge_heads(rms_norm(o_ctx, gn)).astype(gate_ctx.dtype) * jax.nn.silu(gate_ctx)
    return y_lat, y_ctx


def recurrent_mixer(u, u_c, w_in, w_out, lb, w_g2, b_g2, gn_a, gn_b, need_ctx):
    a_lat, b_lat, ag_lat, bg_lat = rec_features(u, w_in, lb, w_g2, b_g2)
    a_ctx, b_ctx, ag_ctx, bg_ctx = rec_features(u_c, w_in, lb, w_g2, b_g2)
    ya, ya_c = bidir_group(a_lat, a_ctx, gn_a, ag_lat, ag_ctx, need_ctx)
    yb, yb_c = bidir_group(b_lat, b_ctx, gn_b, bg_lat, bg_ctx, need_ctx)
    y = jnp.concatenate([ya, yb], axis=-1) @ w_out
    y_c = jnp.concatenate([ya_c, yb_c], axis=-1) @ w_out if need_ctx else None
    return y, y_c


def axial_rope_tables(t_len):
    n_rows = t_len // GRID_W
    row = jnp.repeat(jnp.arange(n_rows), GRID_W).astype(jnp.float32)
    col = jnp.tile(jnp.arange(GRID_W), n_rows).astype(jnp.float32)
    half = C_HEAD_DIM // 2
    inv = ROPE_BASE ** (-jnp.arange(0, half, 2, dtype=jnp.float32) / half)
    ang = jnp.concatenate([row[:, None] * inv, col[:, None] * inv], axis=-1)
    return jnp.cos(ang), jnp.sin(ang)


def apply_rope(x, cos, sin):
    xf = x.astype(jnp.float32).reshape(x.shape[:-1] + (-1, 2))
    x0, x1 = xf[..., 0], xf[..., 1]
    y = jnp.stack([x0 * cos - x1 * sin, x0 * sin + x1 * cos], axis=-1)
    return y.reshape(x.shape).astype(x.dtype)


def windowed_sink_attention(q, k, v, k_c, v_c, sink):
    b_, hq, t_, e_ = q.shape
    g_ = hq // C_KV_HEADS
    nb = t_ // ATT_BLOCK
    lc = k_c.shape[2]
    scale = e_ ** -0.5
    qb = q.reshape(b_, C_KV_HEADS, g_, nb, ATT_BLOCK, e_)

    def band(a):
        ap = jnp.pad(a, ((0, 0), (0, 0), (ATT_BLOCK, ATT_BLOCK), (0, 0)))
        ap = ap.reshape(b_, C_KV_HEADS, nb + 2, ATT_BLOCK, e_)
        return jnp.concatenate([ap[:, :, :-2], ap[:, :, 1:-1], ap[:, :, 2:]], axis=3)

    kb, vb = band(k), band(v)
    qi = jnp.arange(ATT_BLOCK)[:, None]
    kj = jnp.arange(3 * ATT_BLOCK)[None, :] - ATT_BLOCK
    rel_ok = jnp.abs(kj - qi) <= WINDOW
    key_pos = jnp.arange(nb)[:, None] * ATT_BLOCK + kj
    in_range = (key_pos >= 0) & (key_pos < t_)
    mask = rel_ok[None, :, :] & in_range[:, None, :]
    sink_f = sink.astype(jnp.float32).reshape(1, C_KV_HEADS, g_, 1, 1)
    k_cf = k_c.astype(jnp.float32)
    v_cf = v_c.astype(jnp.float32)

    def block_fn(args):
        q_n, k_n, v_n, m_n = args
        q_n = q_n.astype(jnp.float32) * scale
        s_lat = jnp.einsum('bkgie,bkje->bkgij', q_n, k_n.astype(jnp.float32))
        s_lat = jnp.where(m_n, s_lat, -1e30)
        s_ctx = jnp.einsum('bkgie,bkje->bkgij', q_n, k_cf)
        s_snk = jnp.broadcast_to(sink_f, s_lat.shape[:-1] + (1,))
        p = jax.nn.softmax(jnp.concatenate([s_lat, s_ctx, s_snk], axis=-1), axis=-1)
        p_lat = p[..., :3 * ATT_BLOCK]
        p_ctx = p[..., 3 * ATT_BLOCK:3 * ATT_BLOCK + lc]
        return (jnp.einsum('bkgij,bkje->bkgie', p_lat, v_n.astype(jnp.float32))
                + jnp.einsum('bkgij,bkje->bkgie', p_ctx, v_cf))

    out = lax.map(block_fn, (jnp.moveaxis(qb, 3, 0), jnp.moveaxis(kb, 2, 0), jnp.moveaxis(vb, 2, 0), mask))
    out = out.transpose(1, 2, 3, 0, 4, 5).reshape(b_, hq, t_, e_)
    return out.astype(q.dtype)


def context_sink_attention(q, k, v, sink):
    b_, hq, l_, e_ = q.shape
    g_ = hq // C_KV_HEADS
    qg = q.reshape(b_, C_KV_HEADS, g_, l_, e_).astype(jnp.float32) * e_ ** -0.5
    s = jnp.einsum('bkgie,bkje->bkgij', qg, k.astype(jnp.float32))
    s_snk = jnp.broadcast_to(sink.astype(jnp.float32).reshape(1, C_KV_HEADS, g_, 1, 1), s.shape[:-1] + (1,))
    p = jax.nn.softmax(jnp.concatenate([s, s_snk], axis=-1), axis=-1)[..., :-1]
    o = jnp.einsum('bkgij,bkje->bkgie', p, v.astype(jnp.float32))
    return o.reshape(b_, hq, l_, e_).astype(q.dtype)


def attention_mixer(u, u_c, w_qkv, w_o, sink, cos, sin, need_ctx):
    q, k, v = jnp.split(u @ w_qkv, [C_Q, C_Q + C_KV], axis=-1)
    q = apply_rope(split_heads(q, C_HEADS), cos, sin)
    k = apply_rope(split_heads(k, C_KV_HEADS), cos, sin)
    v = split_heads(v, C_KV_HEADS)
    k_c, v_c = jnp.split(u_c @ w_qkv[:, C_Q:], 2, axis=-1)
    k_c = split_heads(k_c, C_KV_HEADS)
    v_c = split_heads(v_c, C_KV_HEADS)
    y = merge_heads(windowed_sink_attention(q, k, v, k_c, v_c, sink)) @ w_o
    y_c = None
    if need_ctx:
        q_c = split_heads(u_c @ w_qkv[:, :C_Q], C_HEADS)
        y_c = merge_heads(context_sink_attention(q_c, k_c, v_c, sink)) @ w_o
    return y, y_c


def setup_inputs(seed: int = 0) -> dict:
    key = jax.random.key(seed)
    ks = jax.random.split(key, 19)

    def nrm(k, shape, fan_in, gain=1.0):
        return jax.random.normal(k, shape, jnp.float32) * (gain * fan_in ** -0.5)

    return {
        'x': jax.random.normal(ks[0], (BATCH, SEQ, D_MODEL), jnp.float32),
        'c': jax.random.normal(ks[1], (BATCH, D_MODEL), jnp.float32),
        'ctx': jax.random.normal(ks[2], (BATCH, CTX_LEN, D_MODEL), jnp.float32),
        'c_ctx': jax.random.normal(ks[3], (D_MODEL,), jnp.float32),
        'ada_w': nrm(ks[4], (DEPTH, D_MODEL, 6 * D_MODEL), D_MODEL, 0.5),
        'ada_b': 0.02 * jax.random.normal(ks[5], (DEPTH, 6 * D_MODEL), jnp.float32),
        'norm_g': 1.0 + 0.05 * jax.random.normal(ks[6], (DEPTH, 4, D_MODEL), jnp.float32),
        'rec_w_in': nrm(ks[7], (N_REC, D_MODEL, REC_IN), D_MODEL),
        'rec_w_out': nrm(ks[8], (N_REC, REC_OUT, D_MODEL), REC_OUT),
        'rec_lb_logits': 0.5 * jax.random.normal(ks[9], (N_REC + 1, 2, A_QK), jnp.float32),
        'rec_w_g2': nrm(ks[10], (N_REC, 2, B_GATE_RANK, B_QK), B_GATE_RANK),
        'rec_b_g2': 0.1 * jax.random.normal(ks[11], (N_REC, 2, B_QK), jnp.float32),
        'rec_gn_a': 1.0 + 0.05 * jax.random.normal(ks[12], (N_REC, A_VAL), jnp.float32),
        'rec_gn_b': 1.0 + 0.05 * jax.random.normal(ks[13], (N_REC, B_VAL), jnp.float32),
        'att_w_qkv': nrm(ks[14], (N_ATT, D_MODEL, C_Q + 2 * C_KV), D_MODEL),
        'att_w_o': nrm(ks[15], (N_ATT, C_Q, D_MODEL), C_Q),
        'att_sink': 0.5 * jax.random.normal(ks[16], (N_ATT, C_HEADS), jnp.float32),
        'ffn_w_in': nrm(ks[17], (DEPTH, D_MODEL, 2 * FFN_HIDDEN), D_MODEL),
        'ffn_w_out': nrm(ks[18], (DEPTH, FFN_HIDDEN, D_MODEL), FFN_HIDDEN),
    }


def reference(x, c, ctx, c_ctx, ada_w, ada_b, norm_g, rec_w_in, rec_w_out, rec_lb_logits,
              rec_w_g2, rec_b_g2, rec_gn_a, rec_gn_b, att_w_qkv, att_w_o, att_sink,
              ffn_w_in, ffn_w_out):
    x_lat, x_ctx = x, ctx
    lbs = hgrn_lower_bounds(rec_lb_logits)
    cos, sin = axial_rope_tables(x.shape[1])
    s_lat = jax.nn.silu(c)
    s_ctx = jax.nn.silu(c_ctx)
    for l in range(DEPTH):
        need_ctx = l < DEPTH - 1
        ng = norm_g[l]
        ml = [m[:, None, :] for m in jnp.split(s_lat @ ada_w[l] + ada_b[l], 6, axis=-1)]
        mc = jnp.split(s_ctx @ ada_w[l] + ada_b[l], 6, axis=-1)
        u_lat = modulate(x_lat, ng[0], ml[0], ml[1])
        u_ctx = modulate(x_ctx, ng[0], mc[0], mc[1])
        j = l // 2
        if l % 2 == 0:
            y_lat, y_ctx = recurrent_mixer(u_lat, u_ctx, rec_w_in[j], rec_w_out[j], lbs[j],
                                           rec_w_g2[j], rec_b_g2[j], rec_gn_a[j], rec_gn_b[j], need_ctx)
        else:
            y_lat, y_ctx = attention_mixer(u_lat, u_ctx, att_w_qkv[j], att_w_o[j], att_sink[j],
                                           cos, sin, need_ctx)
        x_lat = x_lat + ml[2] * rms_norm(y_lat, ng[1])
        h_lat = swiglu(modulate(x_lat, ng[2], ml[3], ml[4]), ffn_w_in[l], ffn_w_out[l])
        x_lat = x_lat + ml[5] * rms_norm(h_lat, ng[3])
        if need_ctx:
            x_ctx = x_ctx + mc[2] * rms_norm(y_ctx, ng[1])
            h_ctx = swiglu(modulate(x_ctx, ng[2], mc[3], mc[4]), ffn_w_in[l], ffn_w_out[l])
            x_ctx = x_ctx + mc[5] * rms_norm(h_ctx, ng[3])
    return x_lat
```

```python
import functools

import numpy as np
import jax
import jax.numpy as jnp
from jax import lax
from jax.experimental import pallas as pl
from jax.experimental.pallas import tpu as pltpu

F32 = jnp.float32
BF16 = jnp.bfloat16
HIGHEST = lax.Precision.HIGHEST

EPS = 1e-6
LANES = 128
TOKEN_TILE = 256
SCAN_BLOCK = 256
LA_CHUNK = 32
ATT_BLOCK = 128
GRID_W = 64
ROPE_BASE = 10000.0
GLA_GATE_NORM = 16.0
A_HEADS, A_KEY, A_VAL = 4, 128, 128
B_HEADS, B_KEY, B_VAL = 4, 64, 128
B_GATE_RANK = 16
C_HEADS, C_KV_HEADS, C_HEAD_DIM = 16, 4, 64
VMEM_LIMIT = 56 * 1024 * 1024

NT = (((1,), (1,)), ((), ()))
TN = (((0,), (0,)), ((), ()))


def _sigmoid(x):
    return 1.0 / (1.0 + jnp.exp(-x))


def _silu(x):
    return x * _sigmoid(x)


def _log_sigmoid(x):
    return -(jnp.maximum(-x, 0.0) + jnp.log1p(jnp.exp(-jnp.abs(x))))


def _rms(x, g):
    ms = jnp.mean(x * x, axis=-1, keepdims=True)
    return (x * lax.rsqrt(ms + EPS)) * g


def _modulate(x, g, shift, scale):
    return _rms(x, g) * (1.0 + scale) + shift


def _const_spec(shape):
    nd = len(shape)
    return pl.BlockSpec(shape, lambda *_: (0,) * nd, pipeline_mode=pl.Buffered(1))


def _ada_kernel(c_ref, w_ref, b_ref, o_ref):
    s = _silu(c_ref[...])
    o_ref[0] = jnp.dot(s, w_ref[0], preferred_element_type=F32, precision=HIGHEST) + b_ref[0]


def _ada(cond, ada_w, ada_b):
    depth, d, n = ada_w.shape
    rows = cond.shape[0]
    tn = n // 4
    return pl.pallas_call(
        _ada_kernel,
        grid=(depth, n // tn),
        in_specs=[
            pl.BlockSpec((rows, d), lambda l, j: (0, 0)),
            pl.BlockSpec((1, d, tn), lambda l, j: (l, 0, j)),
            pl.BlockSpec((1, 1, tn), lambda l, j: (l, 0, j)),
        ],
        out_specs=pl.BlockSpec((1, rows, tn), lambda l, j: (l, 0, j)),
        out_shape=jax.ShapeDtypeStruct((depth, rows, n), F32),
        compiler_params=pltpu.CompilerParams(vmem_limit_bytes=VMEM_LIMIT),
        name="ada",
    )(cond, ada_w, ada_b.reshape(depth, 1, n))


def _recin_kernel(x_ref, mod_ref, ng_ref, w_ref, w2_ref, b2_ref, o_ref, *, n_main):
    m = mod_ref[0, 0]
    u = _modulate(x_ref[0], ng_ref[0:1], m[0:1], m[1:2]).astype(BF16)
    p = jnp.dot(u, w_ref[...], preferred_element_type=F32)
    lr = p[:, n_main:].astype(BF16)
    pre = jnp.dot(lr, w2_ref[...], preferred_element_type=F32) + b2_ref[...]
    o_ref[0, :, :n_main] = p[:, :n_main]
    o_ref[0, :, n_main:] = _log_sigmoid(pre) * (1.0 / GLA_GATE_NORM)


def _rec_in(xcat, mod, ng, w, w2, b2, n_ctx_tiles):
    bsz, t, d = xcat.shape
    n_in = w.shape[1]
    n_main = n_in - LANES
    n_out = n_main + w2.shape[1]
    tm = TOKEN_TILE
    return pl.pallas_call(
        functools.partial(_recin_kernel, n_main=n_main),
        grid=(bsz, t // tm),
        in_specs=[
            pl.BlockSpec((1, tm, d), lambda b, i: (b, i, 0)),
            pl.BlockSpec((1, 1, 6, d), lambda b, i: (b, (i >= n_ctx_tiles).astype(jnp.int32), 0, 0)),
            _const_spec(ng.shape),
            _const_spec(w.shape),
            _const_spec(w2.shape),
            _const_spec(b2.shape),
        ],
        out_specs=pl.BlockSpec((1, tm, n_out), lambda b, i: (b, i, 0)),
        out_shape=jax.ShapeDtypeStruct((bsz, t, n_out), F32),
        compiler_params=pltpu.CompilerParams(vmem_limit_bytes=VMEM_LIMIT),
        name="rec_in",
    )(xcat, mod, ng, w, w2, b2)


def _scan_direction(q, k, g, v_bf, states, tri_incl, tri_rest, cmask, lane_masks, reverse):
    rows = q.shape[0]
    nchunk = rows // LA_CHUNK
    nsub = len(states)
    bc = jnp.dot(tri_incl, g, preferred_element_type=F32, precision=HIGHEST)
    rest = jnp.dot(tri_rest, g, preferred_element_type=F32, precision=HIGHEST)
    q_dec = q * jnp.exp(bc)
    k_inv = (k * jnp.exp(-bc)).astype(BF16)
    k_end = (k * jnp.exp(rest)).astype(BF16)
    states = list(states)
    outs = [[None] * nchunk for _ in range(nsub)]
    order = range(nchunk - 1, -1, -1) if reverse else range(nchunk)
    for n in order:
        r0 = n * LA_CHUNK
        rs = slice(r0, r0 + LA_CHUNK)
        d_n = jnp.exp(bc[r0:r0 + 1] + rest[r0:r0 + 1])
        ki_n, ke_n = k_inv[rs], k_end[rs]
        for j in range(nsub):
            qd = q_dec[rs]
            if lane_masks is not None:
                qd = jnp.where(lane_masks[j], qd, 0.0)
            qd = qd.astype(BF16)
            v_n = v_bf[rs, j * LANES:(j + 1) * LANES]
            sc = lax.dot_general(qd, ki_n, NT, preferred_element_type=F32)
            sc = jnp.where(cmask, sc, 0.0).astype(BF16)
            o = jnp.dot(sc, v_n, preferred_element_type=F32)
            o = o + lax.dot_general(qd, states[j].astype(BF16), NT, preferred_element_type=F32)
            states[j] = states[j] * d_n + lax.dot_general(v_n, ke_n, TN, preferred_element_type=F32)
            outs[j][n] = o
    return [jnp.concatenate(o, axis=0) for o in outs], tuple(states)


def _scan_kernel(*refs, mode, n_ctx_blocks, layer_slot):
    if mode == "hgrn":
        (q_ref, zf_ref, zb_ref, v_ref, og_ref, lb_ref, gn_ref, tri_ref, o_ref, of_ref, ob_ref) = refs
        nsub = 1
    else:
        (q_ref, k_ref, gf_ref, gb_ref, v_ref, og_ref, gn_ref, tri_ref, o_ref, of_ref, ob_ref) = refs
        nsub = 2
    t = q_ref.shape[1]
    blk = SCAN_BLOCK
    nblk = t // blk
    c = LA_CHUNK
    ri = lax.broadcasted_iota(jnp.int32, (c, c), 0)
    ci = lax.broadcasted_iota(jnp.int32, (c, c), 1)
    cmask_f, cmask_b = ci <= ri, ci >= ri
    if nsub == 2:
        lane = lax.broadcasted_iota(jnp.int32, (c, LANES), 1)
        lane_masks = [lane < B_KEY, lane >= B_KEY]
    else:
        lane_masks = None

    if mode == "hgrn":
        lg = lb_ref[...]
        e = jnp.exp(lg - jnp.max(lg, axis=0, keepdims=True))
        p = e / jnp.sum(e, axis=0, keepdims=True)
        lb = jnp.sum(p[:layer_slot + 1], axis=0)

    def features(rows, d):
        q_raw = q_ref[0, rows, :]
        if mode == "hgrn":
            z = (zf_ref, zb_ref)[d][0, rows, :]
            lbd = lb[d:d + 1]
            f = lbd + (1.0 - lbd) * _sigmoid(z)
            return _silu(q_raw) * (A_KEY ** -0.5), 1.0 - f, jnp.log(f)
        return q_raw * (B_KEY ** -0.5), k_ref[0, rows, :], (gf_ref, gb_ref)[d][0, rows, :]

    def body(i, carry):
        st_f, st_b = carry
        jb = jnp.where(i < n_ctx_blocks, n_ctx_blocks - 1 - i, nblk - 1 - (i - n_ctx_blocks))
        rows_f = pl.ds(pl.multiple_of(i * blk, blk), blk)
        rows_b = pl.ds(pl.multiple_of(jb * blk, blk), blk)
        q, k, g = features(rows_f, 0)
        o, st_f = _scan_direction(q, k, g, v_ref[0, rows_f, :].astype(BF16), st_f,
                                  tri_ref[0], tri_ref[1], cmask_f, lane_masks, False)
        for j in range(nsub):
            of_ref[rows_f, j * LANES:(j + 1) * LANES] = o[j]
        q, k, g = features(rows_b, 1)
        o, st_b = _scan_direction(q, k, g, v_ref[0, rows_b, :].astype(BF16), st_b,
                                  tri_ref[2], tri_ref[3], cmask_b, lane_masks, True)
        for j in range(nsub):
            ob_ref[rows_b, j * LANES:(j + 1) * LANES] = o[j]
        return st_f, st_b

    zero = tuple(jnp.zeros((LANES, LANES), F32) for _ in range(nsub))
    lax.fori_loop(0, nblk, body, (zero, zero))

    def finish(i, _):
        rows = pl.ds(pl.multiple_of(i * blk, blk), blk)
        for j in range(nsub):
            cols = slice(j * LANES, (j + 1) * LANES)
            o = of_ref[rows, cols] + ob_ref[rows, cols]
            y = _rms(o, gn_ref[...]) * _silu(og_ref[0, rows, cols])
            o_ref[0, rows, cols] = y.astype(o_ref.dtype)
        return 0

    lax.fori_loop(0, nblk, finish, 0)


def _scan_tri():
    r = np.arange(SCAN_BLOCK)
    same = (r[:, None] // LA_CHUNK) == (r[None, :] // LA_CHUNK)
    lower_incl = same & (r[None, :] <= r[:, None])
    upper_strict = same & (r[None, :] > r[:, None])
    upper_incl = same & (r[None, :] >= r[:, None])
    lower_strict = same & (r[None, :] < r[:, None])
    return jnp.asarray(np.stack([lower_incl, upper_strict, upper_incl, lower_strict]).astype(np.float32))


def _scan_hgrn(p, lb_logits, gn, tri, n_ctx_blocks, layer_slot):
    bsz, t, _ = p.shape
    col = lambda base: pl.BlockSpec((1, t, LANES), lambda b, h, base=base: (b, 0, base + h))
    return pl.pallas_call(
        functools.partial(_scan_kernel, mode="hgrn", n_ctx_blocks=n_ctx_blocks, layer_slot=layer_slot),
        grid=(bsz, A_HEADS),
        in_specs=[col(0), col(4), col(8), col(12), col(16),
                  pl.BlockSpec((lb_logits.shape[0], 2, LANES), lambda b, h: (0, 0, h)),
                  _const_spec(gn.shape), _const_spec(tri.shape)],
        out_specs=pl.BlockSpec((1, t, LANES), lambda b, h: (b, 0, h)),
        out_shape=jax.ShapeDtypeStruct((bsz, t, A_HEADS * A_VAL), BF16),
        scratch_shapes=[pltpu.VMEM((t, LANES), F32), pltpu.VMEM((t, LANES), F32)],
        compiler_params=pltpu.CompilerParams(vmem_limit_bytes=VMEM_LIMIT),
        name="scan_hgrn",
    )(p, p, p, p, p, lb_logits, gn, tri)


def _scan_gla(p, gn, tri, n_ctx_blocks):
    bsz, t, _ = p.shape
    col = lambda base: pl.BlockSpec((1, t, LANES), lambda b, h, base=base: (b, 0, base + h))
    wide = lambda base: pl.BlockSpec((1, t, 2 * LANES), lambda b, h, base=base: (b, 0, base + h))
    return pl.pallas_call(
        functools.partial(_scan_kernel, mode="gla", n_ctx_blocks=n_ctx_blocks, layer_slot=0),
        grid=(bsz, B_HEADS // 2),
        in_specs=[col(20), col(22), col(32), col(34), wide(12), wide(14),
                  _const_spec(gn.shape), _const_spec(tri.shape)],
        out_specs=pl.BlockSpec((1, t, 2 * LANES), lambda b, h: (b, 0, h)),
        out_shape=jax.ShapeDtypeStruct((bsz, t, B_HEADS * B_VAL), BF16),
        scratch_shapes=[pltpu.VMEM((t, 2 * LANES), F32), pltpu.VMEM((t, 2 * LANES), F32)],
        compiler_params=pltpu.CompilerParams(vmem_limit_bytes=VMEM_LIMIT),
        name="scan_gla",
    )(p, p, p, p, p, p, gn, tri)


def _post_kernel(y0_ref, y1_ref, x_ref, mod_ref, ng_ref, wo_ref, win_ref, wout_ref, o_ref):
    m = mod_ref[0, 0]
    half = y0_ref.shape[2]
    hid = wout_ref.shape[0]
    t = jnp.dot(y0_ref[0], wo_ref[:half, :], preferred_element_type=F32)
    t = t + jnp.dot(y1_ref[0], wo_ref[half:, :], preferred_element_type=F32)
    x1 = x_ref[0] + m[2:3] * _rms(t, ng_ref[1:2])
    u = _modulate(x1, ng_ref[2:3], m[3:4], m[4:5]).astype(BF16)
    h = jnp.dot(u, win_ref[...], preferred_element_type=F32)
    act = (_silu(h[:, :hid]) * h[:, hid:]).astype(BF16)
    h2 = jnp.dot(act, wout_ref[...], preferred_element_type=F32)
    o_ref[0] = x1 + m[5:6] * _rms(h2, ng_ref[3:4])


def _post(y0, y1, y_col0, y_col1, xcat, x_tile0, n_tiles, mod, n_ctx_tiles, ng, wo, win, wout):
    bsz, _, d = xcat.shape
    tm = TOKEN_TILE
    half = d // 2
    return pl.pallas_call(
        _post_kernel,
        grid=(bsz, n_tiles),
        in_specs=[
            pl.BlockSpec((1, tm, half), lambda b, i: (b, i, y_col0)),
            pl.BlockSpec((1, tm, half), lambda b, i: (b, i, y_col1)),
            pl.BlockSpec((1, tm, d), lambda b, i: (b, i + x_tile0, 0)),
            pl.BlockSpec((1, 1, 6, d),
                         lambda b, i: (b, (i + x_tile0 >= n_ctx_tiles).astype(jnp.int32), 0, 0)),
            _const_spec(ng.shape), _const_spec(wo.shape), _const_spec(win.shape), _const_spec(wout.shape),
        ],
        out_specs=pl.BlockSpec((1, tm, d), lambda b, i: (b, i, 0)),
        out_shape=jax.ShapeDtypeStruct((bsz, n_tiles * tm, d), F32),
        compiler_params=pltpu.CompilerParams(vmem_limit_bytes=VMEM_LIMIT),
        name="post",
    )(y0, y1, xcat, mod, ng, wo, win, wout)


def _rope(x, cos, sin_signed, first_half):
    partner = jnp.where(first_half, pltpu.roll(x, LANES - 32, 1), pltpu.roll(x, 32, 1))
    return x * cos + partner * sin_signed


def _qkv_kernel(x_ref, mod_ref, ng_ref, w_ref, cos_ref, sin_ref, q_ref, k_ref, v_ref):
    m = mod_ref[0, 0]
    u = _modulate(x_ref[0], ng_ref[0:1], m[0:1], m[1:2]).astype(BF16)
    p = jnp.dot(u, w_ref[...], preferred_element_type=F32)
    nq, nk = q_ref.shape[2], k_ref.shape[2]
    cos, sin = cos_ref[...], sin_ref[...]
    lane = lax.broadcasted_iota(jnp.int32, cos.shape, 1)
    first_half = (lane % C_HEAD_DIM) < (C_HEAD_DIM // 2)
    for j in range(nq // LANES):
        cols = slice(j * LANES, (j + 1) * LANES)
        q_ref[0, :, cols] = (_rope(p[:, cols], cos, sin, first_half) * (C_HEAD_DIM ** -0.5)).astype(BF16)
    for j in range(nk // LANES):
        cols = slice(nq + j * LANES, nq + (j + 1) * LANES)
        k_ref[0, :, j * LANES:(j + 1) * LANES] = _rope(p[:, cols], cos, sin, first_half).astype(BF16)
    v_ref[0] = p[:, nq + nk:].astype(BF16)


def _qkv(xcat, mod, ng, w, cos, sin, n_ctx_tiles):
    bsz, t, d = xcat.shape
    tm = TOKEN_TILE
    nq = C_HEADS * C_HEAD_DIM
    nk = C_KV_HEADS * LANES
    return pl.pallas_call(
        _qkv_kernel,
        grid=(bsz, t // tm),
        in_specs=[
            pl.BlockSpec((1, tm, d), lambda b, i: (b, i, 0)),
            pl.BlockSpec((1, 1, 6, d), lambda b, i: (b, (i >= n_ctx_tiles).astype(jnp.int32), 0, 0)),
            _const_spec(ng.shape), _const_spec(w.shape),
            pl.BlockSpec((tm, LANES), lambda b, i: (i, 0)),
            pl.BlockSpec((tm, LANES), lambda b, i: (i, 0)),
        ],
        out_specs=[
            pl.BlockSpec((1, tm, nq), lambda b, i: (b, i, 0)),
            pl.BlockSpec((1, tm, nk), lambda b, i: (b, i, 0)),
            pl.BlockSpec((1, tm, nk), lambda b, i: (b, i, 0)),
        ],
        out_shape=[jax.ShapeDtypeStruct((bsz, t, nq), BF16),
                   jax.ShapeDtypeStruct((bsz, t, nk), BF16),
                   jax.ShapeDtypeStruct((bsz, t, nk), BF16)],
        compiler_params=pltpu.CompilerParams(vmem_limit_bytes=VMEM_LIMIT),
        name="qkv",
    )(xcat, mod, ng, w, cos, sin)


def _attn_kernel(sink_ref, q_ref, k_ref, v_ref, o_ref, *, n_ctx, n_blocks):
    n = pl.program_id(1)
    blk = ATT_BLOCK
    t = k_ref.shape[1]
    group = C_HEADS // C_KV_HEADS
    qb = q_ref[0]
    r_own = n_ctx + n * blk
    r_prev = pl.multiple_of(r_own - blk, blk)
    r_next = pl.multiple_of(jnp.minimum(r_own + blk, t - blk), blk)
    r_own = pl.multiple_of(r_own, blk)
    rows = group * blk
    qi = lax.broadcasted_iota(jnp.int32, (rows, blk), 0) % blk
    kj = lax.broadcasted_iota(jnp.int32, (rows, blk), 1)
    mask_prev = (kj >= qi) & (n > 0)
    mask_next = (kj <= qi) & (n < n_blocks - 1)
    lane = lax.broadcasted_iota(jnp.int32, (blk, LANES), 1)
    low = lane < C_HEAD_DIM
    zero = jnp.zeros((blk, LANES), BF16)
    for kg in range(C_KV_HEADS):
        cols = slice(kg * LANES, (kg + 1) * LANES)
        parts, sinks = [], []
        for hh in range(group):
            h = kg * group + hh
            pair = qb[:, (h // 2) * LANES:(h // 2 + 1) * LANES]
            parts.append(jnp.where(low if h % 2 == 0 else ~low, pair, zero))
            sinks.append(jnp.full((blk, 1), sink_ref[h], F32))
        q4 = jnp.concatenate(parts, axis=0)
        sink = jnp.concatenate(sinks, axis=0)
        s_prev = lax.dot_general(q4, k_ref[0, pl.ds(r_prev, blk), cols], NT, preferred_element_type=F32)
        s_own = lax.dot_general(q4, k_ref[0, pl.ds(r_own, blk), cols], NT, preferred_element_type=F32)
        s_next = lax.dot_general(q4, k_ref[0, pl.ds(r_next, blk), cols], NT, preferred_element_type=F32)
        s_ctx = lax.dot_general(q4, k_ref[0, 0:n_ctx, cols], NT, preferred_element_type=F32)
        s = jnp.concatenate([jnp.where(mask_prev, s_prev, -1e30), s_own,
                             jnp.where(mask_next, s_next, -1e30), s_ctx], axis=1)
        mx = jnp.maximum(jnp.max(s, axis=1, keepdims=True), sink)
        p = jnp.exp(s - mx)
        denom = jnp.sum(p, axis=1, keepdims=True) + jnp.exp(sink - mx)
        vv = jnp.concatenate([v_ref[0, pl.ds(r_prev, blk), cols], v_ref[0, pl.ds(r_own, blk), cols],
                              v_ref[0, pl.ds(r_next, blk), cols], v_ref[0, 0:n_ctx, cols]], axis=0)
        o = jnp.dot(p.astype(BF16), vv, preferred_element_type=F32) / denom
        for pp in range(group // 2):
            even = o[(2 * pp) * blk:(2 * pp + 1) * blk]
            odd = o[(2 * pp + 1) * blk:(2 * pp + 2) * blk]
            c0 = (kg * (group // 2) + pp) * LANES
            o_ref[0, :, c0:c0 + LANES] = jnp.where(low, even, odd).astype(o_ref.dtype)


def _attention(q, k2, v2, sink, n_ctx):
    bsz, t, nq = q.shape
    nk = k2.shape[2]
    blk = ATT_BLOCK
    n_blocks = (t - n_ctx) // blk
    ctx_blocks = n_ctx // blk
    return pl.pallas_call(
        functools.partial(_attn_kernel, n_ctx=n_ctx, n_blocks=n_blocks),
        grid=(bsz, n_blocks),
        in_specs=[
            pl.BlockSpec(memory_space=pltpu.SMEM),
            pl.BlockSpec((1, blk, nq), lambda b, n: (b, n + ctx_blocks, 0)),
            pl.BlockSpec((1, t, nk), lambda b, n: (b, 0, 0)),
            pl.BlockSpec((1, t, nk), lambda b, n: (b, 0, 0)),
        ],
        out_specs=pl.BlockSpec((1, blk, nq), lambda b, n: (b, n, 0)),
        out_shape=jax.ShapeDtypeStruct((bsz, t - n_ctx, nq), BF16),
        compiler_params=pltpu.CompilerParams(vmem_limit_bytes=VMEM_LIMIT),
        name="attention",
    )(sink, q, k2, v2)


def _rec_weights(w_in, w_g2, b_g2):
    widths = (A_HEADS * A_KEY,) * 3 + (A_HEADS * A_VAL,) * 2 + (B_HEADS * B_KEY,) * 2 + (
        B_HEADS * B_VAL, B_GATE_RANK, B_GATE_RANK, B_HEADS * B_VAL)
    off = np.concatenate([[0], np.cumsum(widths)])
    seg = lambda i: w_in[:, off[i]:off[i + 1]]
    d = w_in.shape[0]
    pad = jnp.zeros((d, LANES - 2 * B_GATE_RANK), w_in.dtype)
    w = jnp.concatenate([seg(i) for i in (0, 1, 2, 3, 4, 5, 6, 7, 10, 8, 9)] + [pad], axis=1).astype(BF16)
    nk = B_HEADS * B_KEY
    w2 = jnp.zeros((LANES, 2 * nk), F32)
    w2 = w2.at[:B_GATE_RANK, :nk].set(w_g2[0]).at[B_GATE_RANK:2 * B_GATE_RANK, nk:].set(w_g2[1])
    return w, w2.astype(BF16), b_g2.reshape(1, 2 * nk)


def _att_weights(w_qkv):
    d = w_qkv.shape[0]
    nq, nkv = C_HEADS * C_HEAD_DIM, C_KV_HEADS * C_HEAD_DIM
    perm = np.concatenate([np.arange(0, C_HEAD_DIM, 2), np.arange(1, C_HEAD_DIM, 2)])
    wq = w_qkv[:, :nq].reshape(d, C_HEADS, C_HEAD_DIM)[:, :, perm].reshape(d, nq)
    wk = w_qkv[:, nq:nq + nkv].reshape(d, C_KV_HEADS, C_HEAD_DIM)[:, :, perm]
    wv = w_qkv[:, nq + nkv:].reshape(d, C_KV_HEADS, C_HEAD_DIM)
    wk = jnp.concatenate([wk, wk], axis=2).reshape(d, 2 * nkv)
    wv = jnp.concatenate([wv, wv], axis=2).reshape(d, 2 * nkv)
    return jnp.concatenate([wq, wk, wv], axis=1).astype(BF16)


def _rope_tables(seq, n_ctx):
    n_rows = seq // GRID_W
    row = jnp.repeat(jnp.arange(n_rows), GRID_W).astype(F32)
    col = jnp.tile(jnp.arange(GRID_W), n_rows).astype(F32)
    half = C_HEAD_DIM // 2
    inv = ROPE_BASE ** (-jnp.arange(0, half, 2, dtype=F32) / half)
    ang = jnp.concatenate([row[:, None] * inv, col[:, None] * inv], axis=-1)
    cos, sin = jnp.cos(ang), jnp.sin(ang)
    cos = jnp.concatenate([jnp.ones((n_ctx, half), F32), cos], axis=0)
    sin = jnp.concatenate([jnp.zeros((n_ctx, half), F32), sin], axis=0)
    reps = LANES // C_HEAD_DIM
    return (jnp.tile(jnp.concatenate([cos, cos], axis=1), (1, reps)),
            jnp.tile(jnp.concatenate([-sin, sin], axis=1), (1, reps)))


def kernel(x, c, ctx, c_ctx, ada_w, ada_b, norm_g, rec_w_in, rec_w_out, rec_lb_logits, rec_w_g2, rec_b_g2,
           rec_gn_a, rec_gn_b, att_w_qkv, att_w_o, att_sink, ffn_w_in, ffn_w_out):
    bsz, seq, d = x.shape
    n_ctx = ctx.shape[1]
    depth = ada_w.shape[0]
    tm = TOKEN_TILE
    assert n_ctx % tm == 0 and seq % tm == 0 and n_ctx % SCAN_BLOCK == 0 and seq % SCAN_BLOCK == 0
    n_ctx_tiles = n_ctx // tm
    n_lat_tiles = seq // tm

    rows = -(-(bsz + 1) // 8) * 8
    cond = jnp.concatenate([c, c_ctx[None], jnp.zeros((rows - bsz - 1, d), F32)], axis=0)
    mod_all = _ada(cond, ada_w, ada_b)
    mods = []
    for l in range(depth):
        lat = mod_all[l, :bsz].reshape(bsz, 1, 6, d)
        cx = jnp.broadcast_to(mod_all[l, bsz].reshape(1, 1, 6, d), (bsz, 1, 6, d))
        mods.append(jnp.concatenate([cx, lat], axis=1))

    xcat = jnp.concatenate([ctx, x], axis=1)
    tri = _scan_tri()
    cos, sin = _rope_tables(seq, n_ctx)

    for l in range(depth):
        need_ctx = l < depth - 1
        j = l // 2
        ng = norm_g[l]
        if l % 2 == 0:
            w, w2, b2 = _rec_weights(rec_w_in[j], rec_w_g2[j], rec_b_g2[j])
            p = _rec_in(xcat, mods[l], ng, w, w2, b2, n_ctx_tiles)
            ya = _scan_hgrn(p, rec_lb_logits, rec_gn_a[j].reshape(1, -1), tri, n_ctx // SCAN_BLOCK, j)
            yb = _scan_gla(p, rec_gn_b[j].reshape(1, -1), tri, n_ctx // SCAN_BLOCK)
            y0, y1, c0, c1, wo = ya, yb, 0, 0, rec_w_out[j]
            y_tile0 = 0
        else:
            q, k2, v2 = _qkv(xcat, mods[l], ng, _att_weights(att_w_qkv[j]), cos, sin, n_ctx_tiles)
            y = _attention(q, k2, v2, att_sink[j], n_ctx)
            y0, y1, c0, c1, wo = y, y, 0, 1, att_w_o[j]
            y_tile0 = n_ctx_tiles
        wo, win, wout = wo.astype(BF16), ffn_w_in[l].astype(BF16), ffn_w_out[l].astype(BF16)
        if need_ctx:
            assert y_tile0 == 0
            xcat = _post(y0, y1, c0, c1, xcat, 0, n_ctx_tiles + n_lat_tiles, mods[l], n_ctx_tiles,
                         ng, wo, win, wout)
        else:
            if y_tile0 == 0:
                y0 = y0[:, n_ctx:]
                y1 = y1[:, n_ctx:]
            x_lat = _post(y0, y1, c0, c1, xcat, n_ctx_tiles, n_lat_tiles, mods[l], n_ctx_tiles,
                          ng, wo, win, wout)
    return x_lat
```

```python
import functools

import numpy as np
import jax
import jax.numpy as jnp
from jax import lax
from jax.experimental import pallas as pl
from jax.experimental.pallas import tpu as pltpu

F32 = jnp.float32
BF16 = jnp.bfloat16
HIGHEST = lax.Precision.HIGHEST

EPS = 1e-6
LANES = 128
TOKEN_TILE = 256
SCAN_BLOCK = 256
LA_CHUNK = 32
SCAN_GROUP = 4 * LA_CHUNK
ATT_BLOCK = 128
GRID_W = 64
ROPE_BASE = 10000.0
GLA_GATE_NORM = 16.0
A_HEADS, A_KEY, A_VAL = 4, 128, 128
B_HEADS, B_KEY, B_VAL = 4, 64, 128
B_GATE_RANK = 16
C_HEADS, C_KV_HEADS, C_HEAD_DIM = 16, 4, 64
VMEM_LIMIT = 56 * 1024 * 1024

NT = (((1,), (1,)), ((), ()))
TN = (((0,), (0,)), ((), ()))


def _sigmoid(x):
    return 1.0 / (1.0 + jnp.exp(-x))


def _silu(x):
    return x * _sigmoid(x)


def _log_sigmoid(x):
    return -(jnp.maximum(-x, 0.0) + jnp.log1p(jnp.exp(-jnp.abs(x))))


def _rms(x, g):
    ms = jnp.mean(x * x, axis=-1, keepdims=True)
    return (x * lax.rsqrt(ms + EPS)) * g


def _modulate(x, g, shift, scale):
    return _rms(x, g) * (1.0 + scale) + shift


def _const_spec(shape):
    nd = len(shape)
    return pl.BlockSpec(shape, lambda *_: (0,) * nd, pipeline_mode=pl.Buffered(1))


def _ada_kernel(c_ref, w_ref, b_ref, o_ref):
    s = _silu(c_ref[...])
    o_ref[0] = jnp.dot(s, w_ref[0], preferred_element_type=F32, precision=HIGHEST) + b_ref[0]


def _ada(cond, ada_w, ada_b):
    depth, d, n = ada_w.shape
    rows = cond.shape[0]
    tn = n // 4
    return pl.pallas_call(
        _ada_kernel,
        grid=(depth, n // tn),
        in_specs=[
            pl.BlockSpec((rows, d), lambda l, j: (0, 0)),
            pl.BlockSpec((1, d, tn), lambda l, j: (l, 0, j)),
            pl.BlockSpec((1, 1, tn), lambda l, j: (l, 0, j)),
        ],
        out_specs=pl.BlockSpec((1, rows, tn), lambda l, j: (l, 0, j)),
        out_shape=jax.ShapeDtypeStruct((depth, rows, n), F32),
        compiler_params=pltpu.CompilerParams(vmem_limit_bytes=VMEM_LIMIT),
        name="ada",
    )(cond, ada_w, ada_b.reshape(depth, 1, n))


def _recin_kernel(x_ref, mod_ref, ng_ref, w_ref, w2_ref, b2_ref, o_ref, *, n_main):
    m = mod_ref[0, 0]
    u = _modulate(x_ref[0], ng_ref[0:1], m[0:1], m[1:2]).astype(BF16)
    p = jnp.dot(u, w_ref[...], preferred_element_type=F32)
    lr = p[:, n_main:].astype(BF16)
    pre = jnp.dot(lr, w2_ref[...], preferred_element_type=F32) + b2_ref[...]
    o_ref[0, :, :n_main] = p[:, :n_main]
    o_ref[0, :, n_main:] = _log_sigmoid(pre) * (1.0 / GLA_GATE_NORM)


def _rec_in(xcat, mod, ng, w, w2, b2, n_ctx_tiles):
    bsz, t, d = xcat.shape
    n_in = w.shape[1]
    n_main = n_in - LANES
    n_out = n_main + w2.shape[1]
    tm = TOKEN_TILE
    return pl.pallas_call(
        functools.partial(_recin_kernel, n_main=n_main),
        grid=(bsz, t // tm),
        in_specs=[
            pl.BlockSpec((1, tm, d), lambda b, i: (b, i, 0)),
            pl.BlockSpec((1, 1, 6, d), lambda b, i: (b, (i >= n_ctx_tiles).astype(jnp.int32), 0, 0)),
            _const_spec(ng.shape),
            _const_spec(w.shape),
            _const_spec(w2.shape),
            _const_spec(b2.shape),
        ],
        out_specs=pl.BlockSpec((1, tm, n_out), lambda b, i: (b, i, 0)),
        out_shape=jax.ShapeDtypeStruct((bsz, t, n_out), F32),
        compiler_params=pltpu.CompilerParams(vmem_limit_bytes=VMEM_LIMIT),
        name="rec_in",
    )(xcat, mod, ng, w, w2, b2)


def _chunk_cumsum(g, reverse):
    r = g.shape[0]
    sl = 8
    x = g.reshape(r // sl, sl, LANES)
    sub = lax.broadcasted_iota(jnp.int32, x.shape, 1)
    for s in (1, 2, 4):
        if reverse:
            x = x + jnp.where(sub < sl - s, pltpu.roll(x, sl - s, 1), 0.0)
        else:
            x = x + jnp.where(sub >= s, pltpu.roll(x, s, 1), 0.0)
    per = LA_CHUNK // sl
    x = x.reshape(r // LA_CHUNK, per, sl, LANES)
    parts = [None] * per
    carry = None
    for v in (range(per - 1, -1, -1) if reverse else range(per)):
        xv = x[:, v:v + 1]
        parts[v] = xv if carry is None else xv + carry
        edge = xv[:, :, 0:1] if reverse else xv[:, :, sl - 1:sl]
        carry = edge if carry is None else carry + edge
    return jnp.concatenate(parts, axis=1).reshape(r, LANES)


def _group_masks(reverse):
    n = SCAN_GROUP
    ri = lax.broadcasted_iota(jnp.int32, (n, n), 0)
    ci = lax.broadcasted_iota(jnp.int32, (n, n), 1)
    rc, cc = ri // LA_CHUNK, ci // LA_CHUNK
    if reverse:
        return (rc == cc) & (ci >= ri), (rc + 1 == cc) & (rc % 2 == 0), (rc < 2) & (cc >= 2)
    return (rc == cc) & (ci <= ri), (rc == cc + 1) & (rc % 2 == 1), (rc >= 2) & (cc < 2)


def _scale_chunks(x, factors):
    parts = []
    for c, f in enumerate(factors):
        xc = x[c * LA_CHUNK:(c + 1) * LA_CHUNK]
        parts.append(xc if f is None else xc * f)
    return jnp.concatenate(parts, axis=0)


def _scan_direction(q, k, g, v_bf, states, masks, lane_masks, reverse):
    rows = q.shape[0]
    nsub = len(states)
    nch = SCAN_GROUP // LA_CHUNK
    assert nch == 4
    ngroup = rows // SCAN_GROUP
    m_diag, m_adj, m_far = masks
    bc = _chunk_cumsum(g, reverse)
    q_dec = q * jnp.exp(bc)
    k_inv = k * jnp.exp(-bc)
    states = list(states)
    outs = [[None] * ngroup for _ in range(nsub)]
    pos = (lambda c: nch - 1 - c) if reverse else (lambda c: c)
    edge = 0 if reverse else LA_CHUNK - 1
    for gi in (range(ngroup - 1, -1, -1) if reverse else range(ngroup)):
        r0 = gi * SCAN_GROUP
        rs = slice(r0, r0 + SCAN_GROUP)
        tot = [bc[r0 + c * LA_CHUNK + edge:r0 + c * LA_CHUNK + edge + 1] for c in range(nch)]
        ts = [tot[pos(s)] for s in range(nch)]
        e_ts = [jnp.exp(t) for t in ts]
        e_pre = [None, e_ts[0], jnp.exp(ts[0] + ts[1]), jnp.exp(ts[0] + ts[1] + ts[2])]
        e_post = [jnp.exp(ts[1] + ts[2] + ts[3]), jnp.exp(ts[2] + ts[3]), e_ts[3], None]
        e_total = jnp.exp(ts[0] + ts[1] + ts[2] + ts[3])
        ki = k_inv[rs]
        ke = _scale_chunks(ki, [e_ts[pos(c)] for c in range(nch)])
        kh = _scale_chunks(ke, [e_post[pos(c)] for c in range(nch)])
        kb = _scale_chunks(ke, [e_ts[1] if pos(c) == 0 else None for c in range(nch)])
        kcat = jnp.concatenate([ki.astype(BF16), ke.astype(BF16)], axis=0)
        kb, kh = kb.astype(BF16), kh.astype(BF16)
        for j in range(nsub):
            qd = q_dec[rs]
            if lane_masks is not None:
                qd = jnp.where(lane_masks[j], qd, 0.0)
            qt = _scale_chunks(qd, [e_pre[pos(c)] for c in range(nch)]).astype(BF16)
            qb = _scale_chunks(qd, [e_ts[2] if pos(c) == 3 else None for c in range(nch)]).astype(BF16)
            qd = qd.astype(BF16)
            v_g = v_bf[rs, j * LANES:(j + 1) * LANES]
            s12 = lax.dot_general(qd, kcat, NT, preferred_element_type=F32)
            s3 = lax.dot_general(qb, kb, NT, preferred_element_type=F32)
            sc = jnp.where(m_diag, s12[:, :SCAN_GROUP],
                           jnp.where(m_adj, s12[:, SCAN_GROUP:], jnp.where(m_far, s3, 0.0)))
            o = jnp.dot(sc.astype(BF16), v_g, preferred_element_type=F32)
            o = o + lax.dot_general(qt, states[j].astype(BF16), NT, preferred_element_type=F32)
            states[j] = states[j] * e_total + lax.dot_general(v_g, kh, TN, preferred_element_type=F32)
            outs[j][gi] = o
    return [jnp.concatenate(o, axis=0) for o in outs], tuple(states)


def _scan_kernel(*refs, mode, n_ctx_blocks, layer_slot):
    if mode == "hgrn":
        (q_ref, zf_ref, zb_ref, v_ref, og_ref, lb_ref, gn_ref, o_ref, of_ref, ob_ref) = refs
        nsub = 1
    else:
        (q_ref, k_ref, gf_ref, gb_ref, v_ref, og_ref, gn_ref, o_ref, of_ref, ob_ref) = refs
        nsub = 2
    t = q_ref.shape[1]
    blk = SCAN_BLOCK
    nblk = t // blk
    masks_f, masks_b = _group_masks(False), _group_masks(True)
    if nsub == 2:
        lane = lax.broadcasted_iota(jnp.int32, (SCAN_GROUP, LANES), 1)
        lane_masks = [lane < B_KEY, lane >= B_KEY]
    else:
        lane_masks = None

    if mode == "hgrn":
        lg = lb_ref[...]
        e = jnp.exp(lg - jnp.max(lg, axis=0, keepdims=True))
        p = e / jnp.sum(e, axis=0, keepdims=True)
        lb = jnp.sum(p[:layer_slot + 1], axis=0)

    def features(rows, d):
        q_raw = q_ref[0, rows, :]
        if mode == "hgrn":
            z = (zf_ref, zb_ref)[d][0, rows, :]
            lbd = lb[d:d + 1]
            f = lbd + (1.0 - lbd) * _sigmoid(z)
            return _silu(q_raw) * (A_KEY ** -0.5), 1.0 - f, jnp.log(f)
        return q_raw * (B_KEY ** -0.5), k_ref[0, rows, :], (gf_ref, gb_ref)[d][0, rows, :]

    def body(i, carry):
        st_f, st_b = carry
        jb = jnp.where(i < n_ctx_blocks, n_ctx_blocks - 1 - i, nblk - 1 - (i - n_ctx_blocks))
        rows_f = pl.ds(pl.multiple_of(i * blk, blk), blk)
        rows_b = pl.ds(pl.multiple_of(jb * blk, blk), blk)
        q, k, g = features(rows_f, 0)
        o, st_f = _scan_direction(q, k, g, v_ref[0, rows_f, :].astype(BF16), st_f,
                                  masks_f, lane_masks, False)
        for j in range(nsub):
            of_ref[rows_f, j * LANES:(j + 1) * LANES] = o[j]
        q, k, g = features(rows_b, 1)
        o, st_b = _scan_direction(q, k, g, v_ref[0, rows_b, :].astype(BF16), st_b,
                                  masks_b, lane_masks, True)
        for j in range(nsub):
            ob_ref[rows_b, j * LANES:(j + 1) * LANES] = o[j]
        return st_f, st_b

    zero = tuple(jnp.zeros((LANES, LANES), F32) for _ in range(nsub))
    lax.fori_loop(0, nblk, body, (zero, zero))

    def finish(i, _):
        rows = pl.ds(pl.multiple_of(i * blk, blk), blk)
        for j in range(nsub):
            cols = slice(j * LANES, (j + 1) * LANES)
            o = of_ref[rows, cols] + ob_ref[rows, cols]
            y = _rms(o, gn_ref[...]) * _silu(og_ref[0, rows, cols])
            o_ref[0, rows, cols] = y.astype(o_ref.dtype)
        return 0

    lax.fori_loop(0, nblk, finish, 0)


def _scan_hgrn(p, lb_logits, gn, n_ctx_blocks, layer_slot):
    bsz, t, _ = p.shape
    col = lambda base: pl.BlockSpec((1, t, LANES), lambda b, h, base=base: (b, 0, base + h))
    return pl.pallas_call(
        functools.partial(_scan_kernel, mode="hgrn", n_ctx_blocks=n_ctx_blocks, layer_slot=layer_slot),
        grid=(bsz, A_HEADS),
        in_specs=[col(0), col(4), col(8), col(12), col(16),
                  pl.BlockSpec((lb_logits.shape[0], 2, LANES), lambda b, h: (0, 0, h)),
                  _const_spec(gn.shape)],
        out_specs=pl.BlockSpec((1, t, LANES), lambda b, h: (b, 0, h)),
        out_shape=jax.ShapeDtypeStruct((bsz, t, A_HEADS * A_VAL), BF16),
        scratch_shapes=[pltpu.VMEM((t, LANES), F32), pltpu.VMEM((t, LANES), F32)],
        compiler_params=pltpu.CompilerParams(vmem_limit_bytes=VMEM_LIMIT),
        name="scan_hgrn",
    )(p, p, p, p, p, lb_logits, gn)


def _scan_gla(p, gn, n_ctx_blocks):
    bsz, t, _ = p.shape
    col = lambda base: pl.BlockSpec((1, t, LANES), lambda b, h, base=base: (b, 0, base + h))
    wide = lambda base: pl.BlockSpec((1, t, 2 * LANES), lambda b, h, base=base: (b, 0, base + h))
    return pl.pallas_call(
        functools.partial(_scan_kernel, mode="gla", n_ctx_blocks=n_ctx_blocks, layer_slot=0),
        grid=(bsz, B_HEADS // 2),
        in_specs=[col(20), col(22), col(32), col(34), wide(12), wide(14),
                  _const_spec(gn.shape)],
        out_specs=pl.BlockSpec((1, t, 2 * LANES), lambda b, h: (b, 0, h)),
        out_shape=jax.ShapeDtypeStruct((bsz, t, B_HEADS * B_VAL), BF16),
        scratch_shapes=[pltpu.VMEM((t, 2 * LANES), F32), pltpu.VMEM((t, 2 * LANES), F32)],
        compiler_params=pltpu.CompilerParams(vmem_limit_bytes=VMEM_LIMIT),
        name="scan_gla",
    )(p, p, p, p, p, p, gn)


def _post_kernel(y0_ref, y1_ref, x_ref, mod_ref, ng_ref, wo_ref, win_ref, wout_ref, o_ref):
    m = mod_ref[0, 0]
    half = y0_ref.shape[2]
    hid = wout_ref.shape[0]
    t = jnp.dot(y0_ref[0], wo_ref[:half, :], preferred_element_type=F32)
    t = t + jnp.dot(y1_ref[0], wo_ref[half:, :], preferred_element_type=F32)
    x1 = x_ref[0] + m[2:3] * _rms(t, ng_ref[1:2])
    u = _modulate(x1, ng_ref[2:3], m[3:4], m[4:5]).astype(BF16)
    h = jnp.dot(u, win_ref[...], preferred_element_type=F32)
    act = (_silu(h[:, :hid]) * h[:, hid:]).astype(BF16)
    h2 = jnp.dot(act, wout_ref[...], preferred_element_type=F32)
    o_ref[0] = x1 + m[5:6] * _rms(h2, ng_ref[3:4])


def _post(y0, y1, y_col0, y_col1, xcat, x_tile0, n_tiles, mod, n_ctx_tiles, ng, wo, win, wout):
    bsz, _, d = xcat.shape
    tm = TOKEN_TILE
    half = d // 2
    return pl.pallas_call(
        _post_kernel,
        grid=(bsz, n_tiles),
        in_specs=[
            pl.BlockSpec((1, tm, half), lambda b, i: (b, i, y_col0)),
            pl.BlockSpec((1, tm, half), lambda b, i: (b, i, y_col1)),
            pl.BlockSpec((1, tm, d), lambda b, i: (b, i + x_tile0, 0)),
            pl.BlockSpec((1, 1, 6, d),
                         lambda b, i: (b, (i + x_tile0 >= n_ctx_tiles).astype(jnp.int32), 0, 0)),
            _const_spec(ng.shape), _const_spec(wo.shape), _const_spec(win.shape), _const_spec(wout.shape),
        ],
        out_specs=pl.BlockSpec((1, tm, d), lambda b, i: (b, i, 0)),
        out_shape=jax.ShapeDtypeStruct((bsz, n_tiles * tm, d), F32),
        compiler_params=pltpu.CompilerParams(vmem_limit_bytes=VMEM_LIMIT),
        name="post",
    )(y0, y1, xcat, mod, ng, wo, win, wout)


def _rope(x, cos, sin_signed, first_half):
    partner = jnp.where(first_half, pltpu.roll(x, LANES - 32, 1), pltpu.roll(x, 32, 1))
    return x * cos + partner * sin_signed


def _qkv_kernel(x_ref, mod_ref, ng_ref, w_ref, cos_ref, sin_ref, q_ref, k_ref, v_ref):
    m = mod_ref[0, 0]
    u = _modulate(x_ref[0], ng_ref[0:1], m[0:1], m[1:2]).astype(BF16)
    p = jnp.dot(u, w_ref[...], preferred_element_type=F32)
    nq, nk = q_ref.shape[2], k_ref.shape[2]
    cos, sin = cos_ref[...], sin_ref[...]
    lane = lax.broadcasted_iota(jnp.int32, cos.shape, 1)
    first_half = (lane % C_HEAD_DIM) < (C_HEAD_DIM // 2)
    for j in range(nq // LANES):
        cols = slice(j * LANES, (j + 1) * LANES)
        q_ref[0, :, cols] = (_rope(p[:, cols], cos, sin, first_half) * (C_HEAD_DIM ** -0.5)).astype(BF16)
    for j in range(nk // LANES):
        cols = slice(nq + j * LANES, nq + (j + 1) * LANES)
        k_ref[0, :, j * LANES:(j + 1) * LANES] = _rope(p[:, cols], cos, sin, first_half).astype(BF16)
    v_ref[0] = p[:, nq + nk:].astype(BF16)


def _qkv(xcat, mod, ng, w, cos, sin, n_ctx_tiles):
    bsz, t, d = xcat.shape
    tm = TOKEN_TILE
    nq = C_HEADS * C_HEAD_DIM
    nk = C_KV_HEADS * LANES
    return pl.pallas_call(
        _qkv_kernel,
        grid=(bsz, t // tm),
        in_specs=[
            pl.BlockSpec((1, tm, d), lambda b, i: (b, i, 0)),
            pl.BlockSpec((1, 1, 6, d), lambda b, i: (b, (i >= n_ctx_tiles).astype(jnp.int32), 0, 0)),
            _const_spec(ng.shape), _const_spec(w.shape),
            pl.BlockSpec((tm, LANES), lambda b, i: (i, 0)),
            pl.BlockSpec((tm, LANES), lambda b, i: (i, 0)),
        ],
        out_specs=[
            pl.BlockSpec((1, tm, nq), lambda b, i: (b, i, 0)),
            pl.BlockSpec((1, tm, nk), lambda b, i: (b, i, 0)),
            pl.BlockSpec((1, tm, nk), lambda b, i: (b, i, 0)),
        ],
        out_shape=[jax.ShapeDtypeStruct((bsz, t, nq), BF16),
                   jax.ShapeDtypeStruct((bsz, t, nk), BF16),
                   jax.ShapeDtypeStruct((bsz, t, nk), BF16)],
        compiler_params=pltpu.CompilerParams(vmem_limit_bytes=VMEM_LIMIT),
        name="qkv",
    )(xcat, mod, ng, w, cos, sin)


def _attn_kernel(sink_ref, q_ref, k_ref, v_ref, o_ref, *, n_ctx, n_blocks):
    n = pl.program_id(1)
    blk = ATT_BLOCK
    t = k_ref.shape[1]
    group = C_HEADS // C_KV_HEADS
    qb = q_ref[0]
    r_own = n_ctx + n * blk
    r_prev = pl.multiple_of(r_own - blk, blk)
    r_next = pl.multiple_of(jnp.minimum(r_own + blk, t - blk), blk)
    r_own = pl.multiple_of(r_own, blk)
    rows = group * blk
    qi = lax.broadcasted_iota(jnp.int32, (rows, blk), 0) % blk
    kj = lax.broadcasted_iota(jnp.int32, (rows, blk), 1)
    mask_prev = (kj >= qi) & (n > 0)
    mask_next = (kj <= qi) & (n < n_blocks - 1)
    lane = lax.broadcasted_iota(jnp.int32, (blk, LANES), 1)
    low = lane < C_HEAD_DIM
    zero = jnp.zeros((blk, LANES), BF16)
    for kg in range(C_KV_HEADS):
        cols = slice(kg * LANES, (kg + 1) * LANES)
        parts, sinks = [], []
        for hh in range(group):
            h = kg * group + hh
            pair = qb[:, (h // 2) * LANES:(h // 2 + 1) * LANES]
            parts.append(jnp.where(low if h % 2 == 0 else ~low, pair, zero))
            sinks.append(jnp.full((blk, 1), sink_ref[h], F32))
        q4 = jnp.concatenate(parts, axis=0)
        sink = jnp.concatenate(sinks, axis=0)
        s_prev = lax.dot_general(q4, k_ref[0, pl.ds(r_prev, blk), cols], NT, preferred_element_type=F32)
        s_own = lax.dot_general(q4, k_ref[0, pl.ds(r_own, blk), cols], NT, preferred_element_type=F32)
        s_next = lax.dot_general(q4, k_ref[0, pl.ds(r_next, blk), cols], NT, preferred_element_type=F32)
        s_ctx = lax.dot_general(q4, k_ref[0, 0:n_ctx, cols], NT, preferred_element_type=F32)
        s = jnp.concatenate([jnp.where(mask_prev, s_prev, -1e30), s_own,
                             jnp.where(mask_next, s_next, -1e30), s_ctx], axis=1)
        mx = jnp.maximum(jnp.max(s, axis=1, keepdims=True), sink)
        p = jnp.exp(s - mx)
        denom = jnp.sum(p, axis=1, keepdims=True) + jnp.exp(sink - mx)
        vv = jnp.concatenate([v_ref[0, pl.ds(r_prev, blk), cols], v_ref[0, pl.ds(r_own, blk), cols],
                              v_ref[0, pl.ds(r_next, blk), cols], v_ref[0, 0:n_ctx, cols]], axis=0)
        o = jnp.dot(p.astype(BF16), vv, preferred_element_type=F32) / denom
        for pp in range(group // 2):
            even = o[(2 * pp) * blk:(2 * pp + 1) * blk]
            odd = o[(2 * pp + 1) * blk:(2 * pp + 2) * blk]
            c0 = (kg * (group // 2) + pp) * LANES
            o_ref[0, :, c0:c0 + LANES] = jnp.where(low, even, odd).astype(o_ref.dtype)


def _attention(q, k2, v2, sink, n_ctx):
    bsz, t, nq = q.shape
    nk = k2.shape[2]
    blk = ATT_BLOCK
    n_blocks = (t - n_ctx) // blk
    ctx_blocks = n_ctx // blk
    return pl.pallas_call(
        functools.partial(_attn_kernel, n_ctx=n_ctx, n_blocks=n_blocks),
        grid=(bsz, n_blocks),
        in_specs=[
            pl.BlockSpec(memory_space=pltpu.SMEM),
            pl.BlockSpec((1, blk, nq), lambda b, n: (b, n + ctx_blocks, 0)),
            pl.BlockSpec((1, t, nk), lambda b, n: (b, 0, 0)),
            pl.BlockSpec((1, t, nk), lambda b, n: (b, 0, 0)),
        ],
        out_specs=pl.BlockSpec((1, blk, nq), lambda b, n: (b, n, 0)),
        out_shape=jax.ShapeDtypeStruct((bsz, t - n_ctx, nq), BF16),
        compiler_params=pltpu.CompilerParams(vmem_limit_bytes=VMEM_LIMIT),
        name="attention",
    )(sink, q, k2, v2)


def _rec_weights(w_in, w_g2, b_g2):
    widths = (A_HEADS * A_KEY,) * 3 + (A_HEADS * A_VAL,) * 2 + (B_HEADS * B_KEY,) * 2 + (
        B_HEADS * B_VAL, B_GATE_RANK, B_GATE_RANK, B_HEADS * B_VAL)
    off = np.concatenate([[0], np.cumsum(widths)])
    seg = lambda i: w_in[:, off[i]:off[i + 1]]
    d = w_in.shape[0]
    pad = jnp.zeros((d, LANES - 2 * B_GATE_RANK), w_in.dtype)
    w = jnp.concatenate([seg(i) for i in (0, 1, 2, 3, 4, 5, 6, 7, 10, 8, 9)] + [pad], axis=1).astype(BF16)
    nk = B_HEADS * B_KEY
    w2 = jnp.zeros((LANES, 2 * nk), F32)
    w2 = w2.at[:B_GATE_RANK, :nk].set(w_g2[0]).at[B_GATE_RANK:2 * B_GATE_RANK, nk:].set(w_g2[1])
    return w, w2.astype(BF16), b_g2.reshape(1, 2 * nk)


def _att_weights(w_qkv):
    d = w_qkv.shape[0]
    nq, nkv = C_HEADS * C_HEAD_DIM, C_KV_HEADS * C_HEAD_DIM
    perm = np.concatenate([np.arange(0, C_HEAD_DIM, 2), np.arange(1, C_HEAD_DIM, 2)])
    wq = w_qkv[:, :nq].reshape(d, C_HEADS, C_HEAD_DIM)[:, :, perm].reshape(d, nq)
    wk = w_qkv[:, nq:nq + nkv].reshape(d, C_KV_HEADS, C_HEAD_DIM)[:, :, perm]
    wv = w_qkv[:, nq + nkv:].reshape(d, C_KV_HEADS, C_HEAD_DIM)
    wk = jnp.concatenate([wk, wk], axis=2).reshape(d, 2 * nkv)
    wv = jnp.concatenate([wv, wv], axis=2).reshape(d, 2 * nkv)
    return jnp.concatenate([wq, wk, wv], axis=1).astype(BF16)


def _rope_tables(seq, n_ctx):
    n_rows = seq // GRID_W
    row = jnp.repeat(jnp.arange(n_rows), GRID_W).astype(F32)
    col = jnp.tile(jnp.arange(GRID_W), n_rows).astype(F32)
    half = C_HEAD_DIM // 2
    inv = ROPE_BASE ** (-jnp.arange(0, half, 2, dtype=F32) / half)
    ang = jnp.concatenate([row[:, None] * inv, col[:, None] * inv], axis=-1)
    cos, sin = jnp.cos(ang), jnp.sin(ang)
    cos = jnp.concatenate([jnp.ones((n_ctx, half), F32), cos], axis=0)
    sin = jnp.concatenate([jnp.zeros((n_ctx, half), F32), sin], axis=0)
    reps = LANES // C_HEAD_DIM
    return (jnp.tile(jnp.concatenate([cos, cos], axis=1), (1, reps)),
            jnp.tile(jnp.concatenate([-sin, sin], axis=1), (1, reps)))


def kernel(x, c, ctx, c_ctx, ada_w, ada_b, norm_g, rec_w_in, rec_w_out, rec_lb_logits, rec_w_g2, rec_b_g2,
           rec_gn_a, rec_gn_b, att_w_qkv, att_w_o, att_sink, ffn_w_in, ffn_w_out):
    bsz, seq, d = x.shape
    n_ctx = ctx.shape[1]
    depth = ada_w.shape[0]
    tm = TOKEN_TILE
    assert n_ctx % tm == 0 and seq % tm == 0 and n_ctx % SCAN_BLOCK == 0 and seq % SCAN_BLOCK == 0
    n_ctx_tiles = n_ctx // tm
    n_lat_tiles = seq // tm

    rows = -(-(bsz + 1) // 8) * 8
    cond = jnp.concatenate([c, c_ctx[None], jnp.zeros((rows - bsz - 1, d), F32)], axis=0)
    mod_all = _ada(cond, ada_w, ada_b)
    mods = []
    for l in range(depth):
        lat = mod_all[l, :bsz].reshape(bsz, 1, 6, d)
        cx = jnp.broadcast_to(mod_all[l, bsz].reshape(1, 1, 6, d), (bsz, 1, 6, d))
        mods.append(jnp.concatenate([cx, lat], axis=1))

    xcat = jnp.concatenate([ctx, x], axis=1)
    cos, sin = _rope_tables(seq, n_ctx)

    for l in range(depth):
        need_ctx = l < depth - 1
        j = l // 2
        ng = norm_g[l]
        if l % 2 == 0:
            w, w2, b2 = _rec_weights(rec_w_in[j], rec_w_g2[j], rec_b_g2[j])
            p = _rec_in(xcat, mods[l], ng, w, w2, b2, n_ctx_tiles)
            ya = _scan_hgrn(p, rec_lb_logits, rec_gn_a[j].reshape(1, -1), n_ctx // SCAN_BLOCK, j)
            yb = _scan_gla(p, rec_gn_b[j].reshape(1, -1), n_ctx // SCAN_BLOCK)
            y0, y1, c0, c1, wo = ya, yb, 0, 0, rec_w_out[j]
            y_tile0 = 0
        else:
            q, k2, v2 = _qkv(xcat, mods[l], ng, _att_weights(att_w_qkv[j]), cos, sin, n_ctx_tiles)
            y = _attention(q, k2, v2, att_sink[j], n_ctx)
            y0, y1, c0, c1, wo = y, y, 0, 1, att_w_o[j]
            y_tile0 = n_ctx_tiles
        wo, win, wout = wo.astype(BF16), ffn_w_in[l].astype(BF16), ffn_w_out[l].astype(BF16)
        if need_ctx:
            assert y_tile0 == 0
            xcat = _post(y0, y1, c0, c1, xcat, 0, n_ctx_tiles + n_lat_tiles, mods[l], n_ctx_tiles,
                         ng, wo, win, wout)
        else:
            if y_tile0 == 0:
                y0 = y0[:, n_ctx:]
                y1 = y1[:, n_ctx:]
            x_lat = _post(y0, y1, c0, c1, xcat, n_ctx_tiles, n_lat_tiles, mods[l], n_ctx_tiles,
                          ng, wo, win, wout)
    return x_lat
```

```python
import functools

import numpy as np
import jax
import jax.numpy as jnp
from jax import lax
from jax.experimental import pallas as pl
from jax.experimental.pallas import tpu as pltpu

F32 = jnp.float32
BF16 = jnp.bfloat16
HIGHEST = lax.Precision.HIGHEST

EPS = 1e-6
LANES = 128
TOKEN_TILE = 256
SCAN_BLOCK = 256
LA_CHUNK = 32
SCAN_GROUP = 4 * LA_CHUNK
ATT_BLOCK = 128
ATT_WINDOW = 128
ATT_QUERY_BLOCKS = 2
V_ROWS = 80
GRID_W = 64
ROPE_BASE = 10000.0
GLA_GATE_NORM = 16.0
A_HEADS, A_KEY, A_VAL = 4, 128, 128
B_HEADS, B_KEY, B_VAL = 4, 64, 128
B_GATE_RANK = 16
C_HEADS, C_KV_HEADS, C_HEAD_DIM = 16, 4, 64
VMEM_LIMIT = 56 * 1024 * 1024

NT = (((1,), (1,)), ((), ()))
TN = (((0,), (0,)), ((), ()))


def _sigmoid(x):
    return 1.0 / (1.0 + jnp.exp(-x))


def _silu(x):
    return x * _sigmoid(x)


def _log_sigmoid(x):
    return -(jnp.maximum(-x, 0.0) + jnp.log1p(jnp.exp(-jnp.abs(x))))


def _rms(x, g):
    ms = jnp.mean(x * x, axis=-1, keepdims=True)
    return (x * lax.rsqrt(ms + EPS)) * g


def _modulate(x, g, shift, scale):
    return _rms(x, g) * (1.0 + scale) + shift


def _const_spec(shape):
    nd = len(shape)
    return pl.BlockSpec(shape, lambda *_: (0,) * nd, pipeline_mode=pl.Buffered(1))


class _Stream:
    def __init__(self, ctx_arr, lat_arr, n_ctx_tiles, tile0=0, lat_off=0):
        self.arrays = (ctx_arr, lat_arr)
        self.n_ctx_tiles, self.tile0 = n_ctx_tiles, tile0
        d = ctx_arr.shape[-1]
        nct = n_ctx_tiles
        self.specs = [
            pl.BlockSpec((1, TOKEN_TILE, d), lambda b, i: (b, jnp.minimum(i + tile0, nct - 1), 0)),
            pl.BlockSpec((1, TOKEN_TILE, d), lambda b, i: (b, jnp.maximum(i + tile0 - nct, 0) + lat_off, 0)),
        ]
        self.mod_spec = pl.BlockSpec((1, 1, 6, d), lambda b, i: (b, (i + tile0 >= nct).astype(jnp.int32), 0, 0))

    def tile(self, ctx_ref, lat_ref):
        if self.tile0 >= self.n_ctx_tiles:
            return lat_ref[0]
        return jnp.where(pl.program_id(1) + self.tile0 < self.n_ctx_tiles, ctx_ref[0], lat_ref[0])


def _ada_kernel(c_ref, w_ref, b_ref, o_ref):
    s = _silu(c_ref[...])
    o_ref[0] = jnp.dot(s, w_ref[0], preferred_element_type=F32, precision=HIGHEST) + b_ref[0]


def _ada(cond, ada_w, ada_b):
    depth, d, n = ada_w.shape
    rows = cond.shape[0]
    tn = n // 4
    return pl.pallas_call(
        _ada_kernel,
        grid=(depth, n // tn),
        in_specs=[
            pl.BlockSpec((rows, d), lambda l, j: (0, 0)),
            pl.BlockSpec((1, d, tn), lambda l, j: (l, 0, j)),
            pl.BlockSpec((1, 1, tn), lambda l, j: (l, 0, j)),
        ],
        out_specs=pl.BlockSpec((1, rows, tn), lambda l, j: (l, 0, j)),
        out_shape=jax.ShapeDtypeStruct((depth, rows, n), F32),
        compiler_params=pltpu.CompilerParams(vmem_limit_bytes=VMEM_LIMIT),
        name="ada",
    )(cond, ada_w, ada_b.reshape(depth, 1, n))


def _recin_kernel(xc_ref, xl_ref, mod_ref, ng_ref, w_ref, w2_ref, b2_ref, o_ref, *, n_main, stream):
    m = mod_ref[0, 0]
    u = _modulate(stream.tile(xc_ref, xl_ref), ng_ref[0:1], m[0:1], m[1:2]).astype(BF16)
    p = jnp.dot(u, w_ref[...], preferred_element_type=F32)
    lr = p[:, n_main:].astype(BF16)
    pre = jnp.dot(lr, w2_ref[...], preferred_element_type=F32) + b2_ref[...]
    o_ref[0, :, :n_main] = p[:, :n_main]
    o_ref[0, :, n_main:] = _log_sigmoid(pre) * (1.0 / GLA_GATE_NORM)


def _rec_in(stream, n_tiles, mod, ng, w, w2, b2):
    bsz = mod.shape[0]
    n_in = w.shape[1]
    n_main = n_in - LANES
    n_out = n_main + w2.shape[1]
    tm = TOKEN_TILE
    return pl.pallas_call(
        functools.partial(_recin_kernel, n_main=n_main, stream=stream),
        grid=(bsz, n_tiles),
        in_specs=stream.specs + [
            stream.mod_spec,
            _const_spec(ng.shape),
            _const_spec(w.shape),
            _const_spec(w2.shape),
            _const_spec(b2.shape),
        ],
        out_specs=pl.BlockSpec((1, tm, n_out), lambda b, i: (b, i, 0)),
        out_shape=jax.ShapeDtypeStruct((bsz, n_tiles * tm, n_out), F32),
        compiler_params=pltpu.CompilerParams(vmem_limit_bytes=VMEM_LIMIT),
        name="rec_in",
    )(*stream.arrays, mod, ng, w, w2, b2)


def _chunk_cumsum(g, reverse):
    r = g.shape[0]
    sl = 8
    x = g.reshape(r // sl, sl, LANES)
    sub = lax.broadcasted_iota(jnp.int32, x.shape, 1)
    for s in (1, 2, 4):
        if reverse:
            x = x + jnp.where(sub < sl - s, pltpu.roll(x, sl - s, 1), 0.0)
        else:
            x = x + jnp.where(sub >= s, pltpu.roll(x, s, 1), 0.0)
    per = LA_CHUNK // sl
    x = x.reshape(r // LA_CHUNK, per, sl, LANES)
    parts = [None] * per
    carry = None
    for v in (range(per - 1, -1, -1) if reverse else range(per)):
        xv = x[:, v:v + 1]
        parts[v] = xv if carry is None else xv + carry
        edge = xv[:, :, 0:1] if reverse else xv[:, :, sl - 1:sl]
        carry = edge if carry is None else carry + edge
    return jnp.concatenate(parts, axis=1).reshape(r, LANES)


def _group_masks(reverse):
    n = SCAN_GROUP
    ri = lax.broadcasted_iota(jnp.int32, (n, n), 0)
    ci = lax.broadcasted_iota(jnp.int32, (n, n), 1)
    rc, cc = ri // LA_CHUNK, ci // LA_CHUNK
    if reverse:
        return (rc == cc) & (ci >= ri), (rc + 1 == cc) & (rc % 2 == 0), (rc < 2) & (cc >= 2)
    return (rc == cc) & (ci <= ri), (rc == cc + 1) & (rc % 2 == 1), (rc >= 2) & (cc < 2)


def _scale_chunks(x, factors):
    parts = []
    for c, f in enumerate(factors):
        xc = x[c * LA_CHUNK:(c + 1) * LA_CHUNK]
        parts.append(xc if f is None else xc * f)
    return jnp.concatenate(parts, axis=0)


def _scan_direction(q, k, g, v_bf, states, masks, lane_masks, reverse):
    rows = q.shape[0]
    nsub = len(states)
    nch = SCAN_GROUP // LA_CHUNK
    assert nch == 4
    ngroup = rows // SCAN_GROUP
    m_diag, m_adj, m_far = masks
    bc = _chunk_cumsum(g, reverse)
    q_dec = q * jnp.exp(bc)
    k_inv = k * jnp.exp(-bc)
    states = list(states)
    outs = [[None] * ngroup for _ in range(nsub)]
    pos = (lambda c: nch - 1 - c) if reverse else (lambda c: c)
    edge = 0 if reverse else LA_CHUNK - 1
    for gi in (range(ngroup - 1, -1, -1) if reverse else range(ngroup)):
        r0 = gi * SCAN_GROUP
        rs = slice(r0, r0 + SCAN_GROUP)
        tot = [bc[r0 + c * LA_CHUNK + edge:r0 + c * LA_CHUNK + edge + 1] for c in range(nch)]
        ts = [tot[pos(s)] for s in range(nch)]
        e_ts = [jnp.exp(t) for t in ts]
        e_pre = [None, e_ts[0], jnp.exp(ts[0] + ts[1]), jnp.exp(ts[0] + ts[1] + ts[2])]
        e_post = [jnp.exp(ts[1] + ts[2] + ts[3]), jnp.exp(ts[2] + ts[3]), e_ts[3], None]
        e_total = jnp.exp(ts[0] + ts[1] + ts[2] + ts[3])
        ki = k_inv[rs]
        ke = _scale_chunks(ki, [e_ts[pos(c)] for c in range(nch)])
        kh = _scale_chunks(ke, [e_post[pos(c)] for c in range(nch)])
        kb = _scale_chunks(ke, [e_ts[1] if pos(c) == 0 else None for c in range(nch)])
        kcat = jnp.concatenate([ki.astype(BF16), ke.astype(BF16)], axis=0)
        kb, kh = kb.astype(BF16), kh.astype(BF16)
        for j in range(nsub):
            qd = q_dec[rs]
            if lane_masks is not None:
                qd = jnp.where(lane_masks[j], qd, 0.0)
            qt = _scale_chunks(qd, [e_pre[pos(c)] for c in range(nch)]).astype(BF16)
            qb = _scale_chunks(qd, [e_ts[2] if pos(c) == 3 else None for c in range(nch)]).astype(BF16)
            qd = qd.astype(BF16)
            v_g = v_bf[rs, j * LANES:(j + 1) * LANES]
            s12 = lax.dot_general(qd, kcat, NT, preferred_element_type=F32)
            s3 = lax.dot_general(qb, kb, NT, preferred_element_type=F32)
            sc = jnp.where(m_diag, s12[:, :SCAN_GROUP],
                           jnp.where(m_adj, s12[:, SCAN_GROUP:], jnp.where(m_far, s3, 0.0)))
            o = jnp.dot(sc.astype(BF16), v_g, preferred_element_type=F32)
            o = o + lax.dot_general(qt, states[j].astype(BF16), NT, preferred_element_type=F32)
            states[j] = states[j] * e_total + lax.dot_general(v_g, kh, TN, preferred_element_type=F32)
            outs[j][gi] = o
    return [jnp.concatenate(o, axis=0) for o in outs], tuple(states)


def _scan_kernel(*refs, mode, n_ctx_blocks, layer_slot):
    if mode == "hgrn":
        (q_ref, zf_ref, zb_ref, v_ref, og_ref, lb_ref, gn_ref, o_ref, of_ref, ob_ref) = refs
        nsub = 1
    else:
        (q_ref, k_ref, gf_ref, gb_ref, v_ref, og_ref, gn_ref, o_ref, of_ref, ob_ref) = refs
        nsub = 2
    t = q_ref.shape[1]
    blk = SCAN_BLOCK
    nblk = t // blk
    masks_f, masks_b = _group_masks(False), _group_masks(True)
    if nsub == 2:
        lane = lax.broadcasted_iota(jnp.int32, (SCAN_GROUP, LANES), 1)
        lane_masks = [lane < B_KEY, lane >= B_KEY]
    else:
        lane_masks = None

    if mode == "hgrn":
        lg = lb_ref[...]
        e = jnp.exp(lg - jnp.max(lg, axis=0, keepdims=True))
        p = e / jnp.sum(e, axis=0, keepdims=True)
        lb = jnp.sum(p[:layer_slot + 1], axis=0)

    def features(rows, d):
        q_raw = q_ref[0, rows, :]
        if mode == "hgrn":
            z = (zf_ref, zb_ref)[d][0, rows, :]
            lbd = lb[d:d + 1]
            f = lbd + (1.0 - lbd) * _sigmoid(z)
            return _silu(q_raw) * (A_KEY ** -0.5), 1.0 - f, jnp.log(f)
        return q_raw * (B_KEY ** -0.5), k_ref[0, rows, :], (gf_ref, gb_ref)[d][0, rows, :]

    def body(i, carry):
        st_f, st_b = carry
        jb = jnp.where(i < n_ctx_blocks, n_ctx_blocks - 1 - i, nblk - 1 - (i - n_ctx_blocks))
        rows_f = pl.ds(pl.multiple_of(i * blk, blk), blk)
        rows_b = pl.ds(pl.multiple_of(jb * blk, blk), blk)
        q, k, g = features(rows_f, 0)
        o, st_f = _scan_direction(q, k, g, v_ref[0, rows_f, :].astype(BF16), st_f,
                                  masks_f, lane_masks, False)
        for j in range(nsub):
            of_ref[rows_f, j * LANES:(j + 1) * LANES] = o[j]
        q, k, g = features(rows_b, 1)
        o, st_b = _scan_direction(q, k, g, v_ref[0, rows_b, :].astype(BF16), st_b,
                                  masks_b, lane_masks, True)
        for j in range(nsub):
            ob_ref[rows_b, j * LANES:(j + 1) * LANES] = o[j]
        return st_f, st_b

    zero = tuple(jnp.zeros((LANES, LANES), F32) for _ in range(nsub))
    lax.fori_loop(0, nblk, body, (zero, zero))

    def finish(i, _):
        rows = pl.ds(pl.multiple_of(i * blk, blk), blk)
        for j in range(nsub):
            cols = slice(j * LANES, (j + 1) * LANES)
            o = of_ref[rows, cols] + ob_ref[rows, cols]
            y = _rms(o, gn_ref[...]) * _silu(og_ref[0, rows, cols])
            o_ref[0, rows, cols] = y.astype(o_ref.dtype)
        return 0

    lax.fori_loop(0, nblk, finish, 0)


def _scan_hgrn(p, lb_logits, gn, n_ctx_blocks, layer_slot):
    bsz, t, _ = p.shape
    col = lambda base: pl.BlockSpec((1, t, LANES), lambda b, h, base=base: (b, 0, base + h))
    return pl.pallas_call(
        functools.partial(_scan_kernel, mode="hgrn", n_ctx_blocks=n_ctx_blocks, layer_slot=layer_slot),
        grid=(bsz, A_HEADS),
        in_specs=[col(0), col(4), col(8), col(12), col(16),
                  pl.BlockSpec((lb_logits.shape[0], 2, LANES), lambda b, h: (0, 0, h)),
                  _const_spec(gn.shape)],
        out_specs=pl.BlockSpec((1, t, LANES), lambda b, h: (b, 0, h)),
        out_shape=jax.ShapeDtypeStruct((bsz, t, A_HEADS * A_VAL), BF16),
        scratch_shapes=[pltpu.VMEM((t, LANES), F32), pltpu.VMEM((t, LANES), F32)],
        compiler_params=pltpu.CompilerParams(vmem_limit_bytes=VMEM_LIMIT),
        name="scan_hgrn",
    )(p, p, p, p, p, lb_logits, gn)


def _scan_gla(p, gn, n_ctx_blocks):
    bsz, t, _ = p.shape
    col = lambda base: pl.BlockSpec((1, t, LANES), lambda b, h, base=base: (b, 0, base + h))
    wide = lambda base: pl.BlockSpec((1, t, 2 * LANES), lambda b, h, base=base: (b, 0, base + h))
    return pl.pallas_call(
        functools.partial(_scan_kernel, mode="gla", n_ctx_blocks=n_ctx_blocks, layer_slot=0),
        grid=(bsz, B_HEADS // 2),
        in_specs=[col(20), col(22), col(32), col(34), wide(12), wide(14),
                  _const_spec(gn.shape)],
        out_specs=pl.BlockSpec((1, t, 2 * LANES), lambda b, h: (b, 0, h)),
        out_shape=jax.ShapeDtypeStruct((bsz, t, B_HEADS * B_VAL), BF16),
        scratch_shapes=[pltpu.VMEM((t, 2 * LANES), F32), pltpu.VMEM((t, 2 * LANES), F32)],
        compiler_params=pltpu.CompilerParams(vmem_limit_bytes=VMEM_LIMIT),
        name="scan_gla",
    )(p, p, p, p, p, p, gn)


def _post_kernel(y0_ref, y1_ref, xc_ref, xl_ref, mod_ref, ng_ref, wo_ref, win_ref, wout_ref, o_ref, *, stream):
    m = mod_ref[0, 0]
    half = y0_ref.shape[2]
    hid = wout_ref.shape[0]
    t = jnp.dot(y0_ref[0], wo_ref[:half, :], preferred_element_type=F32)
    t = t + jnp.dot(y1_ref[0], wo_ref[half:, :], preferred_element_type=F32)
    x1 = stream.tile(xc_ref, xl_ref) + m[2:3] * _rms(t, ng_ref[1:2])
    u = _modulate(x1, ng_ref[2:3], m[3:4], m[4:5]).astype(BF16)
    h = jnp.dot(u, win_ref[...], preferred_element_type=F32)
    act = (_silu(h[:, :hid]) * h[:, hid:]).astype(BF16)
    h2 = jnp.dot(act, wout_ref[...], preferred_element_type=F32)
    o_ref[0] = x1 + m[5:6] * _rms(h2, ng_ref[3:4])


def _post(y0, y1, y_col0, y_col1, stream, n_tiles, mod, ng, wo, win, wout):
    bsz, d = mod.shape[0], mod.shape[-1]
    tm = TOKEN_TILE
    half = d // 2
    return pl.pallas_call(
        functools.partial(_post_kernel, stream=stream),
        grid=(bsz, n_tiles),
        in_specs=[
            pl.BlockSpec((1, tm, half), lambda b, i: (b, i, y_col0)),
            pl.BlockSpec((1, tm, half), lambda b, i: (b, i, y_col1)),
        ] + stream.specs + [
            stream.mod_spec,
            _const_spec(ng.shape), _const_spec(wo.shape), _const_spec(win.shape), _const_spec(wout.shape),
        ],
        out_specs=pl.BlockSpec((1, tm, d), lambda b, i: (b, i, 0)),
        out_shape=jax.ShapeDtypeStruct((bsz, n_tiles * tm, d), F32),
        compiler_params=pltpu.CompilerParams(vmem_limit_bytes=VMEM_LIMIT),
        name="post",
    )(y0, y1, *stream.arrays, mod, ng, wo, win, wout)


def _rope(x, cos, sin_signed, first_half):
    partner = jnp.where(first_half, pltpu.roll(x, LANES - 32, 1), pltpu.roll(x, 32, 1))
    return x * cos + partner * sin_signed


def _qkv_kernel(xc_ref, xl_ref, mod_ref, ng_ref, w_ref, cos_ref, sin_ref, q_ref, k_ref, vt_ref, *, stream):
    m = mod_ref[0, 0]
    u = _modulate(stream.tile(xc_ref, xl_ref), ng_ref[0:1], m[0:1], m[1:2]).astype(BF16)
    p = jnp.dot(u, w_ref[...], preferred_element_type=F32)
    nq, nk = q_ref.shape[2], k_ref.shape[2]
    cos, sin = cos_ref[...], sin_ref[...]
    lane = lax.broadcasted_iota(jnp.int32, cos.shape, 1)
    first_half = (lane % C_HEAD_DIM) < (C_HEAD_DIM // 2)
    for j in range(nq // LANES):
        cols = slice(j * LANES, (j + 1) * LANES)
        q_ref[0, :, cols] = (_rope(p[:, cols], cos, sin, first_half) * (C_HEAD_DIM ** -0.5)).astype(BF16)
    for j in range(nk // LANES):
        cols = slice(nq + j * LANES, nq + (j + 1) * LANES)
        k_ref[0, :, j * LANES:(j + 1) * LANES] = _rope(p[:, cols], cos, sin, first_half).astype(BF16)
    ones_rows = (lane >= C_HEAD_DIM) & (lane < V_ROWS)
    for j in range(C_KV_HEADS):
        cols = slice(nq + nk + j * LANES, nq + nk + (j + 1) * LANES)
        vt_ref[0, j * LANES:(j + 1) * LANES, :] = jnp.where(ones_rows, 1.0, p[:, cols]).T.astype(BF16)


def _qkv(stream, n_tiles, mod, ng, w, cos, sin):
    bsz = mod.shape[0]
    tm = TOKEN_TILE
    t = n_tiles * tm
    nq = C_HEADS * C_HEAD_DIM
    nk = C_KV_HEADS * LANES
    return pl.pallas_call(
        functools.partial(_qkv_kernel, stream=stream),
        grid=(bsz, n_tiles),
        in_specs=stream.specs + [
            stream.mod_spec,
            _const_spec(ng.shape), _const_spec(w.shape),
            pl.BlockSpec((tm, LANES), lambda b, i: (i, 0)),
            pl.BlockSpec((tm, LANES), lambda b, i: (i, 0)),
        ],
        out_specs=[
            pl.BlockSpec((1, tm, nq), lambda b, i: (b, i, 0)),
            pl.BlockSpec((1, tm, nk), lambda b, i: (b, i, 0)),
            pl.BlockSpec((1, nk, tm), lambda b, i: (b, 0, i)),
        ],
        out_shape=[jax.ShapeDtypeStruct((bsz, t, nq), BF16),
                   jax.ShapeDtypeStruct((bsz, t, nk), BF16),
                   jax.ShapeDtypeStruct((bsz, nk, t), BF16)],
        compiler_params=pltpu.CompilerParams(vmem_limit_bytes=VMEM_LIMIT),
        name="qkv",
    )(*stream.arrays, mod, ng, w, cos, sin)


def _attn_kernel(sink_ref, bias_ref, q_ref, k_ref, vt_ref, o_ref, *, n_ctx):
    blk = ATT_BLOCK
    band = 3 * blk
    t = k_ref.shape[1]
    group = C_HEADS // C_KV_HEADS
    nsub = q_ref.shape[1] // blk
    lane = lax.broadcasted_iota(jnp.int32, (blk, LANES), 1)
    low = lane < C_HEAD_DIM
    zero = jnp.zeros((blk, LANES), BF16)

    def band_start(sb):
        n = pl.program_id(1) * nsub + sb
        return pl.multiple_of(jnp.minimum(n_ctx + (n - 1) * blk, t - band), blk)

    def scores(sb, kg):
        cols = slice(kg * LANES, (kg + 1) * LANES)
        r_band = band_start(sb)
        parts, sinks = [], []
        for hh in range(group):
            h = kg * group + hh
            pair = q_ref[0, sb * blk:(sb + 1) * blk, (h // 2) * LANES:(h // 2 + 1) * LANES]
            parts.append(jnp.where(low if h % 2 == 0 else ~low, pair, zero))
            sinks.append(jnp.full((1, blk), sink_ref[h], F32))
        q4 = jnp.concatenate(parts, axis=0)
        sink = jnp.concatenate(sinks, axis=1)
        kk = jnp.concatenate([k_ref[0, pl.ds(r_band, band), cols], k_ref[0, 0:n_ctx, cols]], axis=0)
        s = lax.dot_general(kk, q4, NT, preferred_element_type=F32)
        bias = jnp.concatenate([bias_ref[sb]] * group, axis=1)
        s = jnp.concatenate([s[:band] + bias, s[band:]], axis=0)
        mx = jnp.maximum(jnp.max(s, axis=0, keepdims=True), sink)
        return s, mx, sink

    def probs(s, mx, sink):
        return jnp.exp((s - mx).astype(BF16)), jnp.exp(sink - mx)

    def finish(sb, kg, p, p_sink):
        r_band = band_start(sb)
        rows = slice(kg * LANES, kg * LANES + V_ROWS)
        vt = jnp.concatenate([vt_ref[0, rows, pl.ds(r_band, band)], vt_ref[0, rows, 0:n_ctx]], axis=1)
        o3 = jnp.dot(vt, p, preferred_element_type=F32)
        o = o3[:C_HEAD_DIM] * (1.0 / (o3[C_HEAD_DIM:C_HEAD_DIM + 1] + p_sink))
        for pp in range(group // 2):
            y = jnp.concatenate([o[:, (2 * pp) * blk:(2 * pp + 1) * blk],
                                 o[:, (2 * pp + 1) * blk:(2 * pp + 2) * blk]], axis=0)
            c0 = (kg * (group // 2) + pp) * LANES
            o_ref[0, sb * blk:(sb + 1) * blk, c0:c0 + LANES] = y.T.astype(o_ref.dtype)

    units = [(sb, kg) for sb in range(nsub) for kg in range(C_KV_HEADS)]
    nu = len(units)
    s_q = {0: scores(*units[0]), 1: scores(*units[1])}
    p_q = {0: probs(*s_q.pop(0))}
    for i in range(nu):
        if i + 2 < nu:
            s_q[i + 2] = scores(*units[i + 2])
        if i + 1 < nu:
            p_q[i + 1] = probs(*s_q.pop(i + 1))
        finish(*units[i], *p_q.pop(i))


def _attn_bias(n_blocks, n_ctx, t):
    blk, band = ATT_BLOCK, 3 * ATT_BLOCK
    n = np.arange(n_blocks)[:, None, None]
    r_band = np.minimum(n_ctx + (n - 1) * blk, t - band)
    kp = np.arange(band)[None, :, None] + (r_band - n_ctx)
    qi = np.arange(blk)[None, None, :] + n * blk
    ok = (np.abs(kp - qi) <= ATT_WINDOW) & (kp >= 0)
    return jnp.asarray(np.where(ok, 0.0, -1e30).astype(np.float32))


def _attention(q, k2, vt, sink, n_ctx):
    bsz, t, nq = q.shape
    nk = k2.shape[2]
    rows = ATT_QUERY_BLOCKS * ATT_BLOCK
    n_blocks = (t - n_ctx) // ATT_BLOCK
    assert n_ctx % rows == 0 and (t - n_ctx) % rows == 0 and n_ctx >= ATT_BLOCK and n_blocks >= 2
    bias = _attn_bias(n_blocks, n_ctx, t)
    return pl.pallas_call(
        functools.partial(_attn_kernel, n_ctx=n_ctx),
        grid=(bsz, n_blocks // ATT_QUERY_BLOCKS),
        in_specs=[
            pl.BlockSpec(memory_space=pltpu.SMEM),
            pl.BlockSpec((ATT_QUERY_BLOCKS,) + bias.shape[1:], lambda b, n: (n, 0, 0)),
            pl.BlockSpec((1, rows, nq), lambda b, n: (b, n + n_ctx // rows, 0)),
            pl.BlockSpec((1, t, nk), lambda b, n: (b, 0, 0)),
            pl.BlockSpec((1, nk, t), lambda b, n: (b, 0, 0)),
        ],
        out_specs=pl.BlockSpec((1, rows, nq), lambda b, n: (b, n, 0)),
        out_shape=jax.ShapeDtypeStruct((bsz, t - n_ctx, nq), BF16),
        compiler_params=pltpu.CompilerParams(vmem_limit_bytes=VMEM_LIMIT),
        name="attention",
    )(sink, bias, q, k2, vt)


def _rec_weights(w_in, w_g2, b_g2):
    widths = (A_HEADS * A_KEY,) * 3 + (A_HEADS * A_VAL,) * 2 + (B_HEADS * B_KEY,) * 2 + (
        B_HEADS * B_VAL, B_GATE_RANK, B_GATE_RANK, B_HEADS * B_VAL)
    off = np.concatenate([[0], np.cumsum(widths)])
    seg = lambda i: w_in[:, off[i]:off[i + 1]]
    d = w_in.shape[0]
    pad = jnp.zeros((d, LANES - 2 * B_GATE_RANK), w_in.dtype)
    w = jnp.concatenate([seg(i) for i in (0, 1, 2, 3, 4, 5, 6, 7, 10, 8, 9)] + [pad], axis=1).astype(BF16)
    nk = B_HEADS * B_KEY
    w2 = jnp.zeros((LANES, 2 * nk), F32)
    w2 = w2.at[:B_GATE_RANK, :nk].set(w_g2[0]).at[B_GATE_RANK:2 * B_GATE_RANK, nk:].set(w_g2[1])
    return w, w2.astype(BF16), b_g2.reshape(1, 2 * nk)


def _att_weights(w_qkv):
    d = w_qkv.shape[0]
    nq, nkv = C_HEADS * C_HEAD_DIM, C_KV_HEADS * C_HEAD_DIM
    perm = np.concatenate([np.arange(0, C_HEAD_DIM, 2), np.arange(1, C_HEAD_DIM, 2)])
    wq = w_qkv[:, :nq].reshape(d, C_HEADS, C_HEAD_DIM)[:, :, perm].reshape(d, nq)
    wk = w_qkv[:, nq:nq + nkv].reshape(d, C_KV_HEADS, C_HEAD_DIM)[:, :, perm]
    wv = w_qkv[:, nq + nkv:].reshape(d, C_KV_HEADS, C_HEAD_DIM)
    wk = jnp.concatenate([wk, wk], axis=2).reshape(d, 2 * nkv)
    wv = jnp.concatenate([wv, jnp.zeros_like(wv)], axis=2).reshape(d, 2 * nkv)
    return jnp.concatenate([wq, wk, wv], axis=1).astype(BF16)


def _rope_tables(seq, n_ctx):
    n_rows = seq // GRID_W
    row = jnp.repeat(jnp.arange(n_rows), GRID_W).astype(F32)
    col = jnp.tile(jnp.arange(GRID_W), n_rows).astype(F32)
    half = C_HEAD_DIM // 2
    inv = ROPE_BASE ** (-jnp.arange(0, half, 2, dtype=F32) / half)
    ang = jnp.concatenate([row[:, None] * inv, col[:, None] * inv], axis=-1)
    cos, sin = jnp.cos(ang), jnp.sin(ang)
    cos = jnp.concatenate([jnp.ones((n_ctx, half), F32), cos], axis=0)
    sin = jnp.concatenate([jnp.zeros((n_ctx, half), F32), sin], axis=0)
    reps = LANES // C_HEAD_DIM
    return (jnp.tile(jnp.concatenate([cos, cos], axis=1), (1, reps)),
            jnp.tile(jnp.concatenate([-sin, sin], axis=1), (1, reps)))


def kernel(x, c, ctx, c_ctx, ada_w, ada_b, norm_g, rec_w_in, rec_w_out, rec_lb_logits, rec_w_g2, rec_b_g2,
           rec_gn_a, rec_gn_b, att_w_qkv, att_w_o, att_sink, ffn_w_in, ffn_w_out):
    bsz, seq, d = x.shape
    n_ctx = ctx.shape[1]
    depth = ada_w.shape[0]
    tm = TOKEN_TILE
    assert n_ctx % tm == 0 and seq % tm == 0 and n_ctx % SCAN_BLOCK == 0 and seq % SCAN_BLOCK == 0
    n_ctx_tiles = n_ctx // tm
    n_lat_tiles = seq // tm

    rows = -(-(bsz + 1) // 8) * 8
    cond = jnp.concatenate([c, c_ctx[None], jnp.zeros((rows - bsz - 1, d), F32)], axis=0)
    mod_all = _ada(cond, ada_w, ada_b)
    mods = []
    for l in range(depth):
        lat = mod_all[l, :bsz].reshape(bsz, 1, 6, d)
        cx = jnp.broadcast_to(mod_all[l, bsz].reshape(1, 1, 6, d), (bsz, 1, 6, d))
        mods.append(jnp.concatenate([cx, lat], axis=1))

    cos, sin = _rope_tables(seq, n_ctx)
    n_tiles = n_ctx_tiles + n_lat_tiles
    src = (ctx, x, 0)

    for l in range(depth):
        need_ctx = l < depth - 1
        j = l // 2
        ng = norm_g[l]
        stream = _Stream(src[0], src[1], n_ctx_tiles, 0, src[2])
        if l % 2 == 0:
            w, w2, b2 = _rec_weights(rec_w_in[j], rec_w_g2[j], rec_b_g2[j])
            p = _rec_in(stream, n_tiles, mods[l], ng, w, w2, b2)
            ya = _scan_hgrn(p, rec_lb_logits, rec_gn_a[j].reshape(1, -1), n_ctx // SCAN_BLOCK, j)
            yb = _scan_gla(p, rec_gn_b[j].reshape(1, -1), n_ctx // SCAN_BLOCK)
            y0, y1, c0, c1, wo = ya, yb, 0, 0, rec_w_out[j]
            y_has_ctx = True
        else:
            q, k2, vt = _qkv(stream, n_tiles, mods[l], ng, _att_weights(att_w_qkv[j]), cos, sin)
            y = _attention(q, k2, vt, att_sink[j], n_ctx)
            y0, y1, c0, c1, wo = y, y, 0, 1, att_w_o[j]
            y_has_ctx = False
        wo, win, wout = wo.astype(BF16), ffn_w_in[l].astype(BF16), ffn_w_out[l].astype(BF16)
        if need_ctx:
            assert y_has_ctx
            xcat = _post(y0, y1, c0, c1, stream, n_tiles, mods[l], ng, wo, win, wout)
            src = (xcat, xcat, n_ctx_tiles)
        else:
            if y_has_ctx:
                y0 = y0[:, n_ctx:]
                y1 = y1[:, n_ctx:]
            lat_only = _Stream(src[0], src[1], n_ctx_tiles, n_ctx_tiles, src[2])
            x_lat = _post(y0, y1, c0, c1, lat_only, n_lat_tiles, mods[l], ng, wo, win, wout)
    return x_lat
```

```python
import functools

import numpy as np
import jax
import jax.numpy as jnp
from jax import lax
from jax.experimental import pallas as pl
from jax.experimental.pallas import tpu as pltpu

F32 = jnp.float32
BF16 = jnp.bfloat16
HIGHEST = lax.Precision.HIGHEST

EPS = 1e-6
LANES = 128
TOKEN_TILE = 256
SCAN_BLOCK = 256
LA_CHUNK = 32
SCAN_GROUP = 4 * LA_CHUNK
ATT_BLOCK = 128
ATT_WINDOW = 128
ATT_QUERY_BLOCKS = 2
V_ROWS = 80
GRID_W = 64
ROPE_BASE = 10000.0
GLA_GATE_NORM = 16.0
A_HEADS, A_KEY, A_VAL = 4, 128, 128
B_HEADS, B_KEY, B_VAL = 4, 64, 128
B_GATE_RANK = 16
C_HEADS, C_KV_HEADS, C_HEAD_DIM = 16, 4, 64
VMEM_LIMIT = 56 * 1024 * 1024

NT = (((1,), (1,)), ((), ()))
TN = (((0,), (0,)), ((), ()))


def _sigmoid(x):
    return 0.5 * jnp.tanh(0.5 * x) + 0.5


def _silu(x):
    return x * _sigmoid(x)


def _log_sigmoid(x):
    return -(jnp.maximum(-x, 0.0) + jnp.log1p(jnp.exp(-jnp.abs(x))))


def _rms(x, g):
    ms = jnp.mean(x * x, axis=-1, keepdims=True)
    return (x * lax.rsqrt(ms + EPS)) * g


def _modulate(x, g, shift, scale):
    return _rms(x, g) * (1.0 + scale) + shift


def _const_spec(shape):
    nd = len(shape)
    return pl.BlockSpec(shape, lambda *_: (0,) * nd, pipeline_mode=pl.Buffered(1))


class _Stream:
    def __init__(self, ctx_arr, lat_arr, n_ctx_tiles, tile0=0, lat_off=0, nsub=1):
        self.arrays = (ctx_arr, lat_arr)
        self.n_ctx_tiles, self.tile0, self.lat_off, self.nsub = n_ctx_tiles, tile0, lat_off, nsub
        self.d = ctx_arr.shape[-1]

    def specs_for(self, s):
        nct, off, lat_off, nsub = self.n_ctx_tiles, self.tile0 + s, self.lat_off, self.nsub
        return [
            pl.BlockSpec((1, TOKEN_TILE, self.d), lambda b, i: (b, jnp.minimum(i * nsub + off, nct - 1), 0)),
            pl.BlockSpec((1, TOKEN_TILE, self.d),
                         lambda b, i: (b, jnp.maximum(i * nsub + off - nct, 0) + lat_off, 0)),
        ]

    def mod_spec_for(self, s):
        nct, off, nsub = self.n_ctx_tiles, self.tile0 + s, self.nsub
        return pl.BlockSpec((1, 1, 6, self.d), lambda b, i: (b, (i * nsub + off >= nct).astype(jnp.int32), 0, 0))

    def tile(self, ctx_ref, lat_ref, s):
        if self.tile0 >= self.n_ctx_tiles:
            return lat_ref[0]
        g = pl.program_id(1) * self.nsub + s + self.tile0
        return jnp.where(g < self.n_ctx_tiles, ctx_ref[0], lat_ref[0])

    def operands(self, mod):
        specs, args = [], []
        for s in range(self.nsub):
            specs += self.specs_for(s) + [self.mod_spec_for(s)]
            args += [*self.arrays, mod]
        return specs, args


def _sub_tiles(n_tiles):
    return next(n for n in (3, 2, 1) if n_tiles % n == 0)


def _staggered(nsub, stages):
    live = {}
    for step in range(nsub + len(stages) - 1):
        for k in range(len(stages)):
            s = step - k
            if 0 <= s < nsub:
                live[s] = stages[k](s, live.get(s))


def _ada_kernel(c_ref, w_ref, b_ref, o_ref):
    s = _silu(c_ref[...])
    o_ref[0] = jnp.dot(s, w_ref[0], preferred_element_type=F32, precision=HIGHEST) + b_ref[0]


def _ada(cond, ada_w, ada_b):
    depth, d, n = ada_w.shape
    rows = cond.shape[0]
    tn = n // 4
    return pl.pallas_call(
        _ada_kernel,
        grid=(depth, n // tn),
        in_specs=[
            pl.BlockSpec((rows, d), lambda l, j: (0, 0)),
            pl.BlockSpec((1, d, tn), lambda l, j: (l, 0, j)),
            pl.BlockSpec((1, 1, tn), lambda l, j: (l, 0, j)),
        ],
        out_specs=pl.BlockSpec((1, rows, tn), lambda l, j: (l, 0, j)),
        out_shape=jax.ShapeDtypeStruct((depth, rows, n), F32),
        compiler_params=pltpu.CompilerParams(vmem_limit_bytes=VMEM_LIMIT),
        name="ada",
    )(cond, ada_w, ada_b.reshape(depth, 1, n))


def _recin_kernel(*refs, n_main, stream, layer_slot):
    nsub = stream.nsub
    ng_ref, w_ref, w2_ref, b2_ref, lb_ref, o_ref = refs[3 * nsub:]
    tm = TOKEN_TILE
    lg = lb_ref[...]
    e = jnp.exp(lg - jnp.max(lg, axis=0, keepdims=True))
    lb = jnp.sum((e / jnp.sum(e, axis=0, keepdims=True))[:layer_slot + 1], axis=0)
    na, nb = A_HEADS * A_KEY, B_HEADS * B_KEY
    b0 = 3 * na + 2 * A_HEADS * A_VAL

    def modulated(s, _):
        xc_ref, xl_ref, mod_ref = refs[3 * s:3 * s + 3]
        m = mod_ref[0, 0]
        return _modulate(stream.tile(xc_ref, xl_ref, s), ng_ref[0:1], m[0:1], m[1:2]).astype(BF16)

    def project(s, u):
        rows = slice(s * tm, (s + 1) * tm)
        p = jnp.dot(u, w_ref[...], preferred_element_type=F32)
        pre = jnp.dot(p[:, n_main:].astype(BF16), w2_ref[...], preferred_element_type=F32) + b2_ref[...]
        o_ref[0, rows, :na] = _silu(p[:, :na]) * (A_KEY ** -0.5)
        for d in range(2):
            cols = slice((1 + d) * na, (2 + d) * na)
            o_ref[0, rows, cols] = jnp.log(lb[d:d + 1] + (1.0 - lb[d:d + 1]) * _sigmoid(p[:, cols]))
        o_ref[0, rows, 3 * na:b0] = p[:, 3 * na:b0]
        o_ref[0, rows, b0:b0 + nb] = p[:, b0:b0 + nb] * (B_KEY ** -0.5)
        o_ref[0, rows, b0 + nb:n_main] = p[:, b0 + nb:n_main]
        o_ref[0, rows, n_main:] = _log_sigmoid(pre) * (1.0 / GLA_GATE_NORM)

    _staggered(nsub, [modulated, project])


def _rec_in(stream, n_tiles, mod, ng, w, w2, b2, lb_logits, layer_slot):
    bsz = mod.shape[0]
    n_in = w.shape[1]
    n_main = n_in - LANES
    n_out = n_main + w2.shape[1]
    rows = stream.nsub * TOKEN_TILE
    assert n_tiles % stream.nsub == 0
    specs, args = stream.operands(mod)
    consts = [ng, w, w2, b2, lb_logits]
    return pl.pallas_call(
        functools.partial(_recin_kernel, n_main=n_main, stream=stream, layer_slot=layer_slot),
        grid=(bsz, n_tiles // stream.nsub),
        in_specs=specs + [_const_spec(a.shape) for a in consts],
        out_specs=pl.BlockSpec((1, rows, n_out), lambda b, i: (b, i, 0)),
        out_shape=jax.ShapeDtypeStruct((bsz, n_tiles * TOKEN_TILE, n_out), F32),
        compiler_params=pltpu.CompilerParams(vmem_limit_bytes=VMEM_LIMIT),
        name="rec_in",
    )(*args, *consts)


def _chunk_cumsum(g, reverse):
    r = g.shape[0]
    sl = 8
    x = g.reshape(r // sl, sl, LANES)
    sub = lax.broadcasted_iota(jnp.int32, x.shape, 1)
    for s in (1, 2, 4):
        if reverse:
            x = x + jnp.where(sub < sl - s, pltpu.roll(x, sl - s, 1), 0.0)
        else:
            x = x + jnp.where(sub >= s, pltpu.roll(x, s, 1), 0.0)
    per = LA_CHUNK // sl
    x = x.reshape(r // LA_CHUNK, per, sl, LANES)
    parts = [None] * per
    carry = None
    for v in (range(per - 1, -1, -1) if reverse else range(per)):
        xv = x[:, v:v + 1]
        parts[v] = xv if carry is None else xv + carry
        edge = xv[:, :, 0:1] if reverse else xv[:, :, sl - 1:sl]
        carry = edge if carry is None else carry + edge
    return jnp.concatenate(parts, axis=1).reshape(r, LANES)


def _group_masks(reverse):
    n = SCAN_GROUP
    ri = lax.broadcasted_iota(jnp.int32, (n, n), 0)
    ci = lax.broadcasted_iota(jnp.int32, (n, n), 1)
    rc, cc = ri // LA_CHUNK, ci // LA_CHUNK
    if reverse:
        return (rc == cc) & (ci >= ri), (rc + 1 == cc) & (rc % 2 == 0), (rc < 2) & (cc >= 2)
    return (rc == cc) & (ci <= ri), (rc == cc + 1) & (rc % 2 == 1), (rc >= 2) & (cc < 2)


def _scale_chunks(x, factors):
    parts = []
    for c, f in enumerate(factors):
        xc = x[c * LA_CHUNK:(c + 1) * LA_CHUNK]
        parts.append(xc if f is None else xc * f)
    return jnp.concatenate(parts, axis=0)


def _scan_prep(q, k, g, reverse):
    bc = _chunk_cumsum(g, reverse)
    e = jnp.exp(bc)
    return bc, q * e, k / e


def _scan_group(gi, prep, v_bf, states, masks, lane_masks, reverse):
    bc, q_dec, k_inv = prep
    nsub = len(states)
    nch = SCAN_GROUP // LA_CHUNK
    assert nch == 4
    m_diag, m_adj, m_far = masks
    pos = (lambda c: nch - 1 - c) if reverse else (lambda c: c)
    edge = 0 if reverse else LA_CHUNK - 1
    r0 = gi * SCAN_GROUP
    rs = slice(r0, r0 + SCAN_GROUP)
    tot = [bc[r0 + c * LA_CHUNK + edge:r0 + c * LA_CHUNK + edge + 1] for c in range(nch)]
    ts = [tot[pos(s)] for s in range(nch)]
    e_ts = [jnp.exp(t) for t in ts]
    e_pre = [None, e_ts[0], jnp.exp(ts[0] + ts[1]), jnp.exp(ts[0] + ts[1] + ts[2])]
    e_post = [jnp.exp(ts[1] + ts[2] + ts[3]), jnp.exp(ts[2] + ts[3]), e_ts[3], None]
    e_total = jnp.exp(ts[0] + ts[1] + ts[2] + ts[3])
    ki = k_inv[rs]
    ke = _scale_chunks(ki, [e_ts[pos(c)] for c in range(nch)])
    kh = _scale_chunks(ke, [e_post[pos(c)] for c in range(nch)])
    kb = _scale_chunks(ke, [e_ts[1] if pos(c) == 0 else None for c in range(nch)])
    kcat = jnp.concatenate([ki.astype(BF16), ke.astype(BF16)], axis=0)
    kb, kh = kb.astype(BF16), kh.astype(BF16)
    outs, new_states = [], []
    for j in range(nsub):
        qd = q_dec[rs]
        if lane_masks is not None:
            qd = jnp.where(lane_masks[j], qd, 0.0)
        qt = _scale_chunks(qd, [e_pre[pos(c)] for c in range(nch)]).astype(BF16)
        qb = _scale_chunks(qd, [e_ts[2] if pos(c) == 3 else None for c in range(nch)]).astype(BF16)
        qd = qd.astype(BF16)
        v_g = v_bf[rs, j * LANES:(j + 1) * LANES]
        s12 = lax.dot_general(qd, kcat, NT, preferred_element_type=F32)
        s3 = lax.dot_general(qb, kb, NT, preferred_element_type=F32)
        o_state = lax.dot_general(qt, states[j].astype(BF16), NT, preferred_element_type=F32)
        upd = lax.dot_general(v_g, kh, TN, preferred_element_type=F32)
        sc = jnp.where(m_diag, s12[:, :SCAN_GROUP],
                       jnp.where(m_adj, s12[:, SCAN_GROUP:], jnp.where(m_far, s3, 0.0)))
        outs.append(o_state + jnp.dot(sc.astype(BF16), v_g, preferred_element_type=F32))
        new_states.append(states[j] * e_total + upd)
    return outs, tuple(new_states)


def _scan_kernel(*refs, mode, n_ctx_blocks):
    if mode == "hgrn":
        (q_ref, gf_ref, gb_ref, v_ref, og_ref, gn_ref, o_ref, of_ref, ob_ref) = refs
        nsub = 1
    else:
        (q_ref, k_ref, gf_ref, gb_ref, v_ref, og_ref, gn_ref, o_ref, of_ref, ob_ref) = refs
        nsub = 2
    t = q_ref.shape[1]
    blk = SCAN_BLOCK
    nblk = t // blk
    masks_f, masks_b = _group_masks(False), _group_masks(True)
    if nsub == 2:
        lane = lax.broadcasted_iota(jnp.int32, (SCAN_GROUP, LANES), 1)
        lane_masks = [lane < B_KEY, lane >= B_KEY]
    else:
        lane_masks = None

    def features(rows, d):
        g = (gf_ref, gb_ref)[d][0, rows, :]
        if mode == "hgrn":
            return q_ref[0, rows, :], 1.0 - jnp.exp(g), g
        return q_ref[0, rows, :], k_ref[0, rows, :], g

    ngroup = blk // SCAN_GROUP

    def prep(i):
        jb = n_ctx_blocks - 1 - i if i < n_ctx_blocks else nblk - 1 - (i - n_ctx_blocks)
        rows_f = slice(i * blk, (i + 1) * blk)
        rows_b = slice(jb * blk, (jb + 1) * blk)
        return (jb, _scan_prep(*features(rows_f, 0), False), _scan_prep(*features(rows_b, 1), True),
                v_ref[0, rows_f, :].astype(BF16), v_ref[0, rows_b, :].astype(BF16))

    def finish(blk_i):
        rows = slice(blk_i * blk, (blk_i + 1) * blk)
        for j in range(nsub):
            cols = slice(j * LANES, (j + 1) * LANES)
            o = of_ref[rows, cols] + ob_ref[rows, cols]
            y = _rms(o, gn_ref[...]) * _silu(og_ref[0, rows, cols])
            o_ref[0, rows, cols] = y.astype(o_ref.dtype)

    st_f = st_b = tuple(jnp.zeros((LANES, LANES), F32) for _ in range(nsub))
    seen = {}
    nxt = prep(0)
    for i in range(nblk):
        jb, prep_f, prep_b, v_f, v_b = nxt
        if i + 1 < nblk:
            nxt = prep(i + 1)
        for s in range(ngroup):
            gf, gb = s, ngroup - 1 - s
            o, st_f = _scan_group(gf, prep_f, v_f, st_f, masks_f, lane_masks, False)
            r0 = i * blk + gf * SCAN_GROUP
            for j in range(nsub):
                of_ref[r0:r0 + SCAN_GROUP, j * LANES:(j + 1) * LANES] = o[j]
            o, st_b = _scan_group(gb, prep_b, v_b, st_b, masks_b, lane_masks, True)
            r0 = jb * blk + gb * SCAN_GROUP
            for j in range(nsub):
                ob_ref[r0:r0 + SCAN_GROUP, j * LANES:(j + 1) * LANES] = o[j]
        for done in (i, jb):
            seen[done] = seen.get(done, 0) + 1
            if seen[done] == 2:
                finish(done)


def _scan_hgrn(p, gn, n_ctx_blocks):
    bsz, t, _ = p.shape
    col = lambda base: pl.BlockSpec((1, t, LANES), lambda b, h, base=base: (b, 0, base + h))
    return pl.pallas_call(
        functools.partial(_scan_kernel, mode="hgrn", n_ctx_blocks=n_ctx_blocks),
        grid=(bsz, A_HEADS),
        in_specs=[col(0), col(4), col(8), col(12), col(16), _const_spec(gn.shape)],
        out_specs=pl.BlockSpec((1, t, LANES), lambda b, h: (b, 0, h)),
        out_shape=jax.ShapeDtypeStruct((bsz, t, A_HEADS * A_VAL), BF16),
        scratch_shapes=[pltpu.VMEM((t, LANES), F32), pltpu.VMEM((t, LANES), F32)],
        compiler_params=pltpu.CompilerParams(vmem_limit_bytes=VMEM_LIMIT),
        name="scan_hgrn",
    )(p, p, p, p, p, gn)


def _scan_gla(p, gn, n_ctx_blocks):
    bsz, t, _ = p.shape
    col = lambda base: pl.BlockSpec((1, t, LANES), lambda b, h, base=base: (b, 0, base + h))
    wide = lambda base: pl.BlockSpec((1, t, 2 * LANES), lambda b, h, base=base: (b, 0, base + h))
    return pl.pallas_call(
        functools.partial(_scan_kernel, mode="gla", n_ctx_blocks=n_ctx_blocks),
        grid=(bsz, B_HEADS // 2),
        in_specs=[col(20), col(22), col(32), col(34), wide(12), wide(14),
                  _const_spec(gn.shape)],
        out_specs=pl.BlockSpec((1, t, 2 * LANES), lambda b, h: (b, 0, h)),
        out_shape=jax.ShapeDtypeStruct((bsz, t, B_HEADS * B_VAL), BF16),
        scratch_shapes=[pltpu.VMEM((t, 2 * LANES), F32), pltpu.VMEM((t, 2 * LANES), F32)],
        compiler_params=pltpu.CompilerParams(vmem_limit_bytes=VMEM_LIMIT),
        name="scan_gla",
    )(p, p, p, p, p, p, gn)


def _post_kernel(*refs, stream):
    nsub = stream.nsub
    y_refs = refs[3 * nsub:5 * nsub]
    ng_ref, wo_ref, win_ref, wout_ref, o_ref = refs[5 * nsub:]
    hid = wout_ref.shape[0]
    tm = TOKEN_TILE

    def mix(s, _):
        xc_ref, xl_ref, mod_ref = refs[3 * s:3 * s + 3]
        y0_ref, y1_ref = y_refs[2 * s:2 * s + 2]
        m = mod_ref[0, 0]
        half = y0_ref.shape[2]
        t = jnp.dot(y0_ref[0], wo_ref[:half, :], preferred_element_type=F32)
        t = t + jnp.dot(y1_ref[0], wo_ref[half:, :], preferred_element_type=F32)
        x1 = stream.tile(xc_ref, xl_ref, s) + m[2:3] * _rms(t, ng_ref[1:2])
        return x1, _modulate(x1, ng_ref[2:3], m[3:4], m[4:5]).astype(BF16)

    def ffn_in(s, carry):
        x1, u = carry
        h = jnp.dot(u, win_ref[...], preferred_element_type=F32)
        return x1, (_silu(h[:, :hid]) * h[:, hid:]).astype(BF16)

    def ffn_out(s, carry):
        x1, act = carry
        m = refs[3 * s + 2][0, 0]
        h2 = jnp.dot(act, wout_ref[...], preferred_element_type=F32)
        o_ref[0, s * tm:(s + 1) * tm, :] = x1 + m[5:6] * _rms(h2, ng_ref[3:4])

    _staggered(nsub, [mix, ffn_in, ffn_out])


def _post(y0, y1, y_col0, y_col1, stream, n_tiles, mod, ng, wo, win, wout):
    bsz, d = mod.shape[0], mod.shape[-1]
    tm = TOKEN_TILE
    half = d // 2
    nsub = stream.nsub
    assert n_tiles % nsub == 0
    specs, args = stream.operands(mod)
    for s in range(nsub):
        specs += [pl.BlockSpec((1, tm, half), lambda b, i, s=s: (b, i * nsub + s, y_col0)),
                  pl.BlockSpec((1, tm, half), lambda b, i, s=s: (b, i * nsub + s, y_col1))]
        args += [y0, y1]
    consts = [ng, wo, win, wout]
    return pl.pallas_call(
        functools.partial(_post_kernel, stream=stream),
        grid=(bsz, n_tiles // nsub),
        in_specs=specs + [_const_spec(a.shape) for a in consts],
        out_specs=pl.BlockSpec((1, nsub * tm, d), lambda b, i: (b, i, 0)),
        out_shape=jax.ShapeDtypeStruct((bsz, n_tiles * tm, d), F32),
        compiler_params=pltpu.CompilerParams(vmem_limit_bytes=VMEM_LIMIT),
        name="post",
    )(*args, *consts)


def _rope(x, cos, sin_signed, first_half):
    partner = jnp.where(first_half, pltpu.roll(x, LANES - 32, 1), pltpu.roll(x, 32, 1))
    return x * cos + partner * sin_signed


def _qkv_kernel(*refs, stream):
    nsub = stream.nsub
    ng_ref, w_ref, cos_ref, sin_ref, q_ref, k_ref, vt_ref = refs[3 * nsub:]
    tm = TOKEN_TILE
    nq, nk = q_ref.shape[2], k_ref.shape[2]
    lane = lax.broadcasted_iota(jnp.int32, (tm, LANES), 1)
    first_half = (lane % C_HEAD_DIM) < (C_HEAD_DIM // 2)
    ones_rows = (lane >= C_HEAD_DIM) & (lane < V_ROWS)

    def modulated(s, _):
        xc_ref, xl_ref, mod_ref = refs[3 * s:3 * s + 3]
        m = mod_ref[0, 0]
        return _modulate(stream.tile(xc_ref, xl_ref, s), ng_ref[0:1], m[0:1], m[1:2]).astype(BF16)

    def project(s, u):
        rows = slice(s * tm, (s + 1) * tm)
        p = jnp.dot(u, w_ref[...], preferred_element_type=F32)
        cos, sin = cos_ref[rows, :], sin_ref[rows, :]
        for j in range(nq // LANES):
            cols = slice(j * LANES, (j + 1) * LANES)
            q_ref[0, rows, cols] = (_rope(p[:, cols], cos, sin, first_half) * (C_HEAD_DIM ** -0.5)).astype(BF16)
        for j in range(nk // LANES):
            cols = slice(nq + j * LANES, nq + (j + 1) * LANES)
            k_ref[0, rows, j * LANES:(j + 1) * LANES] = _rope(p[:, cols], cos, sin, first_half).astype(BF16)
        for j in range(C_KV_HEADS):
            cols = slice(nq + nk + j * LANES, nq + nk + (j + 1) * LANES)
            vt_ref[0, j * LANES:(j + 1) * LANES, rows] = jnp.where(ones_rows, 1.0, p[:, cols]).T.astype(BF16)

    _staggered(nsub, [modulated, project])


def _qkv(stream, n_tiles, mod, ng, w, cos, sin):
    bsz = mod.shape[0]
    rows = stream.nsub * TOKEN_TILE
    assert n_tiles % stream.nsub == 0
    t = n_tiles * TOKEN_TILE
    nq = C_HEADS * C_HEAD_DIM
    nk = C_KV_HEADS * LANES
    specs, args = stream.operands(mod)
    return pl.pallas_call(
        functools.partial(_qkv_kernel, stream=stream),
        grid=(bsz, n_tiles // stream.nsub),
        in_specs=specs + [
            _const_spec(ng.shape), _const_spec(w.shape),
            pl.BlockSpec((rows, LANES), lambda b, i: (i, 0)),
            pl.BlockSpec((rows, LANES), lambda b, i: (i, 0)),
        ],
        out_specs=[
            pl.BlockSpec((1, rows, nq), lambda b, i: (b, i, 0)),
            pl.BlockSpec((1, rows, nk), lambda b, i: (b, i, 0)),
            pl.BlockSpec((1, nk, rows), lambda b, i: (b, 0, i)),
        ],
        out_shape=[jax.ShapeDtypeStruct((bsz, t, nq), BF16),
                   jax.ShapeDtypeStruct((bsz, t, nk), BF16),
                   jax.ShapeDtypeStruct((bsz, nk, t), BF16)],
        compiler_params=pltpu.CompilerParams(vmem_limit_bytes=VMEM_LIMIT),
        name="qkv",
    )(*args, ng, w, cos, sin)


def _attn_kernel(sink_ref, bias_ref, q_ref, k_ref, vt_ref, o_ref, *, n_ctx):
    blk = ATT_BLOCK
    band = 3 * blk
    t = k_ref.shape[1]
    group = C_HEADS // C_KV_HEADS
    nsub = q_ref.shape[1] // blk
    lane = lax.broadcasted_iota(jnp.int32, (blk, LANES), 1)
    low = lane < C_HEAD_DIM
    zero = jnp.zeros((blk, LANES), BF16)

    def band_start(sb):
        n = pl.program_id(1) * nsub + sb
        return pl.multiple_of(jnp.minimum(n_ctx + (n - 1) * blk, t - band), blk)

    def scores(sb, kg):
        cols = slice(kg * LANES, (kg + 1) * LANES)
        r_band = band_start(sb)
        parts, sinks = [], []
        for hh in range(group):
            h = kg * group + hh
            pair = q_ref[0, sb * blk:(sb + 1) * blk, (h // 2) * LANES:(h // 2 + 1) * LANES]
            parts.append(jnp.where(low if h % 2 == 0 else ~low, pair, zero))
            sinks.append(jnp.full((1, blk), sink_ref[h], F32))
        q4 = jnp.concatenate(parts, axis=0)
        sink = jnp.concatenate(sinks, axis=1)
        kk = jnp.concatenate([k_ref[0, pl.ds(r_band, band), cols], k_ref[0, 0:n_ctx, cols]], axis=0)
        s = lax.dot_general(kk, q4, NT, preferred_element_type=F32)
        bias = jnp.concatenate([bias_ref[sb]] * group, axis=1)
        s = jnp.concatenate([s[:band] + bias, s[band:]], axis=0)
        mx = jnp.maximum(jnp.max(s, axis=0, keepdims=True), sink)
        return s, mx, sink

    def probs(s, mx, sink):
        return jnp.exp((s - mx).astype(BF16)), jnp.exp(sink - mx)

    def finish(sb, kg, p, p_sink):
        r_band = band_start(sb)
        rows = slice(kg * LANES, kg * LANES + V_ROWS)
        vt = jnp.concatenate([vt_ref[0, rows, pl.ds(r_band, band)], vt_ref[0, rows, 0:n_ctx]], axis=1)
        o3 = jnp.dot(vt, p, preferred_element_type=F32)
        o = o3[:C_HEAD_DIM] * (1.0 / (o3[C_HEAD_DIM:C_HEAD_DIM + 1] + p_sink))
        for pp in range(group // 2):
            y = jnp.concatenate([o[:, (2 * pp) * blk:(2 * pp + 1) * blk],
                                 o[:, (2 * pp + 1) * blk:(2 * pp + 2) * blk]], axis=0)
            c0 = (kg * (group // 2) + pp) * LANES
            o_ref[0, sb * blk:(sb + 1) * blk, c0:c0 + LANES] = y.T.astype(o_ref.dtype)

    units = [(sb, kg) for sb in range(nsub) for kg in range(C_KV_HEADS)]
    nu = len(units)
    s_q = {0: scores(*units[0]), 1: scores(*units[1])}
    p_q = {0: probs(*s_q.pop(0))}
    for i in range(nu):
        if i + 2 < nu:
            s_q[i + 2] = scores(*units[i + 2])
        if i + 1 < nu:
            p_q[i + 1] = probs(*s_q.pop(i + 1))
        finish(*units[i], *p_q.pop(i))


def _attn_bias(n_blocks, n_ctx, t):
    blk, band = ATT_BLOCK, 3 * ATT_BLOCK
    n = np.arange(n_blocks)[:, None, None]
    r_band = np.minimum(n_ctx + (n - 1) * blk, t - band)
    kp = np.arange(band)[None, :, None] + (r_band - n_ctx)
    qi = np.arange(blk)[None, None, :] + n * blk
    ok = (np.abs(kp - qi) <= ATT_WINDOW) & (kp >= 0)
    return jnp.asarray(np.where(ok, 0.0, -1e30).astype(np.float32))


def _attention(q, k2, vt, sink, n_ctx):
    bsz, t, nq = q.shape
    nk = k2.shape[2]
    rows = ATT_QUERY_BLOCKS * ATT_BLOCK
    n_blocks = (t - n_ctx) // ATT_BLOCK
    assert n_ctx % rows == 0 and (t - n_ctx) % rows == 0 and n_ctx >= ATT_BLOCK and n_blocks >= 2
    bias = _attn_bias(n_blocks, n_ctx, t)
    return pl.pallas_call(
        functools.partial(_attn_kernel, n_ctx=n_ctx),
        grid=(bsz, n_blocks // ATT_QUERY_BLOCKS),
        in_specs=[
            pl.BlockSpec(memory_space=pltpu.SMEM),
            pl.BlockSpec((ATT_QUERY_BLOCKS,) + bias.shape[1:], lambda b, n: (n, 0, 0)),
            pl.BlockSpec((1, rows, nq), lambda b, n: (b, n + n_ctx // rows, 0)),
            pl.BlockSpec((1, t, nk), lambda b, n: (b, 0, 0)),
            pl.BlockSpec((1, nk, t), lambda b, n: (b, 0, 0)),
        ],
        out_specs=pl.BlockSpec((1, rows, nq), lambda b, n: (b, n, 0)),
        out_shape=jax.ShapeDtypeStruct((bsz, t - n_ctx, nq), BF16),
        compiler_params=pltpu.CompilerParams(vmem_limit_bytes=VMEM_LIMIT),
        name="attention",
    )(sink, bias, q, k2, vt)


def _rec_weights(w_in, w_g2, b_g2):
    widths = (A_HEADS * A_KEY,) * 3 + (A_HEADS * A_VAL,) * 2 + (B_HEADS * B_KEY,) * 2 + (
        B_HEADS * B_VAL, B_GATE_RANK, B_GATE_RANK, B_HEADS * B_VAL)
    off = np.concatenate([[0], np.cumsum(widths)])
    seg = lambda i: w_in[:, off[i]:off[i + 1]]
    d = w_in.shape[0]
    pad = jnp.zeros((d, LANES - 2 * B_GATE_RANK), w_in.dtype)
    w = jnp.concatenate([seg(i) for i in (0, 1, 2, 3, 4, 5, 6, 7, 10, 8, 9)] + [pad], axis=1).astype(BF16)
    nk = B_HEADS * B_KEY
    w2 = jnp.zeros((LANES, 2 * nk), F32)
    w2 = w2.at[:B_GATE_RANK, :nk].set(w_g2[0]).at[B_GATE_RANK:2 * B_GATE_RANK, nk:].set(w_g2[1])
    return w, w2.astype(BF16), b_g2.reshape(1, 2 * nk)


def _att_weights(w_qkv):
    d = w_qkv.shape[0]
    nq, nkv = C_HEADS * C_HEAD_DIM, C_KV_HEADS * C_HEAD_DIM
    perm = np.concatenate([np.arange(0, C_HEAD_DIM, 2), np.arange(1, C_HEAD_DIM, 2)])
    wq = w_qkv[:, :nq].reshape(d, C_HEADS, C_HEAD_DIM)[:, :, perm].reshape(d, nq)
    wk = w_qkv[:, nq:nq + nkv].reshape(d, C_KV_HEADS, C_HEAD_DIM)[:, :, perm]
    wv = w_qkv[:, nq + nkv:].reshape(d, C_KV_HEADS, C_HEAD_DIM)
    wk = jnp.concatenate([wk, wk], axis=2).reshape(d, 2 * nkv)
    wv = jnp.concatenate([wv, jnp.zeros_like(wv)], axis=2).reshape(d, 2 * nkv)
    return jnp.concatenate([wq, wk, wv], axis=1).astype(BF16)


def _rope_tables(seq, n_ctx):
    n_rows = seq // GRID_W
    row = jnp.repeat(jnp.arange(n_rows), GRID_W).astype(F32)
    col = jnp.tile(jnp.arange(GRID_W), n_rows).astype(F32)
    half = C_HEAD_DIM // 2
    inv = ROPE_BASE ** (-jnp.arange(0, half, 2, dtype=F32) / half)
    ang = jnp.concatenate([row[:, None] * inv, col[:, None] * inv], axis=-1)
    cos, sin = jnp.cos(ang), jnp.sin(ang)
    cos = jnp.concatenate([jnp.ones((n_ctx, half), F32), cos], axis=0)
    sin = jnp.concatenate([jnp.zeros((n_ctx, half), F32), sin], axis=0)
    reps = LANES // C_HEAD_DIM
    return (jnp.tile(jnp.concatenate([cos, cos], axis=1), (1, reps)),
            jnp.tile(jnp.concatenate([-sin, sin], axis=1), (1, reps)))


def kernel(x, c, ctx, c_ctx, ada_w, ada_b, norm_g, rec_w_in, rec_w_out, rec_lb_logits, rec_w_g2, rec_b_g2,
           rec_gn_a, rec_gn_b, att_w_qkv, att_w_o, att_sink, ffn_w_in, ffn_w_out):
    bsz, seq, d = x.shape
    n_ctx = ctx.shape[1]
    depth = ada_w.shape[0]
    tm = TOKEN_TILE
    assert n_ctx % tm == 0 and seq % tm == 0 and n_ctx % SCAN_BLOCK == 0 and seq % SCAN_BLOCK == 0
    n_ctx_tiles = n_ctx // tm
    n_lat_tiles = seq // tm

    rows = -(-(bsz + 1) // 8) * 8
    cond = jnp.concatenate([c, c_ctx[None], jnp.zeros((rows - bsz - 1, d), F32)], axis=0)
    mod_all = _ada(cond, ada_w, ada_b)
    mods = []
    for l in range(depth):
        lat = mod_all[l, :bsz].reshape(bsz, 1, 6, d)
        cx = jnp.broadcast_to(mod_all[l, bsz].reshape(1, 1, 6, d), (bsz, 1, 6, d))
        mods.append(jnp.concatenate([cx, lat], axis=1))

    cos, sin = _rope_tables(seq, n_ctx)
    n_tiles = n_ctx_tiles + n_lat_tiles
    src = (ctx, x, 0)

    for l in range(depth):
        need_ctx = l < depth - 1
        j = l // 2
        ng = norm_g[l]
        stream = _Stream(src[0], src[1], n_ctx_tiles, 0, src[2], _sub_tiles(n_tiles))
        if l % 2 == 0:
            w, w2, b2 = _rec_weights(rec_w_in[j], rec_w_g2[j], rec_b_g2[j])
            p = _rec_in(stream, n_tiles, mods[l], ng, w, w2, b2, rec_lb_logits, j)
            ya = _scan_hgrn(p, rec_gn_a[j].reshape(1, -1), n_ctx // SCAN_BLOCK)
            yb = _scan_gla(p, rec_gn_b[j].reshape(1, -1), n_ctx // SCAN_BLOCK)
            y0, y1, c0, c1, wo = ya, yb, 0, 0, rec_w_out[j]
            y_has_ctx = True
        else:
            q, k2, vt = _qkv(stream, n_tiles, mods[l], ng, _att_weights(att_w_qkv[j]), cos, sin)
            y = _attention(q, k2, vt, att_sink[j], n_ctx)
            y0, y1, c0, c1, wo = y, y, 0, 1, att_w_o[j]
            y_has_ctx = False
        wo, win, wout = wo.astype(BF16), ffn_w_in[l].astype(BF16), ffn_w_out[l].astype(BF16)
        if need_ctx:
            assert y_has_ctx
            xcat = _post(y0, y1, c0, c1, stream, n_tiles, mods[l], ng, wo, win, wout)
            src = (xcat, xcat, n_ctx_tiles)
        else:
            if y_has_ctx:
                y0 = y0[:, n_ctx:]
                y1 = y1[:, n_ctx:]
            lat_only = _Stream(src[0], src[1], n_ctx_tiles, n_ctx_tiles, src[2], _sub_tiles(n_lat_tiles))
            x_lat = _post(y0, y1, c0, c1, lat_only, n_lat_tiles, mods[l], ng, wo, win, wout)
    return x_lat
```

```python
import functools

import numpy as np
import jax
import jax.numpy as jnp
from jax import lax
from jax.experimental import pallas as pl
from jax.experimental.pallas import tpu as pltpu

F32 = jnp.float32
BF16 = jnp.bfloat16
HIGHEST = lax.Precision.HIGHEST

EPS = 1e-6
LANES = 128
TOKEN_TILE = 256
SCAN_BLOCK = 256
LA_CHUNK = 32
SCAN_GROUP = 4 * LA_CHUNK
ATT_BLOCK = 128
ATT_WINDOW = 128
ATT_QUERY_BLOCKS = 4
V_ROWS = 80
GRID_W = 64
ROPE_BASE = 10000.0
GLA_GATE_NORM = 16.0
A_HEADS, A_KEY, A_VAL = 4, 128, 128
B_HEADS, B_KEY, B_VAL = 4, 64, 128
B_GATE_RANK = 16
C_HEADS, C_KV_HEADS, C_HEAD_DIM = 16, 4, 64
VMEM_LIMIT = 56 * 1024 * 1024

NT = (((1,), (1,)), ((), ()))
TN = (((0,), (0,)), ((), ()))


def _silu(x):
    h = 0.5 * x
    return h * (jnp.tanh(h) + 1.0)


def _log_sigmoid(x):
    return jnp.minimum(x, 0.0) - jnp.log(1.0 + jnp.exp(-jnp.abs(x)))


def _rms(x, g):
    ms = jnp.mean(x * x, axis=-1, keepdims=True)
    return (x * lax.rsqrt(ms + EPS)) * g


def _modulate(x, g, shift, scale):
    return _rms(x, g) * (1.0 + scale) + shift


def _const_spec(shape):
    nd = len(shape)
    return pl.BlockSpec(shape, lambda *_: (0,) * nd, pipeline_mode=pl.Buffered(1))


class _Stream:
    def __init__(self, ctx_arr, lat_arr, n_ctx_tiles, tile0=0, lat_off=0, nsub=1):
        self.arrays = (ctx_arr, lat_arr)
        self.n_ctx_tiles, self.tile0, self.lat_off, self.nsub = n_ctx_tiles, tile0, lat_off, nsub
        self.d = ctx_arr.shape[-1]

    def specs_for(self, s):
        nct, off, lat_off, nsub = self.n_ctx_tiles, self.tile0 + s, self.lat_off, self.nsub
        return [
            pl.BlockSpec((1, TOKEN_TILE, self.d), lambda b, i: (b, jnp.minimum(i * nsub + off, nct - 1), 0)),
            pl.BlockSpec((1, TOKEN_TILE, self.d),
                         lambda b, i: (b, jnp.maximum(i * nsub + off - nct, 0) + lat_off, 0)),
        ]

    def mod_spec_for(self, s):
        nct, off, nsub = self.n_ctx_tiles, self.tile0 + s, self.nsub
        return pl.BlockSpec((1, 1, 6, self.d), lambda b, i: (b, (i * nsub + off >= nct).astype(jnp.int32), 0, 0))

    def tile(self, ctx_ref, lat_ref, s):
        if self.tile0 >= self.n_ctx_tiles:
            return lat_ref[0]
        g = pl.program_id(1) * self.nsub + s + self.tile0
        return jnp.where(g < self.n_ctx_tiles, ctx_ref[0], lat_ref[0])

    def operands(self, mod):
        specs, args = [], []
        for s in range(self.nsub):
            specs += self.specs_for(s) + [self.mod_spec_for(s)]
            args += [*self.arrays, mod]
        return specs, args


def _sub_tiles(n_tiles):
    return next(n for n in (3, 2, 1) if n_tiles % n == 0)


def _staggered(nsub, stages):
    live = {}
    for step in range(nsub + len(stages) - 1):
        for k in range(len(stages)):
            s = step - k
            if 0 <= s < nsub:
                live[s] = stages[k](s, live.get(s))


def _ada_kernel(c_ref, w_ref, b_ref, o_ref):
    s = _silu(c_ref[...])
    o_ref[0] = jnp.dot(s, w_ref[0], preferred_element_type=F32, precision=HIGHEST) + b_ref[0]


def _ada(cond, ada_w, ada_b):
    depth, d, n = ada_w.shape
    rows = cond.shape[0]
    tn = n // 4
    return pl.pallas_call(
        _ada_kernel,
        grid=(depth, n // tn),
        in_specs=[
            pl.BlockSpec((rows, d), lambda l, j: (0, 0)),
            pl.BlockSpec((1, d, tn), lambda l, j: (l, 0, j)),
            pl.BlockSpec((1, 1, tn), lambda l, j: (l, 0, j)),
        ],
        out_specs=pl.BlockSpec((1, rows, tn), lambda l, j: (l, 0, j)),
        out_shape=jax.ShapeDtypeStruct((depth, rows, n), F32),
        compiler_params=pltpu.CompilerParams(vmem_limit_bytes=VMEM_LIMIT),
        name="ada",
    )(cond, ada_w, ada_b.reshape(depth, 1, n))


def _recin_kernel(*refs, stream, layer_slot):
    nsub = stream.nsub
    ng_ref, w_ref, w2_ref, b2_ref, lb_ref, o_ref = refs[3 * nsub:]
    tm = TOKEN_TILE
    lg = lb_ref[...]
    e = jnp.exp(lg - jnp.max(lg, axis=0, keepdims=True))
    lb = jnp.sum((e / jnp.sum(e, axis=0, keepdims=True))[:layer_slot + 1], axis=0)
    na, nb = A_HEADS * A_KEY, B_HEADS * B_KEY
    n_act = 3 * na + nb

    def modulated(s, _):
        xc_ref, xl_ref, mod_ref = refs[3 * s:3 * s + 3]
        m = mod_ref[0, 0]
        return _modulate(stream.tile(xc_ref, xl_ref, s), ng_ref[0:1], m[0:1], m[1:2]).astype(BF16)

    def project(s, u):
        rows = slice(s * tm, (s + 1) * tm)
        p = jnp.dot(u, w_ref[:, :n_act + LANES], preferred_element_type=F32)
        o_ref[0, rows, :n_act] = p[:, :n_act]
        pre = jnp.dot(p[:, n_act:].astype(BF16), w2_ref[...], preferred_element_type=F32)
        o_ref[0, rows, n_act:n_act + 2 * nb] = pre + b2_ref[...]
        o_ref[0, rows, n_act + 2 * nb:] = jnp.dot(u, w_ref[:, n_act + LANES:], preferred_element_type=F32)

    def activate(s, _):
        rows = slice(s * tm, (s + 1) * tm)
        for c0 in range(0, n_act + 2 * nb, LANES):
            cols = slice(c0, c0 + LANES)
            v = o_ref[0, rows, cols]
            if c0 < na:
                v = _silu(v) * (A_KEY ** -0.5)
            elif c0 < 3 * na:
                d, h0 = divmod(c0 - na, na)
                lbd = lb[d:d + 1, h0:h0 + LANES]
                v = jnp.log(0.5 * (1.0 + lbd) + (0.5 * (1.0 - lbd)) * jnp.tanh(0.5 * v))
            elif c0 < n_act:
                v = v * (B_KEY ** -0.5)
            else:
                v = _log_sigmoid(v) * (1.0 / GLA_GATE_NORM)
            o_ref[0, rows, cols] = v

    _staggered(nsub, [modulated, project, activate])


def _rec_in(stream, n_tiles, mod, ng, w, w2, b2, lb_logits, layer_slot):
    bsz = mod.shape[0]
    n_out = w.shape[1] - LANES + w2.shape[1]
    rows = stream.nsub * TOKEN_TILE
    assert n_tiles % stream.nsub == 0
    specs, args = stream.operands(mod)
    consts = [ng, w, w2, b2, lb_logits]
    return pl.pallas_call(
        functools.partial(_recin_kernel, stream=stream, layer_slot=layer_slot),
        grid=(bsz, n_tiles // stream.nsub),
        in_specs=specs + [_const_spec(a.shape) for a in consts],
        out_specs=pl.BlockSpec((1, rows, n_out), lambda b, i: (b, i, 0)),
        out_shape=jax.ShapeDtypeStruct((bsz, n_tiles * TOKEN_TILE, n_out), F32),
        compiler_params=pltpu.CompilerParams(vmem_limit_bytes=VMEM_LIMIT),
        name="rec_in",
    )(*args, *consts)


def _chunk_cumsum(g, reverse):
    r, width = g.shape
    sl = 8
    x = g.reshape(r // sl, sl, width)
    sub = lax.broadcasted_iota(jnp.int32, x.shape, 1)
    for s in (1, 2, 4):
        if reverse:
            x = x + jnp.where(sub < sl - s, pltpu.roll(x, sl - s, 1), 0.0)
        else:
            x = x + jnp.where(sub >= s, pltpu.roll(x, s, 1), 0.0)
    per = LA_CHUNK // sl
    x = x.reshape(r // LA_CHUNK, per, sl, width)
    parts = [None] * per
    carry = None
    for v in (range(per - 1, -1, -1) if reverse else range(per)):
        xv = x[:, v:v + 1]
        parts[v] = xv if carry is None else xv + carry
        edge = xv[:, :, 0:1] if reverse else xv[:, :, sl - 1:sl]
        carry = edge if carry is None else carry + edge
    return jnp.concatenate(parts, axis=1).reshape(r, width)


def _group_masks(reverse):
    n = SCAN_GROUP
    ri = lax.broadcasted_iota(jnp.int32, (n, n), 0)
    ci = lax.broadcasted_iota(jnp.int32, (n, n), 1)
    rc, cc = ri // LA_CHUNK, ci // LA_CHUNK
    if reverse:
        return (rc == cc) & (ci >= ri), (rc + 1 == cc) & (rc % 2 == 0), (rc < 2) & (cc >= 2)
    return (rc == cc) & (ci <= ri), (rc == cc + 1) & (rc % 2 == 1), (rc >= 2) & (cc < 2)


def _scale_chunks(x, factors):
    parts = []
    for c, f in enumerate(factors):
        xc = x[c * LA_CHUNK:(c + 1) * LA_CHUNK]
        parts.append(xc if f is None else xc * f)
    return jnp.concatenate(parts, axis=0)


def _scan_prep(q, k, g, reverse):
    bc = _chunk_cumsum(g, reverse)
    e = jnp.exp(bc)
    return bc, q * e, k / e


def _scan_group(gi, prep, v_bf, states, masks, lane_masks, reverse):
    bc, q_dec, k_inv = prep
    nsub = len(states)
    nch = SCAN_GROUP // LA_CHUNK
    assert nch == 4
    m_diag, m_adj, m_far = masks
    pos = (lambda c: nch - 1 - c) if reverse else (lambda c: c)
    edge = 0 if reverse else LA_CHUNK - 1
    r0 = gi * SCAN_GROUP
    rs = slice(r0, r0 + SCAN_GROUP)
    tot = [bc[r0 + c * LA_CHUNK + edge:r0 + c * LA_CHUNK + edge + 1] for c in range(nch)]
    ts = [tot[pos(s)] for s in range(nch)]
    e_ts = [jnp.exp(t) for t in ts]
    e_pre = [None, e_ts[0], jnp.exp(ts[0] + ts[1]), jnp.exp(ts[0] + ts[1] + ts[2])]
    e_post = [jnp.exp(ts[1] + ts[2] + ts[3]), jnp.exp(ts[2] + ts[3]), e_ts[3], None]
    e_total = jnp.exp(ts[0] + ts[1] + ts[2] + ts[3])
    ki = k_inv[rs]
    ke = _scale_chunks(ki, [e_ts[pos(c)] for c in range(nch)])
    kh = _scale_chunks(ke, [e_post[pos(c)] for c in range(nch)])
    kb = _scale_chunks(ke, [e_ts[1] if pos(c) == 0 else None for c in range(nch)])
    kcat = jnp.concatenate([ki.astype(BF16), ke.astype(BF16)], axis=0)
    kb, kh = kb.astype(BF16), kh.astype(BF16)
    outs, new_states = [], []
    for j in range(nsub):
        qd = q_dec[rs]
        if lane_masks is not None:
            qd = jnp.where(lane_masks[j], qd, 0.0)
        qt = _scale_chunks(qd, [e_pre[pos(c)] for c in range(nch)]).astype(BF16)
        qb = _scale_chunks(qd, [e_ts[2] if pos(c) == 3 else None for c in range(nch)]).astype(BF16)
        qd = qd.astype(BF16)
        v_g = v_bf[rs, j * LANES:(j + 1) * LANES]
        s12 = lax.dot_general(qd, kcat, NT, preferred_element_type=F32)
        s3 = lax.dot_general(qb, kb, NT, preferred_element_type=F32)
        o_state = lax.dot_general(qt, states[j].astype(BF16), NT, preferred_element_type=F32)
        upd = lax.dot_general(v_g, kh, TN, preferred_element_type=F32)
        sc = jnp.where(m_diag, s12[:, :SCAN_GROUP],
                       jnp.where(m_adj, s12[:, SCAN_GROUP:], jnp.where(m_far, s3, 0.0)))
        outs.append(o_state + jnp.dot(sc.astype(BF16), v_g, preferred_element_type=F32))
        new_states.append(states[j] * e_total + upd)
    return outs, tuple(new_states)


def _scan_kernel(*refs, mode, n_ctx_blocks):
    if mode == "hgrn":
        (q_ref, gf_ref, gb_ref, v_ref, og_ref, gn_ref, o_ref, of_ref, ob_ref) = refs
        nsub = 1
    else:
        (q_ref, k_ref, gf_ref, gb_ref, v_ref, og_ref, gn_ref, o_ref, of_ref, ob_ref) = refs
        nsub = 2
    t = q_ref.shape[1]
    blk = SCAN_BLOCK
    nblk = t // blk
    masks_f, masks_b = _group_masks(False), _group_masks(True)
    if nsub == 2:
        lane = lax.broadcasted_iota(jnp.int32, (SCAN_GROUP, LANES), 1)
        lane_masks = [lane < B_KEY, lane >= B_KEY]
    else:
        lane_masks = None

    def features(rows, d):
        g = (gf_ref, gb_ref)[d][0, rows, :]
        if mode == "hgrn":
            return q_ref[0, rows, :], 1.0 - jnp.exp(g), g
        return q_ref[0, rows, :], k_ref[0, rows, :], g

    ngroup = blk // SCAN_GROUP

    def prep(i):
        jb = n_ctx_blocks - 1 - i if i < n_ctx_blocks else nblk - 1 - (i - n_ctx_blocks)
        rows_f = slice(i * blk, (i + 1) * blk)
        rows_b = slice(jb * blk, (jb + 1) * blk)
        return (jb, _scan_prep(*features(rows_f, 0), False), _scan_prep(*features(rows_b, 1), True),
                v_ref[0, rows_f, :].astype(BF16), v_ref[0, rows_b, :].astype(BF16))

    def finish(blk_i):
        rows = slice(blk_i * blk, (blk_i + 1) * blk)
        for j in range(nsub):
            cols = slice(j * LANES, (j + 1) * LANES)
            o = of_ref[rows, cols] + ob_ref[rows, cols]
            y = _rms(o, gn_ref[...]) * _silu(og_ref[0, rows, cols])
            o_ref[0, rows, cols] = y.astype(o_ref.dtype)

    st_f = st_b = tuple(jnp.zeros((LANES, LANES), F32) for _ in range(nsub))
    seen = {}
    nxt = prep(0)
    for i in range(nblk):
        jb, prep_f, prep_b, v_f, v_b = nxt
        if i + 1 < nblk:
            nxt = prep(i + 1)
        for s in range(ngroup):
            gf, gb = s, ngroup - 1 - s
            o, st_f = _scan_group(gf, prep_f, v_f, st_f, masks_f, lane_masks, False)
            r0 = i * blk + gf * SCAN_GROUP
            for j in range(nsub):
                of_ref[r0:r0 + SCAN_GROUP, j * LANES:(j + 1) * LANES] = o[j]
            o, st_b = _scan_group(gb, prep_b, v_b, st_b, masks_b, lane_masks, True)
            r0 = jb * blk + gb * SCAN_GROUP
            for j in range(nsub):
                ob_ref[r0:r0 + SCAN_GROUP, j * LANES:(j + 1) * LANES] = o[j]
        for done in (i, jb):
            seen[done] = seen.get(done, 0) + 1
            if seen[done] == 2:
                finish(done)


def _scan_hgrn(p, gn, n_ctx_blocks):
    bsz, t, _ = p.shape
    col = lambda base: pl.BlockSpec((1, t, LANES), lambda b, h, base=base: (b, 0, base + h))
    return pl.pallas_call(
        functools.partial(_scan_kernel, mode="hgrn", n_ctx_blocks=n_ctx_blocks),
        grid=(bsz, A_HEADS),
        in_specs=[col(0), col(4), col(8), col(18), col(22), _const_spec(gn.shape)],
        out_specs=pl.BlockSpec((1, t, LANES), lambda b, h: (b, 0, h)),
        out_shape=jax.ShapeDtypeStruct((bsz, t, A_HEADS * A_VAL), BF16),
        scratch_shapes=[pltpu.VMEM((t, LANES), F32), pltpu.VMEM((t, LANES), F32)],
        compiler_params=pltpu.CompilerParams(vmem_limit_bytes=VMEM_LIMIT),
        name="scan_hgrn",
    )(p, p, p, p, p, gn)


def _scan_gla(p, gn, n_ctx_blocks):
    bsz, t, _ = p.shape
    col = lambda base: pl.BlockSpec((1, t, LANES), lambda b, h, base=base: (b, 0, base + h))
    wide = lambda base: pl.BlockSpec((1, t, 2 * LANES), lambda b, h, base=base: (b, 0, base + h))
    return pl.pallas_call(
        functools.partial(_scan_kernel, mode="gla", n_ctx_blocks=n_ctx_blocks),
        grid=(bsz, B_HEADS // 2),
        in_specs=[col(12), col(26), col(14), col(16), wide(14), wide(16),
                  _const_spec(gn.shape)],
        out_specs=pl.BlockSpec((1, t, 2 * LANES), lambda b, h: (b, 0, h)),
        out_shape=jax.ShapeDtypeStruct((bsz, t, B_HEADS * B_VAL), BF16),
        scratch_shapes=[pltpu.VMEM((t, 2 * LANES), F32), pltpu.VMEM((t, 2 * LANES), F32)],
        compiler_params=pltpu.CompilerParams(vmem_limit_bytes=VMEM_LIMIT),
        name="scan_gla",
    )(p, p, p, p, p, p, gn)


def _post_kernel(*refs, stream):
    nsub = stream.nsub
    y_refs = refs[3 * nsub:5 * nsub]
    ng_ref, wo_ref, win_ref, wout_ref, o_ref = refs[5 * nsub:]
    hid = wout_ref.shape[0]
    tm = TOKEN_TILE

    def mix(s, _):
        xc_ref, xl_ref, mod_ref = refs[3 * s:3 * s + 3]
        y0_ref, y1_ref = y_refs[2 * s:2 * s + 2]
        m = mod_ref[0, 0]
        half = y0_ref.shape[2]
        t = jnp.dot(y0_ref[0], wo_ref[:half, :], preferred_element_type=F32)
        t = t + jnp.dot(y1_ref[0], wo_ref[half:, :], preferred_element_type=F32)
        x1 = stream.tile(xc_ref, xl_ref, s) + m[2:3] * _rms(t, ng_ref[1:2])
        return x1, _modulate(x1, ng_ref[2:3], m[3:4], m[4:5]).astype(BF16)

    def ffn_in(s, carry):
        x1, u = carry
        h = jnp.dot(u, win_ref[...], preferred_element_type=F32)
        return x1, (_silu(h[:, :hid]) * h[:, hid:]).astype(BF16)

    def ffn_out(s, carry):
        x1, act = carry
        m = refs[3 * s + 2][0, 0]
        h2 = jnp.dot(act, wout_ref[...], preferred_element_type=F32)
        o_ref[0, s * tm:(s + 1) * tm, :] = x1 + m[5:6] * _rms(h2, ng_ref[3:4])

    _staggered(nsub, [mix, ffn_in, ffn_out])


def _layer_spec(stacked, layer):
    nd = stacked.ndim - 1
    return pl.BlockSpec((None,) + stacked.shape[1:], lambda *_: (layer,) + (0,) * nd, pipeline_mode=pl.Buffered(1))


def _post(y0, y1, y_col0, y_col1, stream, n_tiles, mod, ng, wo, win_all, wout_all, layer):
    bsz, d = mod.shape[0], mod.shape[-1]
    tm = TOKEN_TILE
    half = d // 2
    nsub = stream.nsub
    assert n_tiles % nsub == 0
    specs, args = stream.operands(mod)
    for s in range(nsub):
        specs += [pl.BlockSpec((1, tm, half), lambda b, i, s=s: (b, i * nsub + s, y_col0)),
                  pl.BlockSpec((1, tm, half), lambda b, i, s=s: (b, i * nsub + s, y_col1))]
        args += [y0, y1]
    consts = [ng, wo, win_all, wout_all]
    return pl.pallas_call(
        functools.partial(_post_kernel, stream=stream),
        grid=(bsz, n_tiles // nsub),
        in_specs=specs + [_const_spec(ng.shape), _const_spec(wo.shape),
                          _layer_spec(win_all, layer), _layer_spec(wout_all, layer)],
        out_specs=pl.BlockSpec((1, nsub * tm, d), lambda b, i: (b, i, 0)),
        out_shape=jax.ShapeDtypeStruct((bsz, n_tiles * tm, d), F32),
        compiler_params=pltpu.CompilerParams(vmem_limit_bytes=VMEM_LIMIT),
        name="post",
    )(*args, *consts)


def _rope(x, cos, sin_signed, first_half):
    partner = jnp.where(first_half, pltpu.roll(x, LANES - 32, 1), pltpu.roll(x, 32, 1))
    return x * cos + partner * sin_signed


def _qkv_kernel(*refs, stream):
    nsub = stream.nsub
    ng_ref, w_ref, cos_ref, sin_ref, q_ref, k_ref, vt_ref, p_ref = refs[3 * nsub:]
    tm = TOKEN_TILE
    nq, nk = q_ref.shape[2], k_ref.shape[2]
    lane = lax.broadcasted_iota(jnp.int32, (tm, LANES), 1)
    first_half = (lane % C_HEAD_DIM) < (C_HEAD_DIM // 2)
    ones_rows = (lane >= C_HEAD_DIM) & (lane < V_ROWS)

    def modulated(s, _):
        xc_ref, xl_ref, mod_ref = refs[3 * s:3 * s + 3]
        m = mod_ref[0, 0]
        return _modulate(stream.tile(xc_ref, xl_ref, s), ng_ref[0:1], m[0:1], m[1:2]).astype(BF16)

    def project(s, u):
        p_ref[s * tm:(s + 1) * tm, :] = jnp.dot(u, w_ref[...], preferred_element_type=F32)

    def rotate(s, _):
        rows = slice(s * tm, (s + 1) * tm)
        cos, sin = cos_ref[rows, :], sin_ref[rows, :]
        for j in range(nq // LANES):
            cols = slice(j * LANES, (j + 1) * LANES)
            q_ref[0, rows, cols] = (_rope(p_ref[rows, cols], cos, sin, first_half)
                                    * (C_HEAD_DIM ** -0.5)).astype(BF16)
        for j in range(nk // LANES):
            cols = slice(nq + j * LANES, nq + (j + 1) * LANES)
            k_ref[0, rows, j * LANES:(j + 1) * LANES] = _rope(p_ref[rows, cols], cos, sin, first_half).astype(BF16)
        for j in range(C_KV_HEADS):
            cols = slice(nq + nk + j * LANES, nq + nk + (j + 1) * LANES)
            vt_ref[0, j * LANES:(j + 1) * LANES, rows] = jnp.where(ones_rows, 1.0, p_ref[rows, cols]).T.astype(BF16)

    _staggered(nsub, [modulated, project, rotate])


def _qkv(stream, n_tiles, mod, ng, w, cos, sin):
    bsz = mod.shape[0]
    rows = stream.nsub * TOKEN_TILE
    assert n_tiles % stream.nsub == 0
    t = n_tiles * TOKEN_TILE
    nq = C_HEADS * C_HEAD_DIM
    nk = C_KV_HEADS * LANES
    specs, args = stream.operands(mod)
    return pl.pallas_call(
        functools.partial(_qkv_kernel, stream=stream),
        grid=(bsz, n_tiles // stream.nsub),
        in_specs=specs + [
            _const_spec(ng.shape), _const_spec(w.shape),
            pl.BlockSpec((rows, LANES), lambda b, i: (i, 0)),
            pl.BlockSpec((rows, LANES), lambda b, i: (i, 0)),
        ],
        out_specs=[
            pl.BlockSpec((1, rows, nq), lambda b, i: (b, i, 0)),
            pl.BlockSpec((1, rows, nk), lambda b, i: (b, i, 0)),
            pl.BlockSpec((1, nk, rows), lambda b, i: (b, 0, i)),
        ],
        out_shape=[jax.ShapeDtypeStruct((bsz, t, nq), BF16),
                   jax.ShapeDtypeStruct((bsz, t, nk), BF16),
                   jax.ShapeDtypeStruct((bsz, nk, t), BF16)],
        scratch_shapes=[pltpu.VMEM((rows, w.shape[1]), F32)],
        compiler_params=pltpu.CompilerParams(vmem_limit_bytes=VMEM_LIMIT),
        name="qkv",
    )(*args, ng, w, cos, sin)


def _attn_kernel(sink_ref, bias_ref, *refs, n_ctx):
    blk = ATT_BLOCK
    band = 3 * blk
    nsub = ATT_QUERY_BLOCKS
    q_refs = refs[:nsub]
    k_ref, vt_ref, o_ref = refs[nsub:]
    t = k_ref.shape[1]
    group = C_HEADS // C_KV_HEADS
    lane = lax.broadcasted_iota(jnp.int32, (blk, LANES), 1)
    low = lane < C_HEAD_DIM
    zero = jnp.zeros((blk, LANES), BF16)

    def band_start(sb):
        n = pl.program_id(1) * nsub + sb
        return pl.multiple_of(jnp.minimum(n_ctx + (n - 1) * blk, t - band), blk)

    def scores(sb, kg):
        cols = slice(kg * LANES, (kg + 1) * LANES)
        r_band = band_start(sb)
        parts, sinks = [], []
        for hh in range(group):
            h = kg * group + hh
            pair = q_refs[sb][0, :, (h // 2) * LANES:(h // 2 + 1) * LANES]
            parts.append(jnp.where(low if h % 2 == 0 else ~low, pair, zero))
            sinks.append(jnp.full((1, blk), sink_ref[h], F32))
        q4 = jnp.concatenate(parts, axis=0)
        sink = jnp.concatenate(sinks, axis=1)
        kk = jnp.concatenate([k_ref[0, pl.ds(r_band, band), cols], k_ref[0, 0:n_ctx, cols]], axis=0)
        s = lax.dot_general(kk, q4, NT, preferred_element_type=F32)
        bias = jnp.concatenate([bias_ref[sb]] * group, axis=1)
        s = jnp.concatenate([s[:band] + bias, s[band:]], axis=0)
        mx = jnp.maximum(jnp.max(s, axis=0, keepdims=True), sink)
        return s, mx, sink

    def probs(s, mx, sink):
        return jnp.exp((s - mx).astype(BF16)), jnp.exp(sink - mx)

    def finish(sb, kg, p, p_sink):
        r_band = band_start(sb)
        rows = slice(kg * LANES, kg * LANES + V_ROWS)
        vt = jnp.concatenate([vt_ref[0, rows, pl.ds(r_band, band)], vt_ref[0, rows, 0:n_ctx]], axis=1)
        o3 = jnp.dot(vt, p, preferred_element_type=F32)
        o = o3[:C_HEAD_DIM] * (1.0 / (o3[C_HEAD_DIM:C_HEAD_DIM + 1] + p_sink))
        for pp in range(group // 2):
            y = jnp.concatenate([o[:, (2 * pp) * blk:(2 * pp + 1) * blk],
                                 o[:, (2 * pp + 1) * blk:(2 * pp + 2) * blk]], axis=0)
            c0 = (kg * (group // 2) + pp) * LANES
            o_ref[0, sb * blk:(sb + 1) * blk, c0:c0 + LANES] = y.T.astype(o_ref.dtype)

    units = [(sb, kg) for sb in range(nsub) for kg in range(C_KV_HEADS)]
    nu = len(units)
    s_q = {0: scores(*units[0]), 1: scores(*units[1])}
    p_q = {0: probs(*s_q.pop(0))}
    for i in range(nu):
        if i + 2 < nu:
            s_q[i + 2] = scores(*units[i + 2])
        if i + 1 < nu:
            p_q[i + 1] = probs(*s_q.pop(i + 1))
        finish(*units[i], *p_q.pop(i))


def _attn_bias(n_blocks, n_ctx, t):
    blk, band = ATT_BLOCK, 3 * ATT_BLOCK
    n = np.arange(n_blocks)[:, None, None]
    r_band = np.minimum(n_ctx + (n - 1) * blk, t - band)
    kp = np.arange(band)[None, :, None] + (r_band - n_ctx)
    qi = np.arange(blk)[None, None, :] + n * blk
    ok = (np.abs(kp - qi) <= ATT_WINDOW) & (kp >= 0)
    return jnp.asarray(np.where(ok, 0.0, -1e30).astype(np.float32))


def _attention(q, k2, vt, sink, n_ctx):
    bsz, t, nq = q.shape
    nk = k2.shape[2]
    rows = ATT_QUERY_BLOCKS * ATT_BLOCK
    n_blocks = (t - n_ctx) // ATT_BLOCK
    assert n_ctx % ATT_BLOCK == 0 and (t - n_ctx) % rows == 0 and n_ctx >= ATT_BLOCK and n_blocks >= 2
    ctx_blocks = n_ctx // ATT_BLOCK
    nqb = ATT_QUERY_BLOCKS
    bias = _attn_bias(n_blocks, n_ctx, t)
    return pl.pallas_call(
        functools.partial(_attn_kernel, n_ctx=n_ctx),
        grid=(bsz, n_blocks // ATT_QUERY_BLOCKS),
        in_specs=[
            pl.BlockSpec(memory_space=pltpu.SMEM),
            pl.BlockSpec((ATT_QUERY_BLOCKS,) + bias.shape[1:], lambda b, n: (n, 0, 0)),
        ] + [
            pl.BlockSpec((1, ATT_BLOCK, nq), lambda b, n, s=s: (b, n * nqb + s + ctx_blocks, 0)) for s in range(nqb)
        ] + [
            pl.BlockSpec((1, t, nk), lambda b, n: (b, 0, 0)),
            pl.BlockSpec((1, nk, t), lambda b, n: (b, 0, 0)),
        ],
        out_specs=pl.BlockSpec((1, rows, nq), lambda b, n: (b, n, 0)),
        out_shape=jax.ShapeDtypeStruct((bsz, t - n_ctx, nq), BF16),
        compiler_params=pltpu.CompilerParams(vmem_limit_bytes=VMEM_LIMIT),
        name="attention",
    )(sink, bias, *([q] * nqb), k2, vt)


def _rec_weights(w_in, w_g2, b_g2):
    widths = (A_HEADS * A_KEY,) * 3 + (A_HEADS * A_VAL,) * 2 + (B_HEADS * B_KEY,) * 2 + (
        B_HEADS * B_VAL, B_GATE_RANK, B_GATE_RANK, B_HEADS * B_VAL)
    off = np.concatenate([[0], np.cumsum(widths)])
    seg = lambda i: w_in[:, off[i]:off[i + 1]]
    d = w_in.shape[0]
    pad = jnp.zeros((d, LANES - 2 * B_GATE_RANK), w_in.dtype)
    w = jnp.concatenate([seg(i) for i in (0, 1, 2, 5, 8, 9)] + [pad] + [seg(i) for i in (3, 4, 6, 7, 10)],
                        axis=1).astype(BF16)
    nk = B_HEADS * B_KEY
    w2 = jnp.zeros((LANES, 2 * nk), F32)
    w2 = w2.at[:B_GATE_RANK, :nk].set(w_g2[0]).at[B_GATE_RANK:2 * B_GATE_RANK, nk:].set(w_g2[1])
    return w, w2.astype(BF16), b_g2.reshape(1, 2 * nk)


def _att_weights(w_qkv):
    d = w_qkv.shape[0]
    nq, nkv = C_HEADS * C_HEAD_DIM, C_KV_HEADS * C_HEAD_DIM
    perm = np.concatenate([np.arange(0, C_HEAD_DIM, 2), np.arange(1, C_HEAD_DIM, 2)])
    wq = w_qkv[:, :nq].reshape(d, C_HEADS, C_HEAD_DIM)[:, :, perm].reshape(d, nq)
    wk = w_qkv[:, nq:nq + nkv].reshape(d, C_KV_HEADS, C_HEAD_DIM)[:, :, perm]
    wv = w_qkv[:, nq + nkv:].reshape(d, C_KV_HEADS, C_HEAD_DIM)
    wk = jnp.concatenate([wk, wk], axis=2).reshape(d, 2 * nkv)
    wv = jnp.concatenate([wv, jnp.zeros_like(wv)], axis=2).reshape(d, 2 * nkv)
    return jnp.concatenate([wq, wk, wv], axis=1).astype(BF16)


def _rope_tables(seq, n_ctx):
    n_rows = seq // GRID_W
    row = jnp.repeat(jnp.arange(n_rows), GRID_W).astype(F32)
    col = jnp.tile(jnp.arange(GRID_W), n_rows).astype(F32)
    half = C_HEAD_DIM // 2
    inv = ROPE_BASE ** (-jnp.arange(0, half, 2, dtype=F32) / half)
    ang = jnp.concatenate([row[:, None] * inv, col[:, None] * inv], axis=-1)
    cos, sin = jnp.cos(ang), jnp.sin(ang)
    cos = jnp.concatenate([jnp.ones((n_ctx, half), F32), cos], axis=0)
    sin = jnp.concatenate([jnp.zeros((n_ctx, half), F32), sin], axis=0)
    reps = LANES // C_HEAD_DIM
    return (jnp.tile(jnp.concatenate([cos, cos], axis=1), (1, reps)),
            jnp.tile(jnp.concatenate([-sin, sin], axis=1), (1, reps)))


def kernel(x, c, ctx, c_ctx, ada_w, ada_b, norm_g, rec_w_in, rec_w_out, rec_lb_logits, rec_w_g2, rec_b_g2,
           rec_gn_a, rec_gn_b, att_w_qkv, att_w_o, att_sink, ffn_w_in, ffn_w_out):
    bsz, seq, d = x.shape
    n_ctx = ctx.shape[1]
    depth = ada_w.shape[0]
    tm = TOKEN_TILE
    assert n_ctx % tm == 0 and seq % tm == 0 and n_ctx % SCAN_BLOCK == 0 and seq % SCAN_BLOCK == 0
    n_ctx_tiles = n_ctx // tm
    n_lat_tiles = seq // tm

    rows = -(-(bsz + 1) // 8) * 8
    cond = jnp.concatenate([c, c_ctx[None], jnp.zeros((rows - bsz - 1, d), F32)], axis=0)
    mod_all = _ada(cond, ada_w, ada_b)
    mods = []
    for l in range(depth):
        lat = mod_all[l, :bsz].reshape(bsz, 1, 6, d)
        cx = jnp.broadcast_to(mod_all[l, bsz].reshape(1, 1, 6, d), (bsz, 1, 6, d))
        mods.append(jnp.concatenate([cx, lat], axis=1))

    cos, sin = _rope_tables(seq, n_ctx)
    n_tiles = n_ctx_tiles + n_lat_tiles
    src = (ctx, x, 0)
    win_all, wout_all = ffn_w_in.astype(BF16), ffn_w_out.astype(BF16)

    for l in range(depth):
        need_ctx = l < depth - 1
        j = l // 2
        ng = norm_g[l]
        stream = _Stream(src[0], src[1], n_ctx_tiles, 0, src[2], _sub_tiles(n_tiles))
        if l % 2 == 0:
            w, w2, b2 = _rec_weights(rec_w_in[j], rec_w_g2[j], rec_b_g2[j])
            p = _rec_in(stream, n_tiles, mods[l], ng, w, w2, b2, rec_lb_logits, j)
            ya = _scan_hgrn(p, rec_gn_a[j].reshape(1, -1), n_ctx // SCAN_BLOCK)
            yb = _scan_gla(p, rec_gn_b[j].reshape(1, -1), n_ctx // SCAN_BLOCK)
            y0, y1, c0, c1, wo = ya, yb, 0, 0, rec_w_out[j]
            y_has_ctx = True
        else:
            q, k2, vt = _qkv(stream, n_tiles, mods[l], ng, _att_weights(att_w_qkv[j]), cos, sin)
            y = _attention(q, k2, vt, att_sink[j], n_ctx)
            y0, y1, c0, c1, wo = y, y, 0, 1, att_w_o[j]
            y_has_ctx = False
        wo = wo.astype(BF16)
        if need_ctx:
            assert y_has_ctx
            xcat = _post(y0, y1, c0, c1, stream, n_tiles, mods[l], ng, wo, win_all, wout_all, l)
            src = (xcat, xcat, n_ctx_tiles)
        else:
            if y_has_ctx:
                y0 = y0[:, n_ctx:]
                y1 = y1[:, n_ctx:]
            lat_only = _Stream(src[0], src[1], n_ctx_tiles, n_ctx_tiles, src[2], _sub_tiles(n_lat_tiles))
            x_lat = _post(y0, y1, c0, c1, lat_only, n_lat_tiles, mods[l], ng, wo, win_all, wout_all, l)
    return x_lat
```

```python
import functools

import numpy as np
import jax
import jax.numpy as jnp
from jax import lax
from jax.experimental import pallas as pl
from jax.experimental.pallas import tpu as pltpu

F32 = jnp.float32
BF16 = jnp.bfloat16
HIGHEST = lax.Precision.HIGHEST

EPS = 1e-6
LANES = 128
TOKEN_TILE = 256
SCAN_BLOCK = 256
LA_CHUNK = 32
SCAN_GROUP = 4 * LA_CHUNK
ATT_BLOCK = 128
ATT_WINDOW = 128
ATT_QUERY_BLOCKS = 4
V_ROWS = 80
GRID_W = 64
ROPE_BASE = 10000.0
GLA_GATE_NORM = 16.0
A_HEADS, A_KEY, A_VAL = 4, 128, 128
B_HEADS, B_KEY, B_VAL = 4, 64, 128
B_GATE_RANK = 16
C_HEADS, C_KV_HEADS, C_HEAD_DIM = 16, 4, 64
VMEM_LIMIT = 56 * 1024 * 1024

NT = (((1,), (1,)), ((), ()))
TN = (((0,), (0,)), ((), ()))


def _silu(x):
    h = 0.5 * x
    return h * (jnp.tanh(h) + 1.0)


def _log_sigmoid(x):
    return jnp.minimum(x, 0.0) - jnp.log(1.0 + jnp.exp(-jnp.abs(x)))


def _rms(x, g):
    ms = jnp.mean(x * x, axis=-1, keepdims=True)
    return (x * lax.rsqrt(ms + EPS)) * g


def _modulate(x, g, shift, scale):
    return _rms(x, g) * (1.0 + scale) + shift


def _const_spec(shape):
    nd = len(shape)
    return pl.BlockSpec(shape, lambda *_: (0,) * nd, pipeline_mode=pl.Buffered(1))


class _Stream:
    def __init__(self, ctx_arr, lat_arr, n_ctx_tiles, tile0=0, lat_off=0, nsub=1):
        self.arrays = (ctx_arr, lat_arr)
        self.n_ctx_tiles, self.tile0, self.lat_off, self.nsub = n_ctx_tiles, tile0, lat_off, nsub
        self.d = ctx_arr.shape[-1]

    def specs_for(self, s):
        nct, off, lat_off, nsub = self.n_ctx_tiles, self.tile0 + s, self.lat_off, self.nsub
        return [
            pl.BlockSpec((1, TOKEN_TILE, self.d), lambda b, i: (b, jnp.minimum(i * nsub + off, nct - 1), 0)),
            pl.BlockSpec((1, TOKEN_TILE, self.d),
                         lambda b, i: (b, jnp.maximum(i * nsub + off - nct, 0) + lat_off, 0)),
        ]

    def mod_spec_for(self, s):
        nct, off, nsub = self.n_ctx_tiles, self.tile0 + s, self.nsub
        return pl.BlockSpec((1, 1, 6, self.d), lambda b, i: (b, (i * nsub + off >= nct).astype(jnp.int32), 0, 0))

    def tile(self, ctx_ref, lat_ref, s):
        if self.tile0 >= self.n_ctx_tiles:
            return lat_ref[0]
        g = pl.program_id(1) * self.nsub + s + self.tile0
        return jnp.where(g < self.n_ctx_tiles, ctx_ref[0], lat_ref[0])

    def operands(self, mod):
        specs, args = [], []
        for s in range(self.nsub):
            specs += self.specs_for(s) + [self.mod_spec_for(s)]
            args += [*self.arrays, mod]
        return specs, args


def _sub_tiles(n_tiles):
    return next(n for n in (4, 3, 2, 1) if n_tiles % n == 0)


def _staggered(nsub, stages):
    live = {}
    for step in range(nsub + len(stages) - 1):
        for k in range(len(stages)):
            s = step - k
            if 0 <= s < nsub:
                live[s] = stages[k](s, live.get(s))


def _ada_kernel(c_ref, w_ref, b_ref, o_ref):
    s = _silu(c_ref[...])
    o_ref[0] = jnp.dot(s, w_ref[0], preferred_element_type=F32, precision=HIGHEST) + b_ref[0]


def _ada(cond, ada_w, ada_b):
    depth, d, n = ada_w.shape
    rows = cond.shape[0]
    tn = n // 4
    return pl.pallas_call(
        _ada_kernel,
        grid=(depth, n // tn),
        in_specs=[
            pl.BlockSpec((rows, d), lambda l, j: (0, 0)),
            pl.BlockSpec((1, d, tn), lambda l, j: (l, 0, j)),
            pl.BlockSpec((1, 1, tn), lambda l, j: (l, 0, j)),
        ],
        out_specs=pl.BlockSpec((1, rows, tn), lambda l, j: (l, 0, j)),
        out_shape=jax.ShapeDtypeStruct((depth, rows, n), F32),
        compiler_params=pltpu.CompilerParams(vmem_limit_bytes=VMEM_LIMIT),
        name="ada",
    )(cond, ada_w, ada_b.reshape(depth, 1, n))


def _recin_kernel(*refs, stream, layer_slot):
    nsub = stream.nsub
    ng_ref, w_ref, w2_ref, b2_ref, lb_ref, o_ref = refs[3 * nsub:]
    tm = TOKEN_TILE
    lg = lb_ref[...]
    e = jnp.exp(lg - jnp.max(lg, axis=0, keepdims=True))
    lb = jnp.sum((e / jnp.sum(e, axis=0, keepdims=True))[:layer_slot + 1], axis=0)
    na, nb = A_HEADS * A_KEY, B_HEADS * B_KEY
    n_act = 3 * na + nb

    def modulated(s, _):
        xc_ref, xl_ref, mod_ref = refs[3 * s:3 * s + 3]
        m = mod_ref[0, 0]
        return _modulate(stream.tile(xc_ref, xl_ref, s), ng_ref[0:1], m[0:1], m[1:2]).astype(BF16)

    def project(s, u):
        rows = slice(s * tm, (s + 1) * tm)

        def put(c0, val):
            for j in range(val.shape[1] // LANES):
                o_ref[c0 // LANES + j, rows, :] = val[:, j * LANES:(j + 1) * LANES]

        p = jnp.dot(u, w_ref[:, :n_act + LANES], preferred_element_type=F32)
        put(0, p[:, :n_act])
        pre = jnp.dot(p[:, n_act:].astype(BF16), w2_ref[...], preferred_element_type=F32)
        put(n_act, pre + b2_ref[...])
        put(n_act + 2 * nb, jnp.dot(u, w_ref[:, n_act + LANES:], preferred_element_type=F32))

    def activate(s, _):
        rows = slice(s * tm, (s + 1) * tm)
        for c0 in range(0, n_act + 2 * nb, LANES):
            v = o_ref[c0 // LANES, rows, :]
            if c0 < na:
                v = _silu(v) * (A_KEY ** -0.5)
            elif c0 < 3 * na:
                d, h0 = divmod(c0 - na, na)
                lbd = lb[d:d + 1, h0:h0 + LANES]
                v = jnp.log(0.5 * (1.0 + lbd) + (0.5 * (1.0 - lbd)) * jnp.tanh(0.5 * v))
            elif c0 < n_act:
                v = v * (B_KEY ** -0.5)
            else:
                v = _log_sigmoid(v) * (1.0 / GLA_GATE_NORM)
            o_ref[c0 // LANES, rows, :] = v

    _staggered(nsub, [modulated, project, activate])


def _rec_in(stream, n_tiles, mod, ng, w, w2, b2, lb_logits, layer_slot):
    bsz = mod.shape[0]
    n_out = w.shape[1] - LANES + w2.shape[1]
    rows = stream.nsub * TOKEN_TILE
    assert n_tiles % stream.nsub == 0
    specs, args = stream.operands(mod)
    consts = [ng, w, w2, b2, lb_logits]
    return pl.pallas_call(
        functools.partial(_recin_kernel, stream=stream, layer_slot=layer_slot),
        grid=(bsz, n_tiles // stream.nsub),
        in_specs=specs + [_const_spec(a.shape) for a in consts],
        out_specs=pl.BlockSpec((None, n_out // LANES, rows, LANES), lambda b, i: (b, 0, i, 0)),
        out_shape=jax.ShapeDtypeStruct((bsz, n_out // LANES, n_tiles * TOKEN_TILE, LANES), F32),
        compiler_params=pltpu.CompilerParams(vmem_limit_bytes=VMEM_LIMIT),
        name="rec_in",
    )(*args, *consts)


def _chunk_cumsum(g, reverse):
    r, width = g.shape
    sl = 8
    x = g.reshape(r // sl, sl, width)
    sub = lax.broadcasted_iota(jnp.int32, x.shape, 1)
    for s in (1, 2, 4):
        if reverse:
            x = x + jnp.where(sub < sl - s, pltpu.roll(x, sl - s, 1), 0.0)
        else:
            x = x + jnp.where(sub >= s, pltpu.roll(x, s, 1), 0.0)
    per = LA_CHUNK // sl
    x = x.reshape(r // LA_CHUNK, per, sl, width)
    parts = [None] * per
    carry = None
    for v in (range(per - 1, -1, -1) if reverse else range(per)):
        xv = x[:, v:v + 1]
        parts[v] = xv if carry is None else xv + carry
        edge = xv[:, :, 0:1] if reverse else xv[:, :, sl - 1:sl]
        carry = edge if carry is None else carry + edge
    return jnp.concatenate(parts, axis=1).reshape(r, width)


def _group_masks(reverse):
    n = SCAN_GROUP
    ri = lax.broadcasted_iota(jnp.int32, (n, n), 0)
    ci = lax.broadcasted_iota(jnp.int32, (n, n), 1)
    rc, cc = ri // LA_CHUNK, ci // LA_CHUNK
    if reverse:
        return (rc == cc) & (ci >= ri), (rc + 1 == cc) & (rc % 2 == 0), (rc < 2) & (cc >= 2)
    return (rc == cc) & (ci <= ri), (rc == cc + 1) & (rc % 2 == 1), (rc >= 2) & (cc < 2)


def _scale_chunks(x, factors):
    parts = []
    for c, f in enumerate(factors):
        xc = x[c * LA_CHUNK:(c + 1) * LA_CHUNK]
        parts.append(xc if f is None else xc * f)
    return jnp.concatenate(parts, axis=0)


def _scan_prep(q, k, g, reverse):
    bc = _chunk_cumsum(g, reverse)
    e = jnp.exp(bc)
    return bc, q * e, k / e


def _scan_group(gi, prep, v_bf, states, masks, lane_masks, reverse):
    bc, q_dec, k_inv = prep
    nsub = len(states)
    nch = SCAN_GROUP // LA_CHUNK
    assert nch == 4
    m_diag, m_adj, m_far = masks
    pos = (lambda c: nch - 1 - c) if reverse else (lambda c: c)
    edge = 0 if reverse else LA_CHUNK - 1
    r0 = gi * SCAN_GROUP
    rs = slice(r0, r0 + SCAN_GROUP)
    tot = [bc[r0 + c * LA_CHUNK + edge:r0 + c * LA_CHUNK + edge + 1] for c in range(nch)]
    ts = [tot[pos(s)] for s in range(nch)]
    e_ts = [jnp.exp(t) for t in ts]
    e_pre = [None, e_ts[0], jnp.exp(ts[0] + ts[1]), jnp.exp(ts[0] + ts[1] + ts[2])]
    e_post = [jnp.exp(ts[1] + ts[2] + ts[3]), jnp.exp(ts[2] + ts[3]), e_ts[3], None]
    e_total = jnp.exp(ts[0] + ts[1] + ts[2] + ts[3])
    ki = k_inv[rs]
    ke = _scale_chunks(ki, [e_ts[pos(c)] for c in range(nch)])
    kh = _scale_chunks(ke, [e_post[pos(c)] for c in range(nch)])
    kb = _scale_chunks(ke, [e_ts[1] if pos(c) == 0 else None for c in range(nch)])
    kcat = jnp.concatenate([ki.astype(BF16), ke.astype(BF16)], axis=0)
    kb, kh = kb.astype(BF16), kh.astype(BF16)
    outs, new_states = [], []
    for j in range(nsub):
        qd = q_dec[rs]
        if lane_masks is not None:
            qd = jnp.where(lane_masks[j], qd, 0.0)
        qt = _scale_chunks(qd, [e_pre[pos(c)] for c in range(nch)]).astype(BF16)
        qb = _scale_chunks(qd, [e_ts[2] if pos(c) == 3 else None for c in range(nch)]).astype(BF16)
        qd = qd.astype(BF16)
        v_g = v_bf[j][rs]
        s12 = lax.dot_general(qd, kcat, NT, preferred_element_type=F32)
        s3 = lax.dot_general(qb, kb, NT, preferred_element_type=F32)
        o_state = lax.dot_general(qt, states[j].astype(BF16), NT, preferred_element_type=F32)
        upd = lax.dot_general(v_g, kh, TN, preferred_element_type=F32)
        sc = jnp.where(m_diag, s12[:, :SCAN_GROUP],
                       jnp.where(m_adj, s12[:, SCAN_GROUP:], jnp.where(m_far, s3, 0.0)))
        outs.append(o_state + jnp.dot(sc.astype(BF16), v_g, preferred_element_type=F32))
        new_states.append(states[j] * e_total + upd)
    return outs, tuple(new_states)


def _scan_kernel(*refs, mode, n_ctx_blocks):
    if mode == "hgrn":
        (q_ref, gf_ref, gb_ref, v_ref, og_ref, gn_ref, o_ref, of_ref, ob_ref) = refs
        nsub = 1
    else:
        (q_ref, k_ref, gf_ref, gb_ref, v_ref, og_ref, gn_ref, o_ref, of_ref, ob_ref) = refs
        nsub = 2
    t = q_ref.shape[0]
    blk = SCAN_BLOCK
    nblk = t // blk
    masks_f, masks_b = _group_masks(False), _group_masks(True)
    if nsub == 2:
        lane = lax.broadcasted_iota(jnp.int32, (SCAN_GROUP, LANES), 1)
        lane_masks = [lane < B_KEY, lane >= B_KEY]
    else:
        lane_masks = None

    def features(rows, d):
        g = (gf_ref, gb_ref)[d][rows, :]
        if mode == "hgrn":
            return q_ref[rows, :], 1.0 - jnp.exp(g), g
        return q_ref[rows, :], k_ref[rows, :], g

    ngroup = blk // SCAN_GROUP

    def prep(i):
        jb = n_ctx_blocks - 1 - i if i < n_ctx_blocks else nblk - 1 - (i - n_ctx_blocks)
        rows_f = slice(i * blk, (i + 1) * blk)
        rows_b = slice(jb * blk, (jb + 1) * blk)
        return (jb, _scan_prep(*features(rows_f, 0), False), _scan_prep(*features(rows_b, 1), True),
                [v_ref[j, rows_f, :].astype(BF16) for j in range(nsub)],
                [v_ref[j, rows_b, :].astype(BF16) for j in range(nsub)])

    def finish(blk_i):
        rows = slice(blk_i * blk, (blk_i + 1) * blk)
        for j in range(nsub):
            cols = slice(j * LANES, (j + 1) * LANES)
            o = of_ref[rows, cols] + ob_ref[rows, cols]
            y = _rms(o, gn_ref[...]) * _silu(og_ref[j, rows, :])
            o_ref[j, rows, :] = y.astype(o_ref.dtype)

    st_f = st_b = tuple(jnp.zeros((LANES, LANES), F32) for _ in range(nsub))
    seen = {}
    nxt = prep(0)
    for i in range(nblk):
        jb, prep_f, prep_b, v_f, v_b = nxt
        if i + 1 < nblk:
            nxt = prep(i + 1)
        for s in range(ngroup):
            gf, gb = s, ngroup - 1 - s
            o, st_f = _scan_group(gf, prep_f, v_f, st_f, masks_f, lane_masks, False)
            r0 = i * blk + gf * SCAN_GROUP
            for j in range(nsub):
                of_ref[r0:r0 + SCAN_GROUP, j * LANES:(j + 1) * LANES] = o[j]
            o, st_b = _scan_group(gb, prep_b, v_b, st_b, masks_b, lane_masks, True)
            r0 = jb * blk + gb * SCAN_GROUP
            for j in range(nsub):
                ob_ref[r0:r0 + SCAN_GROUP, j * LANES:(j + 1) * LANES] = o[j]
        for done in (i, jb):
            seen[done] = seen.get(done, 0) + 1
            if seen[done] == 2:
                finish(done)


def _group_spec(t, first, per_step=None):
    if per_step is None:
        return pl.BlockSpec((None, None, t, LANES), lambda b, h: (b, first + h, 0, 0))
    assert first % per_step == 0
    return pl.BlockSpec((None, per_step, t, LANES), lambda b, h: (b, first // per_step + h, 0, 0))


def _scan_hgrn(p, gn, n_ctx_blocks):
    bsz, _, t, _ = p.shape
    one, sub = (lambda first: _group_spec(t, first)), (lambda first: _group_spec(t, first, 1))
    return pl.pallas_call(
        functools.partial(_scan_kernel, mode="hgrn", n_ctx_blocks=n_ctx_blocks),
        grid=(bsz, A_HEADS),
        in_specs=[one(0), one(4), one(8), sub(18), sub(22), _const_spec(gn.shape)],
        out_specs=sub(0),
        out_shape=jax.ShapeDtypeStruct((bsz, A_HEADS, t, A_VAL), BF16),
        scratch_shapes=[pltpu.VMEM((t, LANES), F32), pltpu.VMEM((t, LANES), F32)],
        compiler_params=pltpu.CompilerParams(vmem_limit_bytes=VMEM_LIMIT),
        name="scan_hgrn",
    )(p, p, p, p, p, gn)


def _scan_gla(p, gn, n_ctx_blocks):
    bsz, _, t, _ = p.shape
    one, two = (lambda first: _group_spec(t, first)), (lambda first: _group_spec(t, first, 2))
    return pl.pallas_call(
        functools.partial(_scan_kernel, mode="gla", n_ctx_blocks=n_ctx_blocks),
        grid=(bsz, B_HEADS // 2),
        in_specs=[one(12), one(26), one(14), one(16), two(28), two(32), _const_spec(gn.shape)],
        out_specs=two(0),
        out_shape=jax.ShapeDtypeStruct((bsz, B_HEADS, t, B_VAL), BF16),
        scratch_shapes=[pltpu.VMEM((t, 2 * LANES), F32), pltpu.VMEM((t, 2 * LANES), F32)],
        compiler_params=pltpu.CompilerParams(vmem_limit_bytes=VMEM_LIMIT),
        name="scan_gla",
    )(p, p, p, p, p, p, gn)


def _post_kernel(*refs, stream):
    nsub = stream.nsub
    y_refs = refs[3 * nsub:5 * nsub]
    ng_ref, wo_ref, win_ref, wout_ref, o_ref = refs[5 * nsub:]
    hid = wout_ref.shape[0]
    tm = TOKEN_TILE

    def mix(s, _):
        xc_ref, xl_ref, mod_ref = refs[3 * s:3 * s + 3]
        y0_ref, y1_ref = y_refs[2 * s:2 * s + 2]
        m = mod_ref[0, 0]
        half = wo_ref.shape[0] // 2

        def rows_of(y_ref):
            if len(y_ref.shape) == 2:
                return y_ref[...]
            return jnp.concatenate([y_ref[h] for h in range(y_ref.shape[0])], axis=1)

        t = jnp.dot(rows_of(y0_ref), wo_ref[:half, :], preferred_element_type=F32)
        t = t + jnp.dot(rows_of(y1_ref), wo_ref[half:, :], preferred_element_type=F32)
        x1 = stream.tile(xc_ref, xl_ref, s) + m[2:3] * _rms(t, ng_ref[1:2])
        return x1, _modulate(x1, ng_ref[2:3], m[3:4], m[4:5]).astype(BF16)

    def ffn_in(s, carry):
        x1, u = carry
        h = jnp.dot(u, win_ref[...], preferred_element_type=F32)
        return x1, (_silu(h[:, :hid]) * h[:, hid:]).astype(BF16)

    def ffn_out(s, carry):
        x1, act = carry
        m = refs[3 * s + 2][0, 0]
        h2 = jnp.dot(act, wout_ref[...], preferred_element_type=F32)
        o_ref[0, s * tm:(s + 1) * tm, :] = x1 + m[5:6] * _rms(h2, ng_ref[3:4])

    _staggered(nsub, [mix, ffn_in, ffn_out])


def _layer_spec(stacked, layer):
    nd = stacked.ndim - 1
    return pl.BlockSpec((None,) + stacked.shape[1:], lambda *_: (layer,) + (0,) * nd, pipeline_mode=pl.Buffered(1))


def _post(y0, y1, y_col0, y_col1, stream, n_tiles, mod, ng, wo, win_all, wout_all, layer):
    bsz, d = mod.shape[0], mod.shape[-1]
    tm = TOKEN_TILE
    half = d // 2
    nsub = stream.nsub
    assert n_tiles % nsub == 0
    specs, args = stream.operands(mod)
    def y_spec(y, col, s):
        if y.ndim == 4:
            return pl.BlockSpec((None, y.shape[1], tm, LANES), lambda b, i: (b, 0, i * nsub + s, 0))
        return pl.BlockSpec((None, tm, half), lambda b, i: (b, i * nsub + s, col))

    for s in range(nsub):
        specs += [y_spec(y0, y_col0, s), y_spec(y1, y_col1, s)]
        args += [y0, y1]
    consts = [ng, wo, win_all, wout_all]
    return pl.pallas_call(
        functools.partial(_post_kernel, stream=stream),
        grid=(bsz, n_tiles // nsub),
        in_specs=specs + [_const_spec(ng.shape), _const_spec(wo.shape),
                          _layer_spec(win_all, layer), _layer_spec(wout_all, layer)],
        out_specs=pl.BlockSpec((1, nsub * tm, d), lambda b, i: (b, i, 0)),
        out_shape=jax.ShapeDtypeStruct((bsz, n_tiles * tm, d), F32),
        compiler_params=pltpu.CompilerParams(vmem_limit_bytes=VMEM_LIMIT),
        name="post",
    )(*args, *consts)


def _rope(x, cos, sin_signed, first_half):
    partner = jnp.where(first_half, pltpu.roll(x, LANES - 32, 1), pltpu.roll(x, 32, 1))
    return x * cos + partner * sin_signed


def _qkv_kernel(*refs, stream):
    nsub = stream.nsub
    ng_ref, w_ref, cos_ref, sin_ref, q_ref, k_ref, vt_ref, p_ref = refs[3 * nsub:]
    tm = TOKEN_TILE
    nq, nk = q_ref.shape[2], k_ref.shape[2]
    lane = lax.broadcasted_iota(jnp.int32, (tm, LANES), 1)
    first_half = (lane % C_HEAD_DIM) < (C_HEAD_DIM // 2)
    ones_rows = (lane >= C_HEAD_DIM) & (lane < V_ROWS)

    def modulated(s, _):
        xc_ref, xl_ref, mod_ref = refs[3 * s:3 * s + 3]
        m = mod_ref[0, 0]
        return _modulate(stream.tile(xc_ref, xl_ref, s), ng_ref[0:1], m[0:1], m[1:2]).astype(BF16)

    def project(s, u):
        p_ref[s * tm:(s + 1) * tm, :] = jnp.dot(u, w_ref[...], preferred_element_type=F32)

    def rotate(s, _):
        rows = slice(s * tm, (s + 1) * tm)
        cos, sin = cos_ref[rows, :], sin_ref[rows, :]
        for j in range(nq // LANES):
            cols = slice(j * LANES, (j + 1) * LANES)
            q_ref[0, rows, cols] = (_rope(p_ref[rows, cols], cos, sin, first_half)
                                    * (C_HEAD_DIM ** -0.5)).astype(BF16)
        for j in range(nk // LANES):
            cols = slice(nq + j * LANES, nq + (j + 1) * LANES)
            k_ref[0, rows, j * LANES:(j + 1) * LANES] = _rope(p_ref[rows, cols], cos, sin, first_half).astype(BF16)
        for j in range(C_KV_HEADS):
            cols = slice(nq + nk + j * LANES, nq + nk + (j + 1) * LANES)
            vt_ref[0, j * LANES:(j + 1) * LANES, rows] = jnp.where(ones_rows, 1.0, p_ref[rows, cols]).T.astype(BF16)

    _staggered(nsub, [modulated, project, rotate])


def _qkv(stream, n_tiles, mod, ng, w, cos, sin):
    bsz = mod.shape[0]
    rows = stream.nsub * TOKEN_TILE
    assert n_tiles % stream.nsub == 0
    t = n_tiles * TOKEN_TILE
    nq = C_HEADS * C_HEAD_DIM
    nk = C_KV_HEADS * LANES
    specs, args = stream.operands(mod)
    return pl.pallas_call(
        functools.partial(_qkv_kernel, stream=stream),
        grid=(bsz, n_tiles // stream.nsub),
        in_specs=specs + [
            _const_spec(ng.shape), _const_spec(w.shape),
            pl.BlockSpec((rows, LANES), lambda b, i: (i, 0)),
            pl.BlockSpec((rows, LANES), lambda b, i: (i, 0)),
        ],
        out_specs=[
            pl.BlockSpec((1, rows, nq), lambda b, i: (b, i, 0)),
            pl.BlockSpec((1, rows, nk), lambda b, i: (b, i, 0)),
            pl.BlockSpec((1, nk, rows), lambda b, i: (b, 0, i)),
        ],
        out_shape=[jax.ShapeDtypeStruct((bsz, t, nq), BF16),
                   jax.ShapeDtypeStruct((bsz, t, nk), BF16),
                   jax.ShapeDtypeStruct((bsz, nk, t), BF16)],
        scratch_shapes=[pltpu.VMEM((rows, w.shape[1]), F32)],
        compiler_params=pltpu.CompilerParams(vmem_limit_bytes=VMEM_LIMIT),
        name="qkv",
    )(*args, ng, w, cos, sin)


def _attn_kernel(sink_ref, bias_ref, *refs, n_ctx):
    blk = ATT_BLOCK
    band = 3 * blk
    nsub = ATT_QUERY_BLOCKS
    q_refs = refs[:nsub]
    k_ref, vt_ref, o_ref = refs[nsub:]
    t = k_ref.shape[1]
    group = C_HEADS // C_KV_HEADS
    lane = lax.broadcasted_iota(jnp.int32, (blk, LANES), 1)
    low = lane < C_HEAD_DIM
    zero = jnp.zeros((blk, LANES), BF16)

    def band_start(sb):
        n = pl.program_id(1) * nsub + sb
        return pl.multiple_of(jnp.minimum(n_ctx + (n - 1) * blk, t - band), blk)

    def scores(sb, kg):
        cols = slice(kg * LANES, (kg + 1) * LANES)
        r_band = band_start(sb)
        parts, sinks = [], []
        for hh in range(group):
            h = kg * group + hh
            pair = q_refs[sb][0, :, (h // 2) * LANES:(h // 2 + 1) * LANES]
            parts.append(jnp.where(low if h % 2 == 0 else ~low, pair, zero))
            sinks.append(jnp.full((1, blk), sink_ref[h], F32))
        q4 = jnp.concatenate(parts, axis=0)
        sink = jnp.concatenate(sinks, axis=1)
        kk = jnp.concatenate([k_ref[0, pl.ds(r_band, band), cols], k_ref[0, 0:n_ctx, cols]], axis=0)
        s = lax.dot_general(kk, q4, NT, preferred_element_type=F32)
        bias = jnp.concatenate([bias_ref[sb]] * group, axis=1)
        s = jnp.concatenate([s[:band] + bias, s[band:]], axis=0)
        mx = jnp.maximum(jnp.max(s, axis=0, keepdims=True), sink)
        return s, mx, sink

    def probs(s, mx, sink):
        return jnp.exp((s - mx).astype(BF16)), jnp.exp(sink - mx)

    def finish(sb, kg, p, p_sink):
        r_band = band_start(sb)
        rows = slice(kg * LANES, kg * LANES + V_ROWS)
        vt = jnp.concatenate([vt_ref[0, rows, pl.ds(r_band, band)], vt_ref[0, rows, 0:n_ctx]], axis=1)
        o3 = jnp.dot(vt, p, preferred_element_type=F32)
        o = o3[:C_HEAD_DIM] * (1.0 / (o3[C_HEAD_DIM:C_HEAD_DIM + 1] + p_sink))
        for pp in range(group // 2):
            y = jnp.concatenate([o[:, (2 * pp) * blk:(2 * pp + 1) * blk],
                                 o[:, (2 * pp + 1) * blk:(2 * pp + 2) * blk]], axis=0)
            c0 = (kg * (group // 2) + pp) * LANES
            o_ref[0, sb * blk:(sb + 1) * blk, c0:c0 + LANES] = y.T.astype(o_ref.dtype)

    units = [(sb, kg) for sb in range(nsub) for kg in range(C_KV_HEADS)]
    nu = len(units)
    s_q = {0: scores(*units[0]), 1: scores(*units[1])}
    p_q = {0: probs(*s_q.pop(0))}
    for i in range(nu):
        if i + 2 < nu:
            s_q[i + 2] = scores(*units[i + 2])
        if i + 1 < nu:
            p_q[i + 1] = probs(*s_q.pop(i + 1))
        finish(*units[i], *p_q.pop(i))


def _attn_bias(n_blocks, n_ctx, t):
    blk, band = ATT_BLOCK, 3 * ATT_BLOCK
    n = np.arange(n_blocks)[:, None, None]
    r_band = np.minimum(n_ctx + (n - 1) * blk, t - band)
    kp = np.arange(band)[None, :, None] + (r_band - n_ctx)
    qi = np.arange(blk)[None, None, :] + n * blk
    ok = (np.abs(kp - qi) <= ATT_WINDOW) & (kp >= 0)
    return jnp.asarray(np.where(ok, 0.0, -1e30).astype(np.float32))


def _attention(q, k2, vt, sink, n_ctx):
    bsz, t, nq = q.shape
    nk = k2.shape[2]
    rows = ATT_QUERY_BLOCKS * ATT_BLOCK
    n_blocks = (t - n_ctx) // ATT_BLOCK
    assert n_ctx % ATT_BLOCK == 0 and (t - n_ctx) % rows == 0 and n_ctx >= ATT_BLOCK and n_blocks >= 2
    ctx_blocks = n_ctx // ATT_BLOCK
    nqb = ATT_QUERY_BLOCKS
    bias = _attn_bias(n_blocks, n_ctx, t)
    return pl.pallas_call(
        functools.partial(_attn_kernel, n_ctx=n_ctx),
        grid=(bsz, n_blocks // ATT_QUERY_BLOCKS),
        in_specs=[
            pl.BlockSpec(memory_space=pltpu.SMEM),
            pl.BlockSpec((ATT_QUERY_BLOCKS,) + bias.shape[1:], lambda b, n: (n, 0, 0)),
        ] + [
            pl.BlockSpec((1, ATT_BLOCK, nq), lambda b, n, s=s: (b, n * nqb + s + ctx_blocks, 0)) for s in range(nqb)
        ] + [
            pl.BlockSpec((1, t, nk), lambda b, n: (b, 0, 0)),
            pl.BlockSpec((1, nk, t), lambda b, n: (b, 0, 0)),
        ],
        out_specs=pl.BlockSpec((1, rows, nq), lambda b, n: (b, n, 0)),
        out_shape=jax.ShapeDtypeStruct((bsz, t - n_ctx, nq), BF16),
        compiler_params=pltpu.CompilerParams(vmem_limit_bytes=VMEM_LIMIT),
        name="attention",
    )(sink, bias, *([q] * nqb), k2, vt)


def _rec_weights(w_in, w_g2, b_g2):
    widths = (A_HEADS * A_KEY,) * 3 + (A_HEADS * A_VAL,) * 2 + (B_HEADS * B_KEY,) * 2 + (
        B_HEADS * B_VAL, B_GATE_RANK, B_GATE_RANK, B_HEADS * B_VAL)
    off = np.concatenate([[0], np.cumsum(widths)])
    seg = lambda i: w_in[:, off[i]:off[i + 1]]
    d = w_in.shape[0]
    pad = jnp.zeros((d, LANES - 2 * B_GATE_RANK), w_in.dtype)
    w = jnp.concatenate([seg(i) for i in (0, 1, 2, 5, 8, 9)] + [pad] + [seg(i) for i in (3, 4, 6, 7, 10)],
                        axis=1).astype(BF16)
    nk = B_HEADS * B_KEY
    w2 = jnp.zeros((LANES, 2 * nk), F32)
    w2 = w2.at[:B_GATE_RANK, :nk].set(w_g2[0]).at[B_GATE_RANK:2 * B_GATE_RANK, nk:].set(w_g2[1])
    return w, w2.astype(BF16), b_g2.reshape(1, 2 * nk)


def _att_weights(w_qkv):
    d = w_qkv.shape[0]
    nq, nkv = C_HEADS * C_HEAD_DIM, C_KV_HEADS * C_HEAD_DIM
    perm = np.concatenate([np.arange(0, C_HEAD_DIM, 2), np.arange(1, C_HEAD_DIM, 2)])
    wq = w_qkv[:, :nq].reshape(d, C_HEADS, C_HEAD_DIM)[:, :, perm].reshape(d, nq)
    wk = w_qkv[:, nq:nq + nkv].reshape(d, C_KV_HEADS, C_HEAD_DIM)[:, :, perm]
    wv = w_qkv[:, nq + nkv:].reshape(d, C_KV_HEADS, C_HEAD_DIM)
    wk = jnp.concatenate([wk, wk], axis=2).reshape(d, 2 * nkv)
    wv = jnp.concatenate([wv, jnp.zeros_like(wv)], axis=2).reshape(d, 2 * nkv)
    return jnp.concatenate([wq, wk, wv], axis=1).astype(BF16)


def _rope_tables(seq, n_ctx):
    n_rows = seq // GRID_W
    row = jnp.repeat(jnp.arange(n_rows), GRID_W).astype(F32)
    col = jnp.tile(jnp.arange(GRID_W), n_rows).astype(F32)
    half = C_HEAD_DIM // 2
    inv = ROPE_BASE ** (-jnp.arange(0, half, 2, dtype=F32) / half)
    ang = jnp.concatenate([row[:, None] * inv, col[:, None] * inv], axis=-1)
    cos, sin = jnp.cos(ang), jnp.sin(ang)
    cos = jnp.concatenate([jnp.ones((n_ctx, half), F32), cos], axis=0)
    sin = jnp.concatenate([jnp.zeros((n_ctx, half), F32), sin], axis=0)
    reps = LANES // C_HEAD_DIM
    return (jnp.tile(jnp.concatenate([cos, cos], axis=1), (1, reps)),
            jnp.tile(jnp.concatenate([-sin, sin], axis=1), (1, reps)))


def kernel(x, c, ctx, c_ctx, ada_w, ada_b, norm_g, rec_w_in, rec_w_out, rec_lb_logits, rec_w_g2, rec_b_g2,
           rec_gn_a, rec_gn_b, att_w_qkv, att_w_o, att_sink, ffn_w_in, ffn_w_out):
    bsz, seq, d = x.shape
    n_ctx = ctx.shape[1]
    depth = ada_w.shape[0]
    tm = TOKEN_TILE
    assert n_ctx % tm == 0 and seq % tm == 0 and n_ctx % SCAN_BLOCK == 0 and seq % SCAN_BLOCK == 0
    n_ctx_tiles = n_ctx // tm
    n_lat_tiles = seq // tm

    rows = -(-(bsz + 1) // 8) * 8
    cond = jnp.concatenate([c, c_ctx[None], jnp.zeros((rows - bsz - 1, d), F32)], axis=0)
    mod_all = _ada(cond, ada_w, ada_b)
    mods = []
    for l in range(depth):
        lat = mod_all[l, :bsz].reshape(bsz, 1, 6, d)
        cx = jnp.broadcast_to(mod_all[l, bsz].reshape(1, 1, 6, d), (bsz, 1, 6, d))
        mods.append(jnp.concatenate([cx, lat], axis=1))

    cos, sin = _rope_tables(seq, n_ctx)
    n_tiles = n_ctx_tiles + n_lat_tiles
    src = (ctx, x, 0)
    win_all, wout_all = ffn_w_in.astype(BF16), ffn_w_out.astype(BF16)

    for l in range(depth):
        need_ctx = l < depth - 1
        j = l // 2
        ng = norm_g[l]
        stream = _Stream(src[0], src[1], n_ctx_tiles, 0, src[2], _sub_tiles(n_tiles))
        if l % 2 == 0:
            w, w2, b2 = _rec_weights(rec_w_in[j], rec_w_g2[j], rec_b_g2[j])
            p = _rec_in(stream, n_tiles, mods[l], ng, w, w2, b2, rec_lb_logits, j)
            ya = _scan_hgrn(p, rec_gn_a[j].reshape(1, -1), n_ctx // SCAN_BLOCK)
            yb = _scan_gla(p, rec_gn_b[j].reshape(1, -1), n_ctx // SCAN_BLOCK)
            y0, y1, c0, c1, wo = ya, yb, 0, 0, rec_w_out[j]
            y_has_ctx = True
        else:
            q, k2, vt = _qkv(stream, n_tiles, mods[l], ng, _att_weights(att_w_qkv[j]), cos, sin)
            y = _attention(q, k2, vt, att_sink[j], n_ctx)
            y0, y1, c0, c1, wo = y, y, 0, 1, att_w_o[j]
            y_has_ctx = False
        wo = wo.astype(BF16)
        if need_ctx:
            assert y_has_ctx
            xcat = _post(y0, y1, c0, c1, stream, n_tiles, mods[l], ng, wo, win_all, wout_all, l)
            src = (xcat, xcat, n_ctx_tiles)
        else:
            if y_has_ctx:
                y0 = y0[:, :, n_ctx:]
                y1 = y1[:, :, n_ctx:]
            lat_only = _Stream(src[0], src[1], n_ctx_tiles, n_ctx_tiles, src[2], _sub_tiles(n_lat_tiles))
            x_lat = _post(y0, y1, c0, c1, lat_only, n_lat_tiles, mods[l], ng, wo, win_all, wout_all, l)
    return x_lat
```

```python
import functools

import numpy as np
import jax
import jax.numpy as jnp
from jax import lax
from jax.experimental import pallas as pl
from jax.experimental.pallas import tpu as pltpu

F32 = jnp.float32
BF16 = jnp.bfloat16
HIGHEST = lax.Precision.HIGHEST

EPS = 1e-6
LANES = 128
TOKEN_TILE = 256
SCAN_BLOCK = 256
LA_CHUNK = 32
SCAN_GROUP = 4 * LA_CHUNK
ATT_BLOCK = 128
ATT_WINDOW = 128
ATT_QUERY_BLOCKS = 4
V_ROWS = 80
GRID_W = 64
ROPE_BASE = 10000.0
GLA_GATE_NORM = 16.0
A_HEADS, A_KEY, A_VAL = 4, 128, 128
B_HEADS, B_KEY, B_VAL = 4, 64, 128
B_GATE_RANK = 16
C_HEADS, C_KV_HEADS, C_HEAD_DIM = 16, 4, 64
VMEM_LIMIT = 56 * 1024 * 1024

NT = (((1,), (1,)), ((), ()))
TN = (((0,), (0,)), ((), ()))


def _silu(x):
    h = 0.5 * x
    return h * (jnp.tanh(h) + 1.0)


def _log_sigmoid(x):
    return jnp.minimum(x, 0.0) - jnp.log(1.0 + jnp.exp(-jnp.abs(x)))


def _rms(x, g):
    ms = jnp.mean(x * x, axis=-1, keepdims=True)
    return (x * lax.rsqrt(ms + EPS)) * g


def _modulate(x, g, shift, scale):
    return _rms(x, g * (1.0 + scale)) + shift


def _const_spec(shape):
    nd = len(shape)
    return pl.BlockSpec(shape, lambda *_: (0,) * nd, pipeline_mode=pl.Buffered(1))


class _Stream:
    def __init__(self, ctx_arr, lat_arr, n_ctx_tiles, tile0=0, lat_off=0, nsub=1):
        self.arrays = (ctx_arr, lat_arr)
        self.n_ctx_tiles, self.tile0, self.lat_off, self.nsub = n_ctx_tiles, tile0, lat_off, nsub
        self.d = ctx_arr.shape[-1]

    def specs_for(self, s):
        nct, off, lat_off, nsub = self.n_ctx_tiles, self.tile0 + s, self.lat_off, self.nsub
        return [
            pl.BlockSpec((1, TOKEN_TILE, self.d), lambda b, i: (b, jnp.minimum(i * nsub + off, nct - 1), 0)),
            pl.BlockSpec((1, TOKEN_TILE, self.d),
                         lambda b, i: (b, jnp.maximum(i * nsub + off - nct, 0) + lat_off, 0)),
        ]

    def mod_spec_for(self, s):
        nct, off, nsub = self.n_ctx_tiles, self.tile0 + s, self.nsub
        return pl.BlockSpec((1, 1, 6, self.d), lambda b, i: (b, (i * nsub + off >= nct).astype(jnp.int32), 0, 0))

    def tile(self, ctx_ref, lat_ref, s):
        if self.tile0 >= self.n_ctx_tiles:
            return lat_ref[0]
        g = pl.program_id(1) * self.nsub + s + self.tile0
        return jnp.where(g < self.n_ctx_tiles, ctx_ref[0], lat_ref[0])

    def operands(self, mod):
        specs, args = [], []
        for s in range(self.nsub):
            specs += self.specs_for(s) + [self.mod_spec_for(s)]
            args += [*self.arrays, mod]
        return specs, args


def _sub_tiles(n_tiles):
    return next(n for n in (4, 3, 2, 1) if n_tiles % n == 0)


def _staggered(nsub, stages):
    live = {}
    for step in range(nsub + len(stages) - 1):
        for k in range(len(stages)):
            s = step - k
            if 0 <= s < nsub:
                live[s] = stages[k](s, live.get(s))


def _ada_kernel(c_ref, w_ref, b_ref, o_ref):
    s = _silu(c_ref[...])
    o_ref[0] = jnp.dot(s, w_ref[0], preferred_element_type=F32, precision=HIGHEST) + b_ref[0]


def _ada(cond, ada_w, ada_b):
    depth, d, n = ada_w.shape
    rows = cond.shape[0]
    tn = n // 4
    return pl.pallas_call(
        _ada_kernel,
        grid=(depth, n // tn),
        in_specs=[
            pl.BlockSpec((rows, d), lambda l, j: (0, 0)),
            pl.BlockSpec((1, d, tn), lambda l, j: (l, 0, j)),
            pl.BlockSpec((1, 1, tn), lambda l, j: (l, 0, j)),
        ],
        out_specs=pl.BlockSpec((1, rows, tn), lambda l, j: (l, 0, j)),
        out_shape=jax.ShapeDtypeStruct((depth, rows, n), F32),
        compiler_params=pltpu.CompilerParams(vmem_limit_bytes=VMEM_LIMIT),
        name="ada",
    )(cond, ada_w, ada_b.reshape(depth, 1, n))


def _recin_kernel(*refs, stream, layer_slot):
    nsub = stream.nsub
    ng_ref, w_ref, w2_ref, b2_ref, lb_ref, o_ref = refs[3 * nsub:]
    tm = TOKEN_TILE
    lg = lb_ref[...]
    e = jnp.exp(lg - jnp.max(lg, axis=0, keepdims=True))
    lb = jnp.sum((e / jnp.sum(e, axis=0, keepdims=True))[:layer_slot + 1], axis=0)
    na, nb = A_HEADS * A_KEY, B_HEADS * B_KEY
    n_act = 3 * na + nb

    def modulated(s, _):
        xc_ref, xl_ref, mod_ref = refs[3 * s:3 * s + 3]
        m = mod_ref[0, 0]
        return _modulate(stream.tile(xc_ref, xl_ref, s), ng_ref[0:1], m[0:1], m[1:2]).astype(BF16)

    def project(s, u):
        rows = slice(s * tm, (s + 1) * tm)

        def put(c0, val):
            for j in range(val.shape[1] // LANES):
                o_ref[c0 // LANES + j, rows, :] = val[:, j * LANES:(j + 1) * LANES]

        p = jnp.dot(u, w_ref[:, :n_act + LANES], preferred_element_type=F32)
        put(0, p[:, :n_act])
        pre = jnp.dot(p[:, n_act:].astype(BF16), w2_ref[...], preferred_element_type=F32)
        put(n_act, pre + b2_ref[...])
        put(n_act + 2 * nb, jnp.dot(u, w_ref[:, n_act + LANES:], preferred_element_type=F32))

    def activate(s, _):
        rows = slice(s * tm, (s + 1) * tm)
        for c0 in range(0, n_act + 2 * nb, LANES):
            v = o_ref[c0 // LANES, rows, :]
            if c0 < na:
                v = _silu(v) * (A_KEY ** -0.5)
            elif c0 < 3 * na:
                d, h0 = divmod(c0 - na, na)
                lbd = lb[d:d + 1, h0:h0 + LANES]
                v = jnp.log(0.5 * (1.0 + lbd) + (0.5 * (1.0 - lbd)) * jnp.tanh(0.5 * v))
            elif c0 < n_act:
                v = v * (B_KEY ** -0.5)
            else:
                v = _log_sigmoid(v) * (1.0 / GLA_GATE_NORM)
            o_ref[c0 // LANES, rows, :] = v

    _staggered(nsub, [modulated, project, activate])


def _rec_in(stream, n_tiles, mod, ng, w, w2, b2, lb_logits, layer_slot):
    bsz = mod.shape[0]
    n_out = w.shape[1] - LANES + w2.shape[1]
    rows = stream.nsub * TOKEN_TILE
    assert n_tiles % stream.nsub == 0
    specs, args = stream.operands(mod)
    consts = [ng, w, w2, b2, lb_logits]
    return pl.pallas_call(
        functools.partial(_recin_kernel, stream=stream, layer_slot=layer_slot),
        grid=(bsz, n_tiles // stream.nsub),
        in_specs=specs + [_const_spec(a.shape) for a in consts],
        out_specs=pl.BlockSpec((None, n_out // LANES, rows, LANES), lambda b, i: (b, 0, i, 0)),
        out_shape=jax.ShapeDtypeStruct((bsz, n_out // LANES, n_tiles * TOKEN_TILE, LANES), F32),
        compiler_params=pltpu.CompilerParams(vmem_limit_bytes=VMEM_LIMIT),
        name="rec_in",
    )(*args, *consts)


def _chunk_cumsum(g, reverse):
    r, width = g.shape
    sl = 8
    x = g.reshape(r // sl, sl, width)
    sub = lax.broadcasted_iota(jnp.int32, x.shape, 1)
    for s in (1, 2, 4):
        if reverse:
            x = x + jnp.where(sub < sl - s, pltpu.roll(x, sl - s, 1), 0.0)
        else:
            x = x + jnp.where(sub >= s, pltpu.roll(x, s, 1), 0.0)
    per = LA_CHUNK // sl
    x = x.reshape(r // LA_CHUNK, per, sl, width)
    parts = [None] * per
    carry = None
    for v in (range(per - 1, -1, -1) if reverse else range(per)):
        xv = x[:, v:v + 1]
        parts[v] = xv if carry is None else xv + carry
        edge = xv[:, :, 0:1] if reverse else xv[:, :, sl - 1:sl]
        carry = edge if carry is None else carry + edge
    return jnp.concatenate(parts, axis=1).reshape(r, width)


def _group_masks(reverse):
    n = SCAN_GROUP
    ri = lax.broadcasted_iota(jnp.int32, (n, n), 0)
    ci = lax.broadcasted_iota(jnp.int32, (n, n), 1)
    rc, cc = ri // LA_CHUNK, ci // LA_CHUNK
    if reverse:
        return (rc == cc) & (ci >= ri), (rc + 1 == cc) & (rc % 2 == 0), (rc < 2) & (cc >= 2)
    return (rc == cc) & (ci <= ri), (rc == cc + 1) & (rc % 2 == 1), (rc >= 2) & (cc < 2)


def _scale_chunks(x, factors):
    parts = []
    for c, f in enumerate(factors):
        xc = x[c * LA_CHUNK:(c + 1) * LA_CHUNK]
        parts.append(xc if f is None else xc * f)
    return jnp.concatenate(parts, axis=0)


def _scan_prep(q, k, g, reverse):
    bc = _chunk_cumsum(g, reverse)
    e = jnp.exp(bc)
    return bc, q * e, k / e


def _scan_group(gi, prep, v_bf, states, masks, lane_masks, reverse):
    bc, q_dec, k_inv = prep
    nsub = len(states)
    nch = SCAN_GROUP // LA_CHUNK
    assert nch == 4
    m_diag, m_adj, m_far = masks
    pos = (lambda c: nch - 1 - c) if reverse else (lambda c: c)
    edge = 0 if reverse else LA_CHUNK - 1
    r0 = gi * SCAN_GROUP
    rs = slice(r0, r0 + SCAN_GROUP)
    tot = [bc[r0 + c * LA_CHUNK + edge:r0 + c * LA_CHUNK + edge + 1] for c in range(nch)]
    ts = [tot[pos(s)] for s in range(nch)]
    e_ts = [jnp.exp(t) for t in ts]
    e_pre = [None, e_ts[0], jnp.exp(ts[0] + ts[1]), jnp.exp(ts[0] + ts[1] + ts[2])]
    e_post = [jnp.exp(ts[1] + ts[2] + ts[3]), jnp.exp(ts[2] + ts[3]), e_ts[3], None]
    e_total = jnp.exp(ts[0] + ts[1] + ts[2] + ts[3])
    ki = k_inv[rs]
    ke = _scale_chunks(ki, [e_ts[pos(c)] for c in range(nch)])
    kh = _scale_chunks(ke, [e_post[pos(c)] for c in range(nch)])
    kb = _scale_chunks(ke, [e_ts[1] if pos(c) == 0 else None for c in range(nch)])
    kcat = jnp.concatenate([ki.astype(BF16), ke.astype(BF16)], axis=0)
    kb, kh = kb.astype(BF16), kh.astype(BF16)
    outs, new_states = [], []
    for j in range(nsub):
        qd = q_dec[rs]
        if lane_masks is not None:
            qd = jnp.where(lane_masks[j], qd, 0.0)
        qt = _scale_chunks(qd, [e_pre[pos(c)] for c in range(nch)]).astype(BF16)
        qb = _scale_chunks(qd, [e_ts[2] if pos(c) == 3 else None for c in range(nch)]).astype(BF16)
        qd = qd.astype(BF16)
        v_g = v_bf[j][rs]
        s12 = lax.dot_general(qd, kcat, NT, preferred_element_type=F32)
        s3 = lax.dot_general(qb, kb, NT, preferred_element_type=F32)
        o_state = lax.dot_general(qt, states[j].astype(BF16), NT, preferred_element_type=F32)
        upd = lax.dot_general(v_g, kh, TN, preferred_element_type=F32)
        sc = jnp.where(m_diag, s12[:, :SCAN_GROUP],
                       jnp.where(m_adj, s12[:, SCAN_GROUP:], jnp.where(m_far, s3, 0.0)))
        outs.append(o_state + jnp.dot(sc.astype(BF16), v_g, preferred_element_type=F32))
        new_states.append(states[j] * e_total + upd)
    return outs, tuple(new_states)


def _scan_kernel(*refs, mode, n_ctx_blocks):
    if mode == "hgrn":
        (q_ref, gf_ref, gb_ref, v_ref, og_ref, gn_ref, o_ref, of_ref, ob_ref) = refs
        nsub = 1
    else:
        (q_ref, k_ref, gf_ref, gb_ref, v_ref, og_ref, gn_ref, o_ref, of_ref, ob_ref) = refs
        nsub = 2
    t = q_ref.shape[0]
    blk = SCAN_BLOCK
    nblk = t // blk
    masks_f, masks_b = _group_masks(False), _group_masks(True)
    if nsub == 2:
        lane = lax.broadcasted_iota(jnp.int32, (SCAN_GROUP, LANES), 1)
        lane_masks = [lane < B_KEY, lane >= B_KEY]
    else:
        lane_masks = None

    def features(rows, d):
        g = (gf_ref, gb_ref)[d][rows, :]
        if mode == "hgrn":
            return q_ref[rows, :], 1.0 - jnp.exp(g), g
        return q_ref[rows, :], k_ref[rows, :], g

    ngroup = blk // SCAN_GROUP

    def prep(i):
        jb = n_ctx_blocks - 1 - i if i < n_ctx_blocks else nblk - 1 - (i - n_ctx_blocks)
        rows_f = slice(i * blk, (i + 1) * blk)
        rows_b = slice(jb * blk, (jb + 1) * blk)
        return (jb, _scan_prep(*features(rows_f, 0), False), _scan_prep(*features(rows_b, 1), True),
                [v_ref[j, rows_f, :].astype(BF16) for j in range(nsub)],
                [v_ref[j, rows_b, :].astype(BF16) for j in range(nsub)])

    def finish(blk_i):
        rows = slice(blk_i * blk, (blk_i + 1) * blk)
        for j in range(nsub):
            cols = slice(j * LANES, (j + 1) * LANES)
            o = of_ref[rows, cols] + ob_ref[rows, cols]
            y = _rms(o, gn_ref[...]) * _silu(og_ref[j, rows, :])
            o_ref[j, rows, :] = y.astype(o_ref.dtype)

    st_f = st_b = tuple(jnp.zeros((LANES, LANES), F32) for _ in range(nsub))
    seen = {}
    nxt = prep(0)
    for i in range(nblk):
        jb, prep_f, prep_b, v_f, v_b = nxt
        if i + 1 < nblk:
            nxt = prep(i + 1)
        for s in range(ngroup):
            gf, gb = s, ngroup - 1 - s
            o, st_f = _scan_group(gf, prep_f, v_f, st_f, masks_f, lane_masks, False)
            r0 = i * blk + gf * SCAN_GROUP
            for j in range(nsub):
                of_ref[r0:r0 + SCAN_GROUP, j * LANES:(j + 1) * LANES] = o[j]
            o, st_b = _scan_group(gb, prep_b, v_b, st_b, masks_b, lane_masks, True)
            r0 = jb * blk + gb * SCAN_GROUP
            for j in range(nsub):
                ob_ref[r0:r0 + SCAN_GROUP, j * LANES:(j + 1) * LANES] = o[j]
        for done in (i, jb):
            seen[done] = seen.get(done, 0) + 1
            if seen[done] == 2:
                finish(done)


def _group_spec(t, first, per_step=None):
    if per_step is None:
        return pl.BlockSpec((None, None, t, LANES), lambda b, h: (b, first + h, 0, 0))
    assert first % per_step == 0
    return pl.BlockSpec((None, per_step, t, LANES), lambda b, h: (b, first // per_step + h, 0, 0))


def _scan_hgrn(p, gn, n_ctx_blocks):
    bsz, _, t, _ = p.shape
    one, sub = (lambda first: _group_spec(t, first)), (lambda first: _group_spec(t, first, 1))
    return pl.pallas_call(
        functools.partial(_scan_kernel, mode="hgrn", n_ctx_blocks=n_ctx_blocks),
        grid=(bsz, A_HEADS),
        in_specs=[one(0), one(4), one(8), sub(18), sub(22), _const_spec(gn.shape)],
        out_specs=sub(0),
        out_shape=jax.ShapeDtypeStruct((bsz, A_HEADS, t, A_VAL), BF16),
        scratch_shapes=[pltpu.VMEM((t, LANES), F32), pltpu.VMEM((t, LANES), F32)],
        compiler_params=pltpu.CompilerParams(vmem_limit_bytes=VMEM_LIMIT),
        name="scan_hgrn",
    )(p, p, p, p, p, gn)


def _scan_gla(p, gn, n_ctx_blocks):
    bsz, _, t, _ = p.shape
    one, two = (lambda first: _group_spec(t, first)), (lambda first: _group_spec(t, first, 2))
    return pl.pallas_call(
        functools.partial(_scan_kernel, mode="gla", n_ctx_blocks=n_ctx_blocks),
        grid=(bsz, B_HEADS // 2),
        in_specs=[one(12), one(26), one(14), one(16), two(28), two(32), _const_spec(gn.shape)],
        out_specs=two(0),
        out_shape=jax.ShapeDtypeStruct((bsz, B_HEADS, t, B_VAL), BF16),
        scratch_shapes=[pltpu.VMEM((t, 2 * LANES), F32), pltpu.VMEM((t, 2 * LANES), F32)],
        compiler_params=pltpu.CompilerParams(vmem_limit_bytes=VMEM_LIMIT),
        name="scan_gla",
    )(p, p, p, p, p, p, gn)


def _post_kernel(*refs, stream):
    nsub = stream.nsub
    y_refs = refs[3 * nsub:5 * nsub]
    ng_ref, wo_ref, win_ref, wout_ref, o_ref = refs[5 * nsub:]
    hid = wout_ref.shape[0]
    tm = TOKEN_TILE

    def mix(s, _):
        xc_ref, xl_ref, mod_ref = refs[3 * s:3 * s + 3]
        y0_ref, y1_ref = y_refs[2 * s:2 * s + 2]
        m = mod_ref[0, 0]
        half = wo_ref.shape[0] // 2

        def rows_of(y_ref):
            if len(y_ref.shape) == 2:
                return y_ref[...]
            return jnp.concatenate([y_ref[h] for h in range(y_ref.shape[0])], axis=1)

        t = jnp.dot(rows_of(y0_ref), wo_ref[:half, :], preferred_element_type=F32)
        t = t + jnp.dot(rows_of(y1_ref), wo_ref[half:, :], preferred_element_type=F32)
        x1 = stream.tile(xc_ref, xl_ref, s) + _rms(t, ng_ref[1:2] * m[2:3])
        return x1, _modulate(x1, ng_ref[2:3], m[3:4], m[4:5]).astype(BF16)

    def ffn_in(s, carry):
        x1, u = carry
        h = jnp.dot(u, win_ref[...], preferred_element_type=F32)
        return x1, (_silu(h[:, :hid]) * h[:, hid:]).astype(BF16)

    def ffn_out(s, carry):
        x1, act = carry
        m = refs[3 * s + 2][0, 0]
        h2 = jnp.dot(act, wout_ref[...], preferred_element_type=F32)
        o_ref[0, s * tm:(s + 1) * tm, :] = x1 + _rms(h2, ng_ref[3:4] * m[5:6])

    _staggered(nsub, [mix, ffn_in, ffn_out])


def _layer_spec(stacked, layer):
    nd = stacked.ndim - 1
    return pl.BlockSpec((None,) + stacked.shape[1:], lambda *_: (layer,) + (0,) * nd, pipeline_mode=pl.Buffered(1))


def _post(y0, y1, y_col0, y_col1, stream, n_tiles, mod, ng, wo, win_all, wout_all, layer):
    bsz, d = mod.shape[0], mod.shape[-1]
    tm = TOKEN_TILE
    half = d // 2
    nsub = stream.nsub
    assert n_tiles % nsub == 0
    specs, args = stream.operands(mod)
    def y_spec(y, col, s):
        if y.ndim == 4:
            return pl.BlockSpec((None, y.shape[1], tm, LANES), lambda b, i: (b, 0, i * nsub + s, 0))
        return pl.BlockSpec((None, tm, half), lambda b, i: (b, i * nsub + s, col))

    for s in range(nsub):
        specs += [y_spec(y0, y_col0, s), y_spec(y1, y_col1, s)]
        args += [y0, y1]
    consts = [ng, wo, win_all, wout_all]
    return pl.pallas_call(
        functools.partial(_post_kernel, stream=stream),
        grid=(bsz, n_tiles // nsub),
        in_specs=specs + [_const_spec(ng.shape), _const_spec(wo.shape),
                          _layer_spec(win_all, layer), _layer_spec(wout_all, layer)],
        out_specs=pl.BlockSpec((1, nsub * tm, d), lambda b, i: (b, i, 0)),
        out_shape=jax.ShapeDtypeStruct((bsz, n_tiles * tm, d), F32),
        compiler_params=pltpu.CompilerParams(vmem_limit_bytes=VMEM_LIMIT),
        name="post",
    )(*args, *consts)


def _rope(x, cos, sin_signed, first_half):
    partner = jnp.where(first_half, pltpu.roll(x, LANES - 32, 1), pltpu.roll(x, 32, 1))
    return x * cos + partner * sin_signed


def _qkv_kernel(*refs, stream):
    nsub = stream.nsub
    ng_ref, w_ref, cos_ref, sin_ref, q_ref, k_ref, vt_ref, p_ref = refs[3 * nsub:]
    tm = TOKEN_TILE
    nq, nk = q_ref.shape[2], k_ref.shape[2]
    lane = lax.broadcasted_iota(jnp.int32, (tm, LANES), 1)
    first_half = (lane % C_HEAD_DIM) < (C_HEAD_DIM // 2)

    def modulated(s, _):
        xc_ref, xl_ref, mod_ref = refs[3 * s:3 * s + 3]
        m = mod_ref[0, 0]
        return _modulate(stream.tile(xc_ref, xl_ref, s), ng_ref[0:1], m[0:1], m[1:2]).astype(BF16)

    def project(s, u):
        p_ref[s * tm:(s + 1) * tm, :] = jnp.dot(u, w_ref[...], preferred_element_type=F32)

    def rotate(s, _):
        rows = slice(s * tm, (s + 1) * tm)
        cos, sin = cos_ref[rows, :], sin_ref[rows, :]
        for j in range(nq // LANES):
            cols = slice(j * LANES, (j + 1) * LANES)
            q_ref[0, rows, cols] = (_rope(p_ref[rows, cols], cos, sin, first_half)
                                    * (C_HEAD_DIM ** -0.5)).astype(BF16)
        for j in range(nk // LANES):
            cols = slice(nq + j * LANES, nq + (j + 1) * LANES)
            k_ref[0, rows, j * LANES:(j + 1) * LANES] = _rope(p_ref[rows, cols], cos, sin, first_half).astype(BF16)
        vt_ref[0, :, rows] = p_ref[rows, nq + nk:].T.astype(BF16)

    _staggered(nsub, [modulated, project, rotate])


def _qkv(stream, n_tiles, mod, ng, w, cos, sin):
    bsz = mod.shape[0]
    rows = stream.nsub * TOKEN_TILE
    assert n_tiles % stream.nsub == 0
    t = n_tiles * TOKEN_TILE
    nq = C_HEADS * C_HEAD_DIM
    nk = C_KV_HEADS * C_HEAD_DIM
    specs, args = stream.operands(mod)
    return pl.pallas_call(
        functools.partial(_qkv_kernel, stream=stream),
        grid=(bsz, n_tiles // stream.nsub),
        in_specs=specs + [
            _const_spec(ng.shape), _const_spec(w.shape),
            pl.BlockSpec((rows, LANES), lambda b, i: (i, 0)),
            pl.BlockSpec((rows, LANES), lambda b, i: (i, 0)),
        ],
        out_specs=[
            pl.BlockSpec((1, rows, nq), lambda b, i: (b, i, 0)),
            pl.BlockSpec((1, rows, nk), lambda b, i: (b, i, 0)),
            pl.BlockSpec((1, nk, rows), lambda b, i: (b, 0, i)),
        ],
        out_shape=[jax.ShapeDtypeStruct((bsz, t, nq), BF16),
                   jax.ShapeDtypeStruct((bsz, t, nk), BF16),
                   jax.ShapeDtypeStruct((bsz, nk, t), BF16)],
        scratch_shapes=[pltpu.VMEM((rows, w.shape[1]), F32)],
        compiler_params=pltpu.CompilerParams(vmem_limit_bytes=VMEM_LIMIT),
        name="qkv",
    )(*args, ng, w, cos, sin)


def _attn_kernel(sink_ref, bias_ref, *refs, n_ctx):
    blk = ATT_BLOCK
    band = 3 * blk
    nsub = ATT_QUERY_BLOCKS
    q_refs = refs[:nsub]
    k_ref, vt_ref, o_ref = refs[nsub:]
    t = k_ref.shape[1]
    group = C_HEADS // C_KV_HEADS
    lane = lax.broadcasted_iota(jnp.int32, (blk, LANES), 1)
    low = lane < C_HEAD_DIM
    zero = jnp.zeros((blk, LANES), BF16)

    def band_start(sb):
        n = pl.program_id(1) * nsub + sb
        return pl.multiple_of(jnp.minimum(n_ctx + (n - 1) * blk, t - band), blk)

    def scores(sb, kg):
        a, high = divmod(kg, 2)
        cols = slice(a * LANES, (a + 1) * LANES)
        r_band = band_start(sb)
        parts, sinks = [], []
        for hh in range(group):
            pair = q_refs[sb][0, :, (a * group + hh) * LANES:(a * group + hh + 1) * LANES]
            parts.append(jnp.where(~low if high else low, pair, zero))
            sinks.append(jnp.full((1, blk), sink_ref[kg * group + hh], F32))
        q4 = jnp.concatenate(parts, axis=0)
        sink = jnp.concatenate(sinks, axis=1)
        kk = jnp.concatenate([k_ref[0, pl.ds(r_band, band), cols], k_ref[0, 0:n_ctx, cols]], axis=0)
        s = lax.dot_general(kk, q4, NT, preferred_element_type=F32)
        bias = jnp.concatenate([bias_ref[sb]] * group, axis=1)
        s = jnp.concatenate([s[:band] + bias, s[band:]], axis=0)
        mx = jnp.maximum(jnp.max(s, axis=0, keepdims=True), sink)
        return s, mx, sink

    def probs(s, mx, sink):
        return jnp.exp((s - mx).astype(BF16)), jnp.exp(sink - mx)

    def finish(sb, kg, p, p_sink):
        r_band = band_start(sb)
        rows = slice(kg * C_HEAD_DIM, (kg + 1) * C_HEAD_DIM)
        vt = jnp.concatenate([vt_ref[0, rows, pl.ds(r_band, band)], vt_ref[0, rows, 0:n_ctx]], axis=1)
        vt = jnp.concatenate([vt, jnp.ones((V_ROWS - C_HEAD_DIM, band + n_ctx), BF16)], axis=0)
        o3 = jnp.dot(vt, p, preferred_element_type=F32)
        o = o3[:C_HEAD_DIM] * (1.0 / (o3[C_HEAD_DIM:C_HEAD_DIM + 1] + p_sink))
        for pp in range(group // 2):
            y = jnp.concatenate([o[:, (2 * pp) * blk:(2 * pp + 1) * blk],
                                 o[:, (2 * pp + 1) * blk:(2 * pp + 2) * blk]], axis=0)
            c0 = (kg * (group // 2) + pp) * LANES
            o_ref[0, sb * blk:(sb + 1) * blk, c0:c0 + LANES] = y.T.astype(o_ref.dtype)

    units = [(sb, kg) for sb in range(nsub) for kg in range(C_KV_HEADS)]
    nu = len(units)
    s_q = {0: scores(*units[0]), 1: scores(*units[1])}
    p_q = {0: probs(*s_q.pop(0))}
    for i in range(nu):
        if i + 2 < nu:
            s_q[i + 2] = scores(*units[i + 2])
        if i + 1 < nu:
            p_q[i + 1] = probs(*s_q.pop(i + 1))
        finish(*units[i], *p_q.pop(i))


def _attn_bias(n_blocks, n_ctx, t):
    blk, band = ATT_BLOCK, 3 * ATT_BLOCK
    n = np.arange(n_blocks)[:, None, None]
    r_band = np.minimum(n_ctx + (n - 1) * blk, t - band)
    kp = np.arange(band)[None, :, None] + (r_band - n_ctx)
    qi = np.arange(blk)[None, None, :] + n * blk
    ok = (np.abs(kp - qi) <= ATT_WINDOW) & (kp >= 0)
    return jnp.asarray(np.where(ok, 0.0, -1e30).astype(np.float32))


def _attention(q, k2, vt, sink, n_ctx):
    bsz, t, nq = q.shape
    nk = k2.shape[2]
    rows = ATT_QUERY_BLOCKS * ATT_BLOCK
    n_blocks = (t - n_ctx) // ATT_BLOCK
    assert n_ctx % ATT_BLOCK == 0 and (t - n_ctx) % rows == 0 and n_ctx >= ATT_BLOCK and n_blocks >= 2
    ctx_blocks = n_ctx // ATT_BLOCK
    nqb = ATT_QUERY_BLOCKS
    bias = _attn_bias(n_blocks, n_ctx, t)
    return pl.pallas_call(
        functools.partial(_attn_kernel, n_ctx=n_ctx),
        grid=(bsz, n_blocks // ATT_QUERY_BLOCKS),
        in_specs=[
            pl.BlockSpec(memory_space=pltpu.SMEM),
            pl.BlockSpec((ATT_QUERY_BLOCKS,) + bias.shape[1:], lambda b, n: (n, 0, 0)),
        ] + [
            pl.BlockSpec((1, ATT_BLOCK, nq), lambda b, n, s=s: (b, n * nqb + s + ctx_blocks, 0)) for s in range(nqb)
        ] + [
            pl.BlockSpec((1, t, nk), lambda b, n: (b, 0, 0)),
            pl.BlockSpec((1, nk, t), lambda b, n: (b, 0, 0)),
        ],
        out_specs=pl.BlockSpec((1, rows, nq), lambda b, n: (b, n, 0)),
        out_shape=jax.ShapeDtypeStruct((bsz, t - n_ctx, nq), BF16),
        compiler_params=pltpu.CompilerParams(vmem_limit_bytes=VMEM_LIMIT),
        name="attention",
    )(sink, bias, *([q] * nqb), k2, vt)


def _rec_weights(w_in, w_g2, b_g2):
    widths = (A_HEADS * A_KEY,) * 3 + (A_HEADS * A_VAL,) * 2 + (B_HEADS * B_KEY,) * 2 + (
        B_HEADS * B_VAL, B_GATE_RANK, B_GATE_RANK, B_HEADS * B_VAL)
    off = np.concatenate([[0], np.cumsum(widths)])
    seg = lambda i: w_in[:, off[i]:off[i + 1]]
    d = w_in.shape[0]
    pad = jnp.zeros((d, LANES - 2 * B_GATE_RANK), w_in.dtype)
    w = jnp.concatenate([seg(i) for i in (0, 1, 2, 5, 8, 9)] + [pad] + [seg(i) for i in (3, 4, 6, 7, 10)],
                        axis=1).astype(BF16)
    nk = B_HEADS * B_KEY
    w2 = jnp.zeros((LANES, 2 * nk), F32)
    w2 = w2.at[:B_GATE_RANK, :nk].set(w_g2[0]).at[B_GATE_RANK:2 * B_GATE_RANK, nk:].set(w_g2[1])
    return w, w2.astype(BF16), b_g2.reshape(1, 2 * nk)


def _att_weights(w_qkv):
    d = w_qkv.shape[0]
    nq, nkv = C_HEADS * C_HEAD_DIM, C_KV_HEADS * C_HEAD_DIM
    group = C_HEADS // C_KV_HEADS
    perm = np.concatenate([np.arange(0, C_HEAD_DIM, 2), np.arange(1, C_HEAD_DIM, 2)])
    order = [(2 * a + r) * group + i for a in range(C_KV_HEADS // 2) for i in range(group) for r in range(2)]
    wq = w_qkv[:, :nq].reshape(d, C_HEADS, C_HEAD_DIM)[:, order][:, :, perm].reshape(d, nq)
    wk = w_qkv[:, nq:nq + nkv].reshape(d, C_KV_HEADS, C_HEAD_DIM)[:, :, perm].reshape(d, nkv)
    return jnp.concatenate([wq, wk, w_qkv[:, nq + nkv:]], axis=1).astype(BF16)


def _rope_tables(seq, n_ctx):
    n_rows = seq // GRID_W
    row = jnp.repeat(jnp.arange(n_rows), GRID_W).astype(F32)
    col = jnp.tile(jnp.arange(GRID_W), n_rows).astype(F32)
    half = C_HEAD_DIM // 2
    inv = ROPE_BASE ** (-jnp.arange(0, half, 2, dtype=F32) / half)
    ang = jnp.concatenate([row[:, None] * inv, col[:, None] * inv], axis=-1)
    cos, sin = jnp.cos(ang), jnp.sin(ang)
    cos = jnp.concatenate([jnp.ones((n_ctx, half), F32), cos], axis=0)
    sin = jnp.concatenate([jnp.zeros((n_ctx, half), F32), sin], axis=0)
    reps = LANES // C_HEAD_DIM
    return (jnp.tile(jnp.concatenate([cos, cos], axis=1), (1, reps)),
            jnp.tile(jnp.concatenate([-sin, sin], axis=1), (1, reps)))


def kernel(x, c, ctx, c_ctx, ada_w, ada_b, norm_g, rec_w_in, rec_w_out, rec_lb_logits, rec_w_g2, rec_b_g2,
           rec_gn_a, rec_gn_b, att_w_qkv, att_w_o, att_sink, ffn_w_in, ffn_w_out):
    bsz, seq, d = x.shape
    n_ctx = ctx.shape[1]
    depth = ada_w.shape[0]
    tm = TOKEN_TILE
    assert n_ctx % tm == 0 and seq % tm == 0 and n_ctx % SCAN_BLOCK == 0 and seq % SCAN_BLOCK == 0
    n_ctx_tiles = n_ctx // tm
    n_lat_tiles = seq // tm

    rows = -(-(bsz + 1) // 8) * 8
    cond = jnp.concatenate([c, c_ctx[None], jnp.zeros((rows - bsz - 1, d), F32)], axis=0)
    mod_all = _ada(cond, ada_w, ada_b)
    mods = []
    for l in range(depth):
        lat = mod_all[l, :bsz].reshape(bsz, 1, 6, d)
        cx = jnp.broadcast_to(mod_all[l, bsz].reshape(1, 1, 6, d), (bsz, 1, 6, d))
        mods.append(jnp.concatenate([cx, lat], axis=1))

    cos, sin = _rope_tables(seq, n_ctx)
    n_tiles = n_ctx_tiles + n_lat_tiles
    src = (ctx, x, 0)
    win_all, wout_all = ffn_w_in.astype(BF16), ffn_w_out.astype(BF16)

    for l in range(depth):
        need_ctx = l < depth - 1
        j = l // 2
        ng = norm_g[l]
        stream = _Stream(src[0], src[1], n_ctx_tiles, 0, src[2], _sub_tiles(n_tiles))
        if l % 2 == 0:
            w, w2, b2 = _rec_weights(rec_w_in[j], rec_w_g2[j], rec_b_g2[j])
            p = _rec_in(stream, n_tiles, mods[l], ng, w, w2, b2, rec_lb_logits, j)
            ya = _scan_hgrn(p, rec_gn_a[j].reshape(1, -1), n_ctx // SCAN_BLOCK)
            yb = _scan_gla(p, rec_gn_b[j].reshape(1, -1), n_ctx // SCAN_BLOCK)
            y0, y1, c0, c1, wo = ya, yb, 0, 0, rec_w_out[j]
            y_has_ctx = True
        else:
            q, k2, vt = _qkv(stream, n_tiles, mods[l], ng, _att_weights(att_w_qkv[j]), cos, sin)
            y = _attention(q, k2, vt, att_sink[j], n_ctx)
            y0, y1, c0, c1, wo = y, y, 0, 1, att_w_o[j]
            y_has_ctx = False
        wo = wo.astype(BF16)
        if need_ctx:
            assert y_has_ctx
            xcat = _post(y0, y1, c0, c1, stream, n_tiles, mods[l], ng, wo, win_all, wout_all, l)
            src = (xcat, xcat, n_ctx_tiles)
        else:
            if y_has_ctx:
                y0 = y0[:, :, n_ctx:]
                y1 = y1[:, :, n_ctx:]
            lat_only = _Stream(src[0], src[1], n_ctx_tiles, n_ctx_tiles, src[2], _sub_tiles(n_lat_tiles))
            x_lat = _post(y0, y1, c0, c1, lat_only, n_lat_tiles, mods[l], ng, wo, win_all, wout_all, l)
    return x_lat
```

```python
import functools

import numpy as np
import jax
import jax.numpy as jnp
from jax import lax
from jax.experimental import pallas as pl
from jax.experimental.pallas import tpu as pltpu

F32 = jnp.float32
BF16 = jnp.bfloat16
HIGHEST = lax.Precision.HIGHEST

EPS = 1e-6
LANES = 128
TOKEN_TILE = 256
SCAN_BLOCK = 256
LA_CHUNK = 32
SCAN_GROUP = 4 * LA_CHUNK
ATT_BLOCK = 128
ATT_WINDOW = 128
ATT_QUERY_BLOCKS = 8
ATT_UNIT_HEADS = 2
ATT_AHEAD = (4, 2)
V_ROWS = 80
GRID_W = 64
ROPE_BASE = 10000.0
GLA_GATE_NORM = 16.0
A_HEADS, A_KEY, A_VAL = 4, 128, 128
B_HEADS, B_KEY, B_VAL = 4, 64, 128
B_GATE_RANK = 16
C_HEADS, C_KV_HEADS, C_HEAD_DIM = 16, 4, 64
VMEM_LIMIT = 56 * 1024 * 1024

NT = (((1,), (1,)), ((), ()))
TN = (((0,), (0,)), ((), ()))


def _silu(x):
    h = 0.5 * x
    return h * (jnp.tanh(h) + 1.0)


def _log_sigmoid(x):
    return jnp.minimum(x, 0.0) - jnp.log(1.0 + jnp.exp(-jnp.abs(x)))


def _rms(x, g):
    ms = jnp.mean(x * x, axis=-1, keepdims=True)
    return (x * lax.rsqrt(ms + EPS)) * g


def _modulate(x, g, shift, scale):
    return _rms(x, g * (1.0 + scale)) + shift


def _const_spec(shape):
    nd = len(shape)
    return pl.BlockSpec(shape, lambda *_: (0,) * nd, pipeline_mode=pl.Buffered(1))


class _Stream:
    def __init__(self, ctx_arr, lat_arr, n_ctx_tiles, tile0=0, lat_off=0, nsub=1):
        self.arrays = (ctx_arr, lat_arr)
        self.n_ctx_tiles, self.tile0, self.lat_off, self.nsub = n_ctx_tiles, tile0, lat_off, nsub
        self.d = ctx_arr.shape[-1]

    def specs_for(self, s):
        nct, off, lat_off, nsub = self.n_ctx_tiles, self.tile0 + s, self.lat_off, self.nsub
        return [
            pl.BlockSpec((1, TOKEN_TILE, self.d), lambda b, i: (b, jnp.minimum(i * nsub + off, nct - 1), 0)),
            pl.BlockSpec((1, TOKEN_TILE, self.d),
                         lambda b, i: (b, jnp.maximum(i * nsub + off - nct, 0) + lat_off, 0)),
        ]

    def mod_spec_for(self, s):
        nct, off, nsub = self.n_ctx_tiles, self.tile0 + s, self.nsub
        return pl.BlockSpec((1, 1, 6, self.d), lambda b, i: (b, (i * nsub + off >= nct).astype(jnp.int32), 0, 0))

    def tile(self, ctx_ref, lat_ref, s):
        if self.tile0 >= self.n_ctx_tiles:
            return lat_ref[0]
        g = pl.program_id(1) * self.nsub + s + self.tile0
        return jnp.where(g < self.n_ctx_tiles, ctx_ref[0], lat_ref[0])

    def operands(self, mod):
        specs, args = [], []
        for s in range(self.nsub):
            specs += self.specs_for(s) + [self.mod_spec_for(s)]
            args += [*self.arrays, mod]
        return specs, args


def _sub_tiles(n_tiles):
    return next(n for n in (4, 3, 2, 1) if n_tiles % n == 0)


def _staggered(nsub, stages):
    live = {}
    for step in range(nsub + len(stages) - 1):
        for k in range(len(stages)):
            s = step - k
            if 0 <= s < nsub:
                live[s] = stages[k](s, live.get(s))


def _ada_kernel(c_ref, w_ref, b_ref, o_ref):
    s = _silu(c_ref[...])
    o_ref[0] = jnp.dot(s, w_ref[0], preferred_element_type=F32, precision=HIGHEST) + b_ref[0]


def _ada(cond, ada_w, ada_b):
    depth, d, n = ada_w.shape
    rows = cond.shape[0]
    tn = n // 4
    return pl.pallas_call(
        _ada_kernel,
        grid=(depth, n // tn),
        in_specs=[
            pl.BlockSpec((rows, d), lambda l, j: (0, 0)),
            pl.BlockSpec((1, d, tn), lambda l, j: (l, 0, j)),
            pl.BlockSpec((1, 1, tn), lambda l, j: (l, 0, j)),
        ],
        out_specs=pl.BlockSpec((1, rows, tn), lambda l, j: (l, 0, j)),
        out_shape=jax.ShapeDtypeStruct((depth, rows, n), F32),
        compiler_params=pltpu.CompilerParams(vmem_limit_bytes=VMEM_LIMIT),
        name="ada",
    )(cond, ada_w, ada_b.reshape(depth, 1, n))


def _recin_kernel(*refs, stream, layer_slot):
    nsub = stream.nsub
    ng_ref, w_ref, w2_ref, b2_ref, lb_ref, o_ref = refs[3 * nsub:]
    tm = TOKEN_TILE
    lg = lb_ref[...]
    e = jnp.exp(lg - jnp.max(lg, axis=0, keepdims=True))
    lb = jnp.sum((e / jnp.sum(e, axis=0, keepdims=True))[:layer_slot + 1], axis=0)
    na, nb = A_HEADS * A_KEY, B_HEADS * B_KEY
    n_act = 3 * na + nb

    def modulated(s, _):
        xc_ref, xl_ref, mod_ref = refs[3 * s:3 * s + 3]
        m = mod_ref[0, 0]
        return _modulate(stream.tile(xc_ref, xl_ref, s), ng_ref[0:1], m[0:1], m[1:2]).astype(BF16)

    def project(s, u):
        rows = slice(s * tm, (s + 1) * tm)

        def put(c0, val):
            for j in range(val.shape[1] // LANES):
                o_ref[c0 // LANES + j, rows, :] = val[:, j * LANES:(j + 1) * LANES]

        p = jnp.dot(u, w_ref[:, :n_act + LANES], preferred_element_type=F32)
        put(0, p[:, :n_act])
        pre = jnp.dot(p[:, n_act:].astype(BF16), w2_ref[...], preferred_element_type=F32)
        put(n_act, pre + b2_ref[...])
        put(n_act + 2 * nb, jnp.dot(u, w_ref[:, n_act + LANES:], preferred_element_type=F32))

    def activate(s, _):
        rows = slice(s * tm, (s + 1) * tm)
        for c0 in range(0, n_act + 2 * nb, LANES):
            v = o_ref[c0 // LANES, rows, :]
            if c0 < na:
                v = _silu(v) * (A_KEY ** -0.5)
            elif c0 < 3 * na:
                d, h0 = divmod(c0 - na, na)
                lbd = lb[d:d + 1, h0:h0 + LANES]
                v = jnp.log(0.5 * (1.0 + lbd) + (0.5 * (1.0 - lbd)) * jnp.tanh(0.5 * v))
            elif c0 < n_act:
                v = v * (B_KEY ** -0.5)
            else:
                v = _log_sigmoid(v) * (1.0 / GLA_GATE_NORM)
            o_ref[c0 // LANES, rows, :] = v

    _staggered(nsub, [modulated, project, activate])


def _rec_in(stream, n_tiles, mod, ng, w, w2, b2, lb_logits, layer_slot):
    bsz = mod.shape[0]
    n_out = w.shape[1] - LANES + w2.shape[1]
    rows = stream.nsub * TOKEN_TILE
    assert n_tiles % stream.nsub == 0
    specs, args = stream.operands(mod)
    consts = [ng, w, w2, b2, lb_logits]
    return pl.pallas_call(
        functools.partial(_recin_kernel, stream=stream, layer_slot=layer_slot),
        grid=(bsz, n_tiles // stream.nsub),
        in_specs=specs + [_const_spec(a.shape) for a in consts],
        out_specs=pl.BlockSpec((None, n_out // LANES, rows, LANES), lambda b, i: (b, 0, i, 0)),
        out_shape=jax.ShapeDtypeStruct((bsz, n_out // LANES, n_tiles * TOKEN_TILE, LANES), F32),
        compiler_params=pltpu.CompilerParams(vmem_limit_bytes=VMEM_LIMIT),
        name="rec_in",
    )(*args, *consts)


def _chunk_cumsum(g, reverse):
    r, width = g.shape
    sl = 8
    x = g.reshape(r // sl, sl, width)
    sub = lax.broadcasted_iota(jnp.int32, x.shape, 1)
    for s in (1, 2, 4):
        if reverse:
            x = x + jnp.where(sub < sl - s, pltpu.roll(x, sl - s, 1), 0.0)
        else:
            x = x + jnp.where(sub >= s, pltpu.roll(x, s, 1), 0.0)
    per = LA_CHUNK // sl
    x = x.reshape(r // LA_CHUNK, per, sl, width)
    parts = [None] * per
    carry = None
    for v in (range(per - 1, -1, -1) if reverse else range(per)):
        xv = x[:, v:v + 1]
        parts[v] = xv if carry is None else xv + carry
        edge = xv[:, :, 0:1] if reverse else xv[:, :, sl - 1:sl]
        carry = edge if carry is None else carry + edge
    return jnp.concatenate(parts, axis=1).reshape(r, width)


def _group_masks(reverse):
    n = SCAN_GROUP
    ri = lax.broadcasted_iota(jnp.int32, (n, n), 0)
    ci = lax.broadcasted_iota(jnp.int32, (n, n), 1)
    rc, cc = ri // LA_CHUNK, ci // LA_CHUNK
    if reverse:
        return (rc == cc) & (ci >= ri), (rc + 1 == cc) & (rc % 2 == 0), (rc < 2) & (cc >= 2)
    return (rc == cc) & (ci <= ri), (rc == cc + 1) & (rc % 2 == 1), (rc >= 2) & (cc < 2)


def _scale_chunks(x, factors):
    parts = []
    for c, f in enumerate(factors):
        xc = x[c * LA_CHUNK:(c + 1) * LA_CHUNK]
        parts.append(xc if f is None else xc * f)
    return jnp.concatenate(parts, axis=0)


def _scan_prep(q, k, g, reverse):
    bc = _chunk_cumsum(g, reverse)
    e = jnp.exp(bc)
    return bc, q * e, k / e


def _scan_group(gi, prep, v_bf, states, masks, lane_masks, reverse):
    bc, q_dec, k_inv = prep
    nsub = len(states)
    nch = SCAN_GROUP // LA_CHUNK
    assert nch == 4
    m_diag, m_adj, m_far = masks
    pos = (lambda c: nch - 1 - c) if reverse else (lambda c: c)
    edge = 0 if reverse else LA_CHUNK - 1
    r0 = gi * SCAN_GROUP
    rs = slice(r0, r0 + SCAN_GROUP)
    tot = [bc[r0 + c * LA_CHUNK + edge:r0 + c * LA_CHUNK + edge + 1] for c in range(nch)]
    ts = [tot[pos(s)] for s in range(nch)]
    e_ts = [jnp.exp(t) for t in ts]
    e_pre = [None, e_ts[0], jnp.exp(ts[0] + ts[1]), jnp.exp(ts[0] + ts[1] + ts[2])]
    e_post = [jnp.exp(ts[1] + ts[2] + ts[3]), jnp.exp(ts[2] + ts[3]), e_ts[3], None]
    e_total = jnp.exp(ts[0] + ts[1] + ts[2] + ts[3])
    ki = k_inv[rs]
    ke = _scale_chunks(ki, [e_ts[pos(c)] for c in range(nch)])
    kh = _scale_chunks(ke, [e_post[pos(c)] for c in range(nch)])
    kb = _scale_chunks(ke, [e_ts[1] if pos(c) == 0 else None for c in range(nch)])
    kcat = jnp.concatenate([ki.astype(BF16), ke.astype(BF16)], axis=0)
    kb, kh = kb.astype(BF16), kh.astype(BF16)
    outs, new_states = [], []
    for j in range(nsub):
        qd = q_dec[rs]
        if lane_masks is not None:
            qd = jnp.where(lane_masks[j], qd, 0.0)
        qt = _scale_chunks(qd, [e_pre[pos(c)] for c in range(nch)]).astype(BF16)
        qb = _scale_chunks(qd, [e_ts[2] if pos(c) == 3 else None for c in range(nch)]).astype(BF16)
        qd = qd.astype(BF16)
        v_g = v_bf[j][rs]
        s12 = lax.dot_general(qd, kcat, NT, preferred_element_type=F32)
        s3 = lax.dot_general(qb, kb, NT, preferred_element_type=F32)
        o_state = lax.dot_general(qt, states[j].astype(BF16), NT, preferred_element_type=F32)
        upd = lax.dot_general(v_g, kh, TN, preferred_element_type=F32)
        sc = jnp.where(m_diag, s12[:, :SCAN_GROUP],
                       jnp.where(m_adj, s12[:, SCAN_GROUP:], jnp.where(m_far, s3, 0.0)))
        outs.append(o_state + jnp.dot(sc.astype(BF16), v_g, preferred_element_type=F32))
        new_states.append(states[j] * e_total + upd)
    return outs, tuple(new_states)


def _scan_kernel(*refs, mode, n_ctx_blocks):
    if mode == "hgrn":
        (q_ref, gf_ref, gb_ref, v_ref, og_ref, gn_ref, o_ref, of_ref, ob_ref) = refs
        nsub = 1
    else:
        (q_ref, k_ref, gf_ref, gb_ref, v_ref, og_ref, gn_ref, o_ref, of_ref, ob_ref) = refs
        nsub = 2
    t = q_ref.shape[0]
    blk = SCAN_BLOCK
    nblk = t // blk
    masks_f, masks_b = _group_masks(False), _group_masks(True)
    if nsub == 2:
        lane = lax.broadcasted_iota(jnp.int32, (SCAN_GROUP, LANES), 1)
        lane_masks = [lane < B_KEY, lane >= B_KEY]
    else:
        lane_masks = None

    def features(rows, d):
        g = (gf_ref, gb_ref)[d][rows, :]
        if mode == "hgrn":
            return q_ref[rows, :], 1.0 - jnp.exp(g), g
        return q_ref[rows, :], k_ref[rows, :], g

    ngroup = blk // SCAN_GROUP

    def prep(i):
        jb = n_ctx_blocks - 1 - i if i < n_ctx_blocks else nblk - 1 - (i - n_ctx_blocks)
        rows_f = slice(i * blk, (i + 1) * blk)
        rows_b = slice(jb * blk, (jb + 1) * blk)
        return (jb, _scan_prep(*features(rows_f, 0), False), _scan_prep(*features(rows_b, 1), True),
                [v_ref[j, rows_f, :].astype(BF16) for j in range(nsub)],
                [v_ref[j, rows_b, :].astype(BF16) for j in range(nsub)])

    def finish(blk_i):
        rows = slice(blk_i * blk, (blk_i + 1) * blk)
        for j in range(nsub):
            cols = slice(j * LANES, (j + 1) * LANES)
            o = of_ref[rows, cols] + ob_ref[rows, cols]
            y = _rms(o, gn_ref[...]) * _silu(og_ref[j, rows, :])
            o_ref[j, rows, :] = y.astype(o_ref.dtype)

    st_f = st_b = tuple(jnp.zeros((LANES, LANES), F32) for _ in range(nsub))
    seen = {}
    nxt = prep(0)
    for i in range(nblk):
        jb, prep_f, prep_b, v_f, v_b = nxt
        if i + 1 < nblk:
            nxt = prep(i + 1)
        for s in range(ngroup):
            gf, gb = s, ngroup - 1 - s
            o, st_f = _scan_group(gf, prep_f, v_f, st_f, masks_f, lane_masks, False)
            r0 = i * blk + gf * SCAN_GROUP
            for j in range(nsub):
                of_ref[r0:r0 + SCAN_GROUP, j * LANES:(j + 1) * LANES] = o[j]
            o, st_b = _scan_group(gb, prep_b, v_b, st_b, masks_b, lane_masks, True)
            r0 = jb * blk + gb * SCAN_GROUP
            for j in range(nsub):
                ob_ref[r0:r0 + SCAN_GROUP, j * LANES:(j + 1) * LANES] = o[j]
        for done in (i, jb):
            seen[done] = seen.get(done, 0) + 1
            if seen[done] == 2:
                finish(done)


def _group_spec(t, first, per_step=None):
    if per_step is None:
        return pl.BlockSpec((None, None, t, LANES), lambda b, h: (b, first + h, 0, 0))
    assert first % per_step == 0
    return pl.BlockSpec((None, per_step, t, LANES), lambda b, h: (b, first // per_step + h, 0, 0))


def _scan_hgrn(p, gn, n_ctx_blocks):
    bsz, _, t, _ = p.shape
    one, sub = (lambda first: _group_spec(t, first)), (lambda first: _group_spec(t, first, 1))
    return pl.pallas_call(
        functools.partial(_scan_kernel, mode="hgrn", n_ctx_blocks=n_ctx_blocks),
        grid=(bsz, A_HEADS),
        in_specs=[one(0), one(4), one(8), sub(18), sub(22), _const_spec(gn.shape)],
        out_specs=sub(0),
        out_shape=jax.ShapeDtypeStruct((bsz, A_HEADS, t, A_VAL), BF16),
        scratch_shapes=[pltpu.VMEM((t, LANES), F32), pltpu.VMEM((t, LANES), F32)],
        compiler_params=pltpu.CompilerParams(vmem_limit_bytes=VMEM_LIMIT),
        name="scan_hgrn",
    )(p, p, p, p, p, gn)


def _scan_gla(p, gn, n_ctx_blocks):
    bsz, _, t, _ = p.shape
    one, two = (lambda first: _group_spec(t, first)), (lambda first: _group_spec(t, first, 2))
    return pl.pallas_call(
        functools.partial(_scan_kernel, mode="gla", n_ctx_blocks=n_ctx_blocks),
        grid=(bsz, B_HEADS // 2),
        in_specs=[one(12), one(26), one(14), one(16), two(28), two(32), _const_spec(gn.shape)],
        out_specs=two(0),
        out_shape=jax.ShapeDtypeStruct((bsz, B_HEADS, t, B_VAL), BF16),
        scratch_shapes=[pltpu.VMEM((t, 2 * LANES), F32), pltpu.VMEM((t, 2 * LANES), F32)],
        compiler_params=pltpu.CompilerParams(vmem_limit_bytes=VMEM_LIMIT),
        name="scan_gla",
    )(p, p, p, p, p, p, gn)


def _post_kernel(*refs, stream):
    nsub = stream.nsub
    y_refs = refs[3 * nsub:5 * nsub]
    ng_ref, wo_ref, win_ref, wout_ref, o_ref = refs[5 * nsub:]
    hid = wout_ref.shape[0]
    tm = TOKEN_TILE

    def mix(s, _):
        xc_ref, xl_ref, mod_ref = refs[3 * s:3 * s + 3]
        y0_ref, y1_ref = y_refs[2 * s:2 * s + 2]
        m = mod_ref[0, 0]
        half = wo_ref.shape[0] // 2

        def rows_of(y_ref):
            if len(y_ref.shape) == 2:
                return y_ref[...]
            return jnp.concatenate([y_ref[h] for h in range(y_ref.shape[0])], axis=1)

        t = jnp.dot(rows_of(y0_ref), wo_ref[:half, :], preferred_element_type=F32)
        t = t + jnp.dot(rows_of(y1_ref), wo_ref[half:, :], preferred_element_type=F32)
        x1 = stream.tile(xc_ref, xl_ref, s) + _rms(t, ng_ref[1:2] * m[2:3])
        return x1, _modulate(x1, ng_ref[2:3], m[3:4], m[4:5]).astype(BF16)

    def ffn_in(s, carry):
        x1, u = carry
        h = jnp.dot(u, win_ref[...], preferred_element_type=F32)
        return x1, (_silu(h[:, :hid]) * h[:, hid:]).astype(BF16)

    def ffn_out(s, carry):
        x1, act = carry
        m = refs[3 * s + 2][0, 0]
        h2 = jnp.dot(act, wout_ref[...], preferred_element_type=F32)
        o_ref[0, s * tm:(s + 1) * tm, :] = x1 + _rms(h2, ng_ref[3:4] * m[5:6])

    _staggered(nsub, [mix, ffn_in, ffn_out])


def _layer_spec(stacked, layer):
    nd = stacked.ndim - 1
    return pl.BlockSpec((None,) + stacked.shape[1:], lambda *_: (layer,) + (0,) * nd, pipeline_mode=pl.Buffered(1))


def _post(y0, y1, y_col0, y_col1, stream, n_tiles, mod, ng, wo, win_all, wout_all, layer):
    bsz, d = mod.shape[0], mod.shape[-1]
    tm = TOKEN_TILE
    half = d // 2
    nsub = stream.nsub
    assert n_tiles % nsub == 0
    specs, args = stream.operands(mod)
    def y_spec(y, col, s):
        if y.ndim == 4:
            return pl.BlockSpec((None, y.shape[1], tm, LANES), lambda b, i: (b, 0, i * nsub + s, 0))
        return pl.BlockSpec((None, tm, half), lambda b, i: (b, i * nsub + s, col))

    for s in range(nsub):
        specs += [y_spec(y0, y_col0, s), y_spec(y1, y_col1, s)]
        args += [y0, y1]
    consts = [ng, wo, win_all, wout_all]
    return pl.pallas_call(
        functools.partial(_post_kernel, stream=stream),
        grid=(bsz, n_tiles // nsub),
        in_specs=specs + [_const_spec(ng.shape), _const_spec(wo.shape),
                          _layer_spec(win_all, layer), _layer_spec(wout_all, layer)],
        out_specs=pl.BlockSpec((1, nsub * tm, d), lambda b, i: (b, i, 0)),
        out_shape=jax.ShapeDtypeStruct((bsz, n_tiles * tm, d), F32),
        compiler_params=pltpu.CompilerParams(vmem_limit_bytes=VMEM_LIMIT),
        name="post",
    )(*args, *consts)


def _rope(x, cos, sin_signed, first_half):
    half = C_HEAD_DIM // 2
    partner = jnp.where(first_half, pltpu.roll(x, LANES - half, 1), pltpu.roll(x, half, 1))
    return x * cos + partner * sin_signed


def _qkv_kernel(*refs, stream):
    nsub = stream.nsub
    ng_ref, w_ref, cos_ref, sin_ref, q_ref, k_ref, vt_ref, p_ref = refs[3 * nsub:]
    tm = TOKEN_TILE
    nq, nk = q_ref.shape[2], k_ref.shape[2]
    lane = lax.broadcasted_iota(jnp.int32, (tm, LANES), 1)
    first_half = (lane % C_HEAD_DIM) < (C_HEAD_DIM // 2)

    def modulated(s, _):
        xc_ref, xl_ref, mod_ref = refs[3 * s:3 * s + 3]
        m = mod_ref[0, 0]
        return _modulate(stream.tile(xc_ref, xl_ref, s), ng_ref[0:1], m[0:1], m[1:2]).astype(BF16)

    def project(s, u):
        p_ref[s * tm:(s + 1) * tm, :] = jnp.dot(u, w_ref[...], preferred_element_type=F32)

    def rotate(s, _):
        rows = slice(s * tm, (s + 1) * tm)
        cos, sin = cos_ref[rows, :], sin_ref[rows, :]
        for j in range(nq // LANES):
            cols = slice(j * LANES, (j + 1) * LANES)
            q_ref[0, rows, cols] = (_rope(p_ref[rows, cols], cos, sin, first_half)
                                    * (C_HEAD_DIM ** -0.5)).astype(BF16)
        for j in range(nk // LANES):
            cols = slice(nq + j * LANES, nq + (j + 1) * LANES)
            k_ref[0, rows, j * LANES:(j + 1) * LANES] = _rope(p_ref[rows, cols], cos, sin, first_half).astype(BF16)
        vt_ref[0, :, rows] = p_ref[rows, nq + nk:].T.astype(BF16)

    _staggered(nsub, [modulated, project, rotate])


def _qkv(stream, n_tiles, mod, ng, w, cos, sin):
    bsz = mod.shape[0]
    rows = stream.nsub * TOKEN_TILE
    assert n_tiles % stream.nsub == 0
    t = n_tiles * TOKEN_TILE
    nq = C_HEADS * C_HEAD_DIM
    nk = C_KV_HEADS * C_HEAD_DIM
    specs, args = stream.operands(mod)
    return pl.pallas_call(
        functools.partial(_qkv_kernel, stream=stream),
        grid=(bsz, n_tiles // stream.nsub),
        in_specs=specs + [
            _const_spec(ng.shape), _const_spec(w.shape),
            pl.BlockSpec((rows, LANES), lambda b, i: (i, 0)),
            pl.BlockSpec((rows, LANES), lambda b, i: (i, 0)),
        ],
        out_specs=[
            pl.BlockSpec((1, rows, nq), lambda b, i: (b, i, 0)),
            pl.BlockSpec((1, rows, nk), lambda b, i: (b, i, 0)),
            pl.BlockSpec((1, nk, rows), lambda b, i: (b, 0, i)),
        ],
        out_shape=[jax.ShapeDtypeStruct((bsz, t, nq), BF16),
                   jax.ShapeDtypeStruct((bsz, t, nk), BF16),
                   jax.ShapeDtypeStruct((bsz, nk, t), BF16)],
        scratch_shapes=[pltpu.VMEM((rows, w.shape[1]), F32)],
        compiler_params=pltpu.CompilerParams(vmem_limit_bytes=VMEM_LIMIT),
        name="qkv",
    )(*args, ng, w, cos, sin)


def _attn_kernel(sink_ref, bias_ref, *refs, n_ctx, nsub):
    blk = ATT_BLOCK
    band = 3 * blk
    q_refs = refs[:nsub]
    k_ref, vt_ref, o_ref = refs[nsub:]
    t = k_ref.shape[1]
    group = C_HEADS // C_KV_HEADS
    lane = lax.broadcasted_iota(jnp.int32, (blk, LANES), 1)
    low = lane < C_HEAD_DIM
    zero = jnp.zeros((blk, LANES), BF16)

    def band_start(sb):
        n = pl.program_id(1) * nsub + sb
        return pl.multiple_of(jnp.minimum(n_ctx + (n - 1) * blk, t - band), blk)

    def scores(sb, kg, part):
        a, high = divmod(kg, 2)
        cols = slice(a * LANES, (a + 1) * LANES)
        r_band = band_start(sb)
        parts, sinks = [], []
        for hh in range(part * ATT_UNIT_HEADS, (part + 1) * ATT_UNIT_HEADS):
            pair = q_refs[sb][0, :, (a * group + hh) * LANES:(a * group + hh + 1) * LANES]
            parts.append(jnp.where(~low if high else low, pair, zero))
            sinks.append(jnp.full((1, blk), sink_ref[kg * group + hh], F32))
        qu = jnp.concatenate(parts, axis=0)
        sink = jnp.concatenate(sinks, axis=1)
        kk = jnp.concatenate([k_ref[0, pl.ds(r_band, band), cols], k_ref[0, 0:n_ctx, cols]], axis=0)
        s = lax.dot_general(kk, qu, NT, preferred_element_type=F32)
        bias = jnp.concatenate([bias_ref[sb]] * ATT_UNIT_HEADS, axis=1)
        s = jnp.concatenate([s[:band] + bias, s[band:]], axis=0)
        mx = jnp.maximum(jnp.max(s, axis=0, keepdims=True), sink)
        return s, mx, sink

    def probs(s, mx, sink):
        return jnp.exp((s - mx).astype(BF16)), jnp.exp(sink - mx)

    def finish(sb, kg, part, p, p_sink):
        r_band = band_start(sb)
        rows = slice(kg * C_HEAD_DIM, (kg + 1) * C_HEAD_DIM)
        vt = jnp.concatenate([vt_ref[0, rows, pl.ds(r_band, band)], vt_ref[0, rows, 0:n_ctx]], axis=1)
        vt = jnp.concatenate([vt, jnp.ones((V_ROWS - C_HEAD_DIM, band + n_ctx), BF16)], axis=0)
        o3 = jnp.dot(vt, p, preferred_element_type=F32)
        o = o3[:C_HEAD_DIM] * (1.0 / (o3[C_HEAD_DIM:C_HEAD_DIM + 1] + p_sink))
        for pp in range(ATT_UNIT_HEADS // 2):
            y = jnp.concatenate([o[:, (2 * pp) * blk:(2 * pp + 1) * blk],
                                 o[:, (2 * pp + 1) * blk:(2 * pp + 2) * blk]], axis=0)
            c0 = ((kg * group + part * ATT_UNIT_HEADS) // 2 + pp) * LANES
            o_ref[0, sb * blk:(sb + 1) * blk, c0:c0 + LANES] = y.T.astype(o_ref.dtype)

    units = [(sb, kg, part) for sb in range(nsub) for kg in range(C_KV_HEADS)
             for part in range(group // ATT_UNIT_HEADS)]
    nu = len(units)
    sa, pa = ATT_AHEAD
    s_q, p_q = {}, {}
    for i in range(-sa, nu):
        if 0 <= i + sa < nu:
            s_q[i + sa] = scores(*units[i + sa])
        if 0 <= i + pa < nu:
            p_q[i + pa] = probs(*s_q.pop(i + pa))
        if i >= 0:
            finish(*units[i], *p_q.pop(i))


def _attn_bias(n_blocks, n_ctx, t):
    blk, band = ATT_BLOCK, 3 * ATT_BLOCK
    n = np.arange(n_blocks)[:, None, None]
    r_band = np.minimum(n_ctx + (n - 1) * blk, t - band)
    kp = np.arange(band)[None, :, None] + (r_band - n_ctx)
    qi = np.arange(blk)[None, None, :] + n * blk
    ok = (np.abs(kp - qi) <= ATT_WINDOW) & (kp >= 0)
    return jnp.asarray(np.where(ok, 0.0, -1e30).astype(np.float32))


def _attention(q, k2, vt, sink, n_ctx):
    bsz, t, nq = q.shape
    nk = k2.shape[2]
    n_blocks = (t - n_ctx) // ATT_BLOCK
    assert n_ctx % ATT_BLOCK == 0 and (t - n_ctx) % ATT_BLOCK == 0 and n_ctx >= ATT_BLOCK and n_blocks >= 2
    ctx_blocks = n_ctx // ATT_BLOCK
    nqb = next(n for n in (ATT_QUERY_BLOCKS, 4, 2, 1) if n_blocks % n == 0)
    rows = nqb * ATT_BLOCK
    bias = _attn_bias(n_blocks, n_ctx, t)
    return pl.pallas_call(
        functools.partial(_attn_kernel, n_ctx=n_ctx, nsub=nqb),
        grid=(bsz, n_blocks // nqb),
        in_specs=[
            pl.BlockSpec(memory_space=pltpu.SMEM),
            pl.BlockSpec((nqb,) + bias.shape[1:], lambda b, n: (n, 0, 0)),
        ] + [
            pl.BlockSpec((1, ATT_BLOCK, nq), lambda b, n, s=s: (b, n * nqb + s + ctx_blocks, 0)) for s in range(nqb)
        ] + [
            pl.BlockSpec((1, t, nk), lambda b, n: (b, 0, 0)),
            pl.BlockSpec((1, nk, t), lambda b, n: (b, 0, 0)),
        ],
        out_specs=pl.BlockSpec((1, rows, nq), lambda b, n: (b, n, 0)),
        out_shape=jax.ShapeDtypeStruct((bsz, t - n_ctx, nq), BF16),
        compiler_params=pltpu.CompilerParams(vmem_limit_bytes=VMEM_LIMIT),
        name="attention",
    )(sink, bias, *([q] * nqb), k2, vt)


def _rec_weights(w_in, w_g2, b_g2):
    widths = (A_HEADS * A_KEY,) * 3 + (A_HEADS * A_VAL,) * 2 + (B_HEADS * B_KEY,) * 2 + (
        B_HEADS * B_VAL, B_GATE_RANK, B_GATE_RANK, B_HEADS * B_VAL)
    off = np.concatenate([[0], np.cumsum(widths)])
    seg = lambda i: w_in[:, off[i]:off[i + 1]]
    d = w_in.shape[0]
    pad = jnp.zeros((d, LANES - 2 * B_GATE_RANK), w_in.dtype)
    w = jnp.concatenate([seg(i) for i in (0, 1, 2, 5, 8, 9)] + [pad] + [seg(i) for i in (3, 4, 6, 7, 10)],
                        axis=1).astype(BF16)
    nk = B_HEADS * B_KEY
    w2 = jnp.zeros((LANES, 2 * nk), F32)
    w2 = w2.at[:B_GATE_RANK, :nk].set(w_g2[0]).at[B_GATE_RANK:2 * B_GATE_RANK, nk:].set(w_g2[1])
    return w, w2.astype(BF16), b_g2.reshape(1, 2 * nk)


def _att_weights(w_qkv):
    d = w_qkv.shape[0]
    nq, nkv = C_HEADS * C_HEAD_DIM, C_KV_HEADS * C_HEAD_DIM
    group = C_HEADS // C_KV_HEADS
    perm = np.concatenate([np.arange(0, C_HEAD_DIM, 2), np.arange(1, C_HEAD_DIM, 2)])
    order = [(2 * a + r) * group + i for a in range(C_KV_HEADS // 2) for i in range(group) for r in range(2)]
    wq = w_qkv[:, :nq].reshape(d, C_HEADS, C_HEAD_DIM)[:, order][:, :, perm].reshape(d, nq)
    wk = w_qkv[:, nq:nq + nkv].reshape(d, C_KV_HEADS, C_HEAD_DIM)[:, :, perm].reshape(d, nkv)
    return jnp.concatenate([wq, wk, w_qkv[:, nq + nkv:]], axis=1).astype(BF16)


def _rope_tables(seq, n_ctx):
    n_rows = seq // GRID_W
    row = jnp.repeat(jnp.arange(n_rows), GRID_W).astype(F32)
    col = jnp.tile(jnp.arange(GRID_W), n_rows).astype(F32)
    half = C_HEAD_DIM // 2
    inv = ROPE_BASE ** (-jnp.arange(0, half, 2, dtype=F32) / half)
    ang = jnp.concatenate([row[:, None] * inv, col[:, None] * inv], axis=-1)
    cos, sin = jnp.cos(ang), jnp.sin(ang)
    cos = jnp.concatenate([jnp.ones((n_ctx, half), F32), cos], axis=0)
    sin = jnp.concatenate([jnp.zeros((n_ctx, half), F32), sin], axis=0)
    reps = LANES // C_HEAD_DIM
    return (jnp.tile(jnp.concatenate([cos, cos], axis=1), (1, reps)),
            jnp.tile(jnp.concatenate([-sin, sin], axis=1), (1, reps)))


def kernel(x, c, ctx, c_ctx, ada_w, ada_b, norm_g, rec_w_in, rec_w_out, rec_lb_logits, rec_w_g2, rec_b_g2,
           rec_gn_a, rec_gn_b, att_w_qkv, att_w_o, att_sink, ffn_w_in, ffn_w_out):
    bsz, seq, d = x.shape
    n_ctx = ctx.shape[1]
    depth = ada_w.shape[0]
    tm = TOKEN_TILE
    assert n_ctx % tm == 0 and seq % tm == 0 and n_ctx % SCAN_BLOCK == 0 and seq % SCAN_BLOCK == 0
    n_ctx_tiles = n_ctx // tm
    n_lat_tiles = seq // tm

    rows = -(-(bsz + 1) // 8) * 8
    cond = jnp.concatenate([c, c_ctx[None], jnp.zeros((rows - bsz - 1, d), F32)], axis=0)
    mod_all = _ada(cond, ada_w, ada_b)
    mods = []
    for l in range(depth):
        lat = mod_all[l, :bsz].reshape(bsz, 1, 6, d)
        cx = jnp.broadcast_to(mod_all[l, bsz].reshape(1, 1, 6, d), (bsz, 1, 6, d))
        mods.append(jnp.concatenate([cx, lat], axis=1))

    cos, sin = _rope_tables(seq, n_ctx)
    n_tiles = n_ctx_tiles + n_lat_tiles
    src = (ctx, x, 0)
    win_all, wout_all = ffn_w_in.astype(BF16), ffn_w_out.astype(BF16)

    for l in range(depth):
        need_ctx = l < depth - 1
        j = l // 2
        ng = norm_g[l]
        stream = _Stream(src[0], src[1], n_ctx_tiles, 0, src[2], _sub_tiles(n_tiles))
        if l % 2 == 0:
            w, w2, b2 = _rec_weights(rec_w_in[j], rec_w_g2[j], rec_b_g2[j])
            p = _rec_in(stream, n_tiles, mods[l], ng, w, w2, b2, rec_lb_logits, j)
            ya = _scan_hgrn(p, rec_gn_a[j].reshape(1, -1), n_ctx // SCAN_BLOCK)
            yb = _scan_gla(p, rec_gn_b[j].reshape(1, -1), n_ctx // SCAN_BLOCK)
            y0, y1, c0, c1, wo = ya, yb, 0, 0, rec_w_out[j]
            y_has_ctx = True
        else:
            q, k2, vt = _qkv(stream, n_tiles, mods[l], ng, _att_weights(att_w_qkv[j]), cos, sin)
            y = _attention(q, k2, vt, att_sink[j], n_ctx)
            y0, y1, c0, c1, wo = y, y, 0, 1, att_w_o[j]
            y_has_ctx = False
        wo = wo.astype(BF16)
        if need_ctx:
            assert y_has_ctx
            xcat = _post(y0, y1, c0, c1, stream, n_tiles, mods[l], ng, wo, win_all, wout_all, l)
            src = (xcat, xcat, n_ctx_tiles)
        else:
            if y_has_ctx:
                y0 = y0[:, :, n_ctx:]
                y1 = y1[:, :, n_ctx:]
            lat_only = _Stream(src[0], src[1], n_ctx_tiles, n_ctx_tiles, src[2], _sub_tiles(n_lat_tiles))
            x_lat = _post(y0, y1, c0, c1, lat_only, n_lat_tiles, mods[l], ng, wo, win_all, wout_all, l)
    return x_lat
```

```python
import functools

import numpy as np
import jax
import jax.numpy as jnp
from jax import lax
from jax.experimental import pallas as pl
from jax.experimental.pallas import tpu as pltpu

F32 = jnp.float32
BF16 = jnp.bfloat16
HIGHEST = lax.Precision.HIGHEST

EPS = 1e-6
LANES = 128
TOKEN_TILE = 256
SCAN_BLOCK = 256
LA_CHUNK = 32
SCAN_GROUP = 4 * LA_CHUNK
ATT_BLOCK = 128
ATT_WINDOW = 128
ATT_QUERY_BLOCKS = 16
ATT_UNIT_HEADS = 2
ATT_AHEAD = (4, 2)
V_ROWS = 80
GRID_W = 64
ROPE_BASE = 10000.0
GLA_GATE_NORM = 16.0
A_HEADS, A_KEY, A_VAL = 4, 128, 128
B_HEADS, B_KEY, B_VAL = 4, 64, 128
B_GATE_RANK = 16
C_HEADS, C_KV_HEADS, C_HEAD_DIM = 16, 4, 64
VMEM_LIMIT = 56 * 1024 * 1024

NT = (((1,), (1,)), ((), ()))
TN = (((0,), (0,)), ((), ()))


def _silu(x):
    h = 0.5 * x
    return h * (jnp.tanh(h) + 1.0)


def _log_sigmoid(x):
    return jnp.minimum(x, 0.0) - jnp.log(1.0 + jnp.exp(-jnp.abs(x)))


def _rms(x, g):
    ms = jnp.mean(x * x, axis=-1, keepdims=True)
    return (x * lax.rsqrt(ms + EPS)) * g


def _modulate(x, g, shift, scale):
    return _rms(x, g * (1.0 + scale)) + shift


def _const_spec(shape):
    nd = len(shape)
    return pl.BlockSpec(shape, lambda *_: (0,) * nd, pipeline_mode=pl.Buffered(1))


class _Stream:
    def __init__(self, ctx_arr, lat_arr, n_ctx_tiles, tile0=0, lat_off=0, nsub=1):
        self.arrays = (ctx_arr, lat_arr)
        self.n_ctx_tiles, self.tile0, self.lat_off, self.nsub = n_ctx_tiles, tile0, lat_off, nsub
        self.d = ctx_arr.shape[-1]

    def specs_for(self, s):
        nct, off, lat_off, nsub = self.n_ctx_tiles, self.tile0 + s, self.lat_off, self.nsub
        ctx_rows, ctx_tile = (8, lambda b, i: 0) if self.tile0 >= nct else (
            TOKEN_TILE, lambda b, i: jnp.minimum(i * nsub + off, nct - 1))
        return [
            pl.BlockSpec((1, ctx_rows, self.d), lambda b, i: (b, ctx_tile(b, i), 0)),
            pl.BlockSpec((1, TOKEN_TILE, self.d),
                         lambda b, i: (b, jnp.maximum(i * nsub + off - nct, 0) + lat_off, 0)),
        ]

    def mod_spec_for(self, s):
        nct, off, nsub = self.n_ctx_tiles, self.tile0 + s, self.nsub
        return pl.BlockSpec((1, 1, 6, self.d), lambda b, i: (b, (i * nsub + off >= nct).astype(jnp.int32), 0, 0))

    def tile(self, ctx_ref, lat_ref, s):
        if self.tile0 >= self.n_ctx_tiles:
            return lat_ref[0]
        g = pl.program_id(1) * self.nsub + s + self.tile0
        return jnp.where(g < self.n_ctx_tiles, ctx_ref[0], lat_ref[0])

    def operands(self, mod):
        specs, args = [], []
        for s in range(self.nsub):
            specs += self.specs_for(s) + [self.mod_spec_for(s)]
            args += [*self.arrays, mod]
        return specs, args


def _sub_tiles(n_tiles):
    return next(n for n in (4, 3, 2, 1) if n_tiles % n == 0)


def _staggered(nsub, stages):
    live = {}
    for step in range(nsub + len(stages) - 1):
        for k in range(len(stages)):
            s = step - k
            if 0 <= s < nsub:
                live[s] = stages[k](s, live.get(s))


def _ada_kernel(c_ref, w_ref, b_ref, o_ref):
    s = _silu(c_ref[...])
    o_ref[0] = jnp.dot(s, w_ref[0], preferred_element_type=F32, precision=HIGHEST) + b_ref[0]


def _ada(cond, ada_w, ada_b):
    depth, d, n = ada_w.shape
    rows = cond.shape[0]
    tn = n // 4
    return pl.pallas_call(
        _ada_kernel,
        grid=(depth, n // tn),
        in_specs=[
            pl.BlockSpec((rows, d), lambda l, j: (0, 0)),
            pl.BlockSpec((1, d, tn), lambda l, j: (l, 0, j)),
            pl.BlockSpec((1, 1, tn), lambda l, j: (l, 0, j)),
        ],
        out_specs=pl.BlockSpec((1, rows, tn), lambda l, j: (l, 0, j)),
        out_shape=jax.ShapeDtypeStruct((depth, rows, n), F32),
        compiler_params=pltpu.CompilerParams(vmem_limit_bytes=VMEM_LIMIT),
        name="ada",
    )(cond, ada_w, ada_b.reshape(depth, 1, n))


def _recin_kernel(*refs, stream, layer_slot):
    nsub = stream.nsub
    ng_ref, w_ref, w2_ref, b2_ref, lb_ref, o_ref = refs[3 * nsub:]
    tm = TOKEN_TILE
    lg = lb_ref[...]
    e = jnp.exp(lg - jnp.max(lg, axis=0, keepdims=True))
    lb = jnp.sum((e / jnp.sum(e, axis=0, keepdims=True))[:layer_slot + 1], axis=0)
    na, nb = A_HEADS * A_KEY, B_HEADS * B_KEY
    n_act = 3 * na + nb

    def modulated(s, _):
        xc_ref, xl_ref, mod_ref = refs[3 * s:3 * s + 3]
        m = mod_ref[0, 0]
        return _modulate(stream.tile(xc_ref, xl_ref, s), ng_ref[0:1], m[0:1], m[1:2]).astype(BF16)

    def project(s, u):
        rows = slice(s * tm, (s + 1) * tm)

        def put(c0, val):
            for j in range(val.shape[1] // LANES):
                o_ref[c0 // LANES + j, rows, :] = val[:, j * LANES:(j + 1) * LANES]

        p = jnp.dot(u, w_ref[:, :n_act + LANES], preferred_element_type=F32)
        put(0, p[:, :n_act])
        pre = jnp.dot(p[:, n_act:].astype(BF16), w2_ref[...], preferred_element_type=F32)
        put(n_act, pre + b2_ref[...])
        put(n_act + 2 * nb, jnp.dot(u, w_ref[:, n_act + LANES:], preferred_element_type=F32))

    def activate(s, _):
        rows = slice(s * tm, (s + 1) * tm)
        for c0 in range(0, n_act + 2 * nb, LANES):
            v = o_ref[c0 // LANES, rows, :]
            if c0 < na:
                v = _silu(v) * (A_KEY ** -0.5)
            elif c0 < 3 * na:
                d, h0 = divmod(c0 - na, na)
                lbd = lb[d:d + 1, h0:h0 + LANES]
                v = jnp.log(0.5 * (1.0 + lbd) + (0.5 * (1.0 - lbd)) * jnp.tanh(0.5 * v))
            elif c0 < n_act:
                v = v * (B_KEY ** -0.5)
            else:
                v = _log_sigmoid(v) * (1.0 / GLA_GATE_NORM)
            o_ref[c0 // LANES, rows, :] = v

    _staggered(nsub, [modulated, project, activate])


def _rec_in(stream, n_tiles, mod, ng, w, w2, b2, lb_logits, layer_slot):
    bsz = mod.shape[0]
    n_out = w.shape[1] - LANES + w2.shape[1]
    rows = stream.nsub * TOKEN_TILE
    assert n_tiles % stream.nsub == 0
    specs, args = stream.operands(mod)
    consts = [ng, w, w2, b2, lb_logits]
    return pl.pallas_call(
        functools.partial(_recin_kernel, stream=stream, layer_slot=layer_slot),
        grid=(bsz, n_tiles // stream.nsub),
        in_specs=specs + [_const_spec(a.shape) for a in consts],
        out_specs=pl.BlockSpec((None, n_out // LANES, rows, LANES), lambda b, i: (b, 0, i, 0)),
        out_shape=jax.ShapeDtypeStruct((bsz, n_out // LANES, n_tiles * TOKEN_TILE, LANES), F32),
        compiler_params=pltpu.CompilerParams(vmem_limit_bytes=VMEM_LIMIT),
        name="rec_in",
    )(*args, *consts)


def _chunk_cumsum(g, reverse):
    r, width = g.shape
    sl = 8
    x = g.reshape(r // sl, sl, width)
    sub = lax.broadcasted_iota(jnp.int32, x.shape, 1)
    for s in (1, 2, 4):
        if reverse:
            x = x + jnp.where(sub < sl - s, pltpu.roll(x, sl - s, 1), 0.0)
        else:
            x = x + jnp.where(sub >= s, pltpu.roll(x, s, 1), 0.0)
    per = LA_CHUNK // sl
    x = x.reshape(r // LA_CHUNK, per, sl, width)
    parts = [None] * per
    carry = None
    for v in (range(per - 1, -1, -1) if reverse else range(per)):
        xv = x[:, v:v + 1]
        parts[v] = xv if carry is None else xv + carry
        edge = xv[:, :, 0:1] if reverse else xv[:, :, sl - 1:sl]
        carry = edge if carry is None else carry + edge
    return jnp.concatenate(parts, axis=1).reshape(r, width)


def _group_masks(reverse):
    n = SCAN_GROUP
    ri = lax.broadcasted_iota(jnp.int32, (n, n), 0)
    ci = lax.broadcasted_iota(jnp.int32, (n, n), 1)
    rc, cc = ri // LA_CHUNK, ci // LA_CHUNK
    if reverse:
        return (rc == cc) & (ci >= ri), (rc + 1 == cc) & (rc % 2 == 0), (rc < 2) & (cc >= 2)
    return (rc == cc) & (ci <= ri), (rc == cc + 1) & (rc % 2 == 1), (rc >= 2) & (cc < 2)


def _scale_chunks(x, factors):
    parts = []
    for c, f in enumerate(factors):
        xc = x[c * LA_CHUNK:(c + 1) * LA_CHUNK]
        parts.append(xc if f is None else xc * f)
    return jnp.concatenate(parts, axis=0)


def _scan_prep(q, k, g, reverse):
    bc = _chunk_cumsum(g, reverse)
    e = jnp.exp(bc)
    return bc, q * e, k / e


def _scan_group(gi, prep, v_bf, states, masks, lane_masks, reverse):
    bc, q_dec, k_inv = prep
    nsub = len(states)
    nch = SCAN_GROUP // LA_CHUNK
    assert nch == 4
    m_diag, m_adj, m_far = masks
    pos = (lambda c: nch - 1 - c) if reverse else (lambda c: c)
    edge = 0 if reverse else LA_CHUNK - 1
    r0 = gi * SCAN_GROUP
    rs = slice(r0, r0 + SCAN_GROUP)
    tot = [bc[r0 + c * LA_CHUNK + edge:r0 + c * LA_CHUNK + edge + 1] for c in range(nch)]
    ts = [tot[pos(s)] for s in range(nch)]
    e_ts = [jnp.exp(t) for t in ts]
    e_pre = [None, e_ts[0], jnp.exp(ts[0] + ts[1]), jnp.exp(ts[0] + ts[1] + ts[2])]
    e_post = [jnp.exp(ts[1] + ts[2] + ts[3]), jnp.exp(ts[2] + ts[3]), e_ts[3], None]
    e_total = jnp.exp(ts[0] + ts[1] + ts[2] + ts[3])
    ki = k_inv[rs]
    ke = _scale_chunks(ki, [e_ts[pos(c)] for c in range(nch)])
    kh = _scale_chunks(ke, [e_post[pos(c)] for c in range(nch)])
    kb = _scale_chunks(ke, [e_ts[1] if pos(c) == 0 else None for c in range(nch)])
    kcat = jnp.concatenate([ki.astype(BF16), ke.astype(BF16)], axis=0)
    kb, kh = kb.astype(BF16), kh.astype(BF16)
    outs, new_states = [], []
    for j in range(nsub):
        qd = q_dec[rs]
        if lane_masks is not None:
            qd = jnp.where(lane_masks[j], qd, 0.0)
        qt = _scale_chunks(qd, [e_pre[pos(c)] for c in range(nch)]).astype(BF16)
        qb = _scale_chunks(qd, [e_ts[2] if pos(c) == 3 else None for c in range(nch)]).astype(BF16)
        qd = qd.astype(BF16)
        v_g = v_bf[j][rs]
        s12 = lax.dot_general(qd, kcat, NT, preferred_element_type=F32)
        s3 = lax.dot_general(qb, kb, NT, preferred_element_type=F32)
        o_state = lax.dot_general(qt, states[j].astype(BF16), NT, preferred_element_type=F32)
        upd = lax.dot_general(v_g, kh, TN, preferred_element_type=F32)
        sc = jnp.where(m_diag, s12[:, :SCAN_GROUP],
                       jnp.where(m_adj, s12[:, SCAN_GROUP:], jnp.where(m_far, s3, 0.0)))
        outs.append(o_state + jnp.dot(sc.astype(BF16), v_g, preferred_element_type=F32))
        new_states.append(states[j] * e_total + upd)
    return outs, tuple(new_states)


def _scan_kernel(*refs, mode, n_ctx_blocks):
    if mode == "hgrn":
        (q_ref, gf_ref, gb_ref, v_ref, og_ref, gn_ref, o_ref, of_ref, ob_ref) = refs
        nsub = 1
    else:
        (q_ref, k_ref, gf_ref, gb_ref, v_ref, og_ref, gn_ref, o_ref, of_ref, ob_ref) = refs
        nsub = 2
    t = q_ref.shape[0]
    blk = SCAN_BLOCK
    nblk = t // blk
    masks_f, masks_b = _group_masks(False), _group_masks(True)
    if nsub == 2:
        lane = lax.broadcasted_iota(jnp.int32, (SCAN_GROUP, LANES), 1)
        lane_masks = [lane < B_KEY, lane >= B_KEY]
    else:
        lane_masks = None

    def features(rows, d):
        g = (gf_ref, gb_ref)[d][rows, :]
        if mode == "hgrn":
            return q_ref[rows, :], 1.0 - jnp.exp(g), g
        return q_ref[rows, :], k_ref[rows, :], g

    ngroup = blk // SCAN_GROUP

    def prep(i):
        jb = n_ctx_blocks - 1 - i if i < n_ctx_blocks else nblk - 1 - (i - n_ctx_blocks)
        rows_f = slice(i * blk, (i + 1) * blk)
        rows_b = slice(jb * blk, (jb + 1) * blk)
        return (jb, _scan_prep(*features(rows_f, 0), False), _scan_prep(*features(rows_b, 1), True),
                [v_ref[j, rows_f, :].astype(BF16) for j in range(nsub)],
                [v_ref[j, rows_b, :].astype(BF16) for j in range(nsub)])

    def finish(blk_i):
        rows = slice(blk_i * blk, (blk_i + 1) * blk)
        for j in range(nsub):
            cols = slice(j * LANES, (j + 1) * LANES)
            o = of_ref[rows, cols] + ob_ref[rows, cols]
            y = _rms(o, gn_ref[...]) * _silu(og_ref[j, rows, :])
            o_ref[j, rows, :] = y.astype(o_ref.dtype)

    st_f = st_b = tuple(jnp.zeros((LANES, LANES), F32) for _ in range(nsub))
    seen = {}
    nxt = prep(0)
    for i in range(nblk):
        jb, prep_f, prep_b, v_f, v_b = nxt
        if i + 1 < nblk:
            nxt = prep(i + 1)
        for s in range(ngroup):
            gf, gb = s, ngroup - 1 - s
            o, st_f = _scan_group(gf, prep_f, v_f, st_f, masks_f, lane_masks, False)
            r0 = i * blk + gf * SCAN_GROUP
            for j in range(nsub):
                of_ref[r0:r0 + SCAN_GROUP, j * LANES:(j + 1) * LANES] = o[j]
            o, st_b = _scan_group(gb, prep_b, v_b, st_b, masks_b, lane_masks, True)
            r0 = jb * blk + gb * SCAN_GROUP
            for j in range(nsub):
                ob_ref[r0:r0 + SCAN_GROUP, j * LANES:(j + 1) * LANES] = o[j]
        for done in (i, jb):
            seen[done] = seen.get(done, 0) + 1
            if seen[done] == 2:
                finish(done)


def _group_spec(t, first, per_step=None):
    if per_step is None:
        return pl.BlockSpec((None, None, t, LANES), lambda b, h: (b, first + h, 0, 0))
    assert first % per_step == 0
    return pl.BlockSpec((None, per_step, t, LANES), lambda b, h: (b, first // per_step + h, 0, 0))


def _scan_hgrn(p, gn, n_ctx_blocks):
    bsz, _, t, _ = p.shape
    one, sub = (lambda first: _group_spec(t, first)), (lambda first: _group_spec(t, first, 1))
    return pl.pallas_call(
        functools.partial(_scan_kernel, mode="hgrn", n_ctx_blocks=n_ctx_blocks),
        grid=(bsz, A_HEADS),
        in_specs=[one(0), one(4), one(8), sub(18), sub(22), _const_spec(gn.shape)],
        out_specs=sub(0),
        out_shape=jax.ShapeDtypeStruct((bsz, A_HEADS, t, A_VAL), BF16),
        scratch_shapes=[pltpu.VMEM((t, LANES), F32), pltpu.VMEM((t, LANES), F32)],
        compiler_params=pltpu.CompilerParams(vmem_limit_bytes=VMEM_LIMIT),
        name="scan_hgrn",
    )(p, p, p, p, p, gn)


def _scan_gla(p, gn, n_ctx_blocks):
    bsz, _, t, _ = p.shape
    one, two = (lambda first: _group_spec(t, first)), (lambda first: _group_spec(t, first, 2))
    return pl.pallas_call(
        functools.partial(_scan_kernel, mode="gla", n_ctx_blocks=n_ctx_blocks),
        grid=(bsz, B_HEADS // 2),
        in_specs=[one(12), one(26), one(14), one(16), two(28), two(32), _const_spec(gn.shape)],
        out_specs=two(0),
        out_shape=jax.ShapeDtypeStruct((bsz, B_HEADS, t, B_VAL), BF16),
        scratch_shapes=[pltpu.VMEM((t, 2 * LANES), F32), pltpu.VMEM((t, 2 * LANES), F32)],
        compiler_params=pltpu.CompilerParams(vmem_limit_bytes=VMEM_LIMIT),
        name="scan_gla",
    )(p, p, p, p, p, p, gn)


def _post_kernel(*refs, stream):
    nsub = stream.nsub
    y_refs = refs[3 * nsub:5 * nsub]
    ng_ref, wo_ref, win_ref, wout_ref, o_ref = refs[5 * nsub:]
    hid = wout_ref.shape[0]
    tm = TOKEN_TILE

    def mix(s, _):
        xc_ref, xl_ref, mod_ref = refs[3 * s:3 * s + 3]
        y0_ref, y1_ref = y_refs[2 * s:2 * s + 2]
        m = mod_ref[0, 0]
        half = wo_ref.shape[0] // 2

        def rows_of(y_ref):
            if len(y_ref.shape) == 2:
                return y_ref[...]
            return jnp.concatenate([y_ref[h] for h in range(y_ref.shape[0])], axis=1)

        t = jnp.dot(rows_of(y0_ref), wo_ref[:half, :], preferred_element_type=F32)
        t = t + jnp.dot(rows_of(y1_ref), wo_ref[half:, :], preferred_element_type=F32)
        x1 = stream.tile(xc_ref, xl_ref, s) + _rms(t, ng_ref[1:2] * m[2:3])
        return x1, _modulate(x1, ng_ref[2:3], m[3:4], m[4:5]).astype(BF16)

    def ffn_in(s, carry):
        x1, u = carry
        h = jnp.dot(u, win_ref[...], preferred_element_type=F32)
        return x1, (_silu(h[:, :hid]) * h[:, hid:]).astype(BF16)

    def ffn_out(s, carry):
        x1, act = carry
        m = refs[3 * s + 2][0, 0]
        h2 = jnp.dot(act, wout_ref[...], preferred_element_type=F32)
        o_ref[0, s * tm:(s + 1) * tm, :] = x1 + _rms(h2, ng_ref[3:4] * m[5:6])

    _staggered(nsub, [mix, ffn_in, ffn_out])


def _layer_spec(stacked, layer):
    nd = stacked.ndim - 1
    return pl.BlockSpec((None,) + stacked.shape[1:], lambda *_: (layer,) + (0,) * nd, pipeline_mode=pl.Buffered(1))


def _post(y0, y1, y_col0, y_col1, stream, n_tiles, mod, ng, wo, win_all, wout_all, layer):
    bsz, d = mod.shape[0], mod.shape[-1]
    tm = TOKEN_TILE
    half = d // 2
    nsub = stream.nsub
    assert n_tiles % nsub == 0
    specs, args = stream.operands(mod)
    def y_spec(y, col, s):
        if y.ndim == 4:
            return pl.BlockSpec((None, y.shape[1], tm, LANES), lambda b, i: (b, 0, i * nsub + s, 0))
        return pl.BlockSpec((None, tm, half), lambda b, i: (b, i * nsub + s, col))

    for s in range(nsub):
        specs += [y_spec(y0, y_col0, s), y_spec(y1, y_col1, s)]
        args += [y0, y1]
    consts = [ng, wo, win_all, wout_all]
    return pl.pallas_call(
        functools.partial(_post_kernel, stream=stream),
        grid=(bsz, n_tiles // nsub),
        in_specs=specs + [_const_spec(ng.shape), _const_spec(wo.shape),
                          _layer_spec(win_all, layer), _layer_spec(wout_all, layer)],
        out_specs=pl.BlockSpec((1, nsub * tm, d), lambda b, i: (b, i, 0)),
        out_shape=jax.ShapeDtypeStruct((bsz, n_tiles * tm, d), F32),
        compiler_params=pltpu.CompilerParams(vmem_limit_bytes=VMEM_LIMIT),
        name="post",
    )(*args, *consts)


def _rope(x, cos, sin_signed):
    return x * cos + pltpu.roll(x, LANES // 2, 1) * sin_signed


def _qkv_kernel(*refs, stream):
    nsub = stream.nsub
    ng_ref, w_ref, cos_ref, sin_ref, q_ref, k_ref, vt_ref, p_ref = refs[3 * nsub:]
    tm = TOKEN_TILE
    nq, nk = q_ref.shape[2], k_ref.shape[2]

    def modulated(s, _):
        xc_ref, xl_ref, mod_ref = refs[3 * s:3 * s + 3]
        m = mod_ref[0, 0]
        return _modulate(stream.tile(xc_ref, xl_ref, s), ng_ref[0:1], m[0:1], m[1:2]).astype(BF16)

    def project(s, u):
        p_ref[s * tm:(s + 1) * tm, :] = jnp.dot(u, w_ref[...], preferred_element_type=F32)

    def rotate(s, _):
        rows = slice(s * tm, (s + 1) * tm)
        cos, sin = cos_ref[rows, :], sin_ref[rows, :]
        for j in range(nq // LANES):
            cols = slice(j * LANES, (j + 1) * LANES)
            q_ref[0, rows, cols] = (_rope(p_ref[rows, cols], cos, sin)
                                    * (C_HEAD_DIM ** -0.5)).astype(BF16)
        for j in range(nk // LANES):
            cols = slice(nq + j * LANES, nq + (j + 1) * LANES)
            k_ref[0, rows, j * LANES:(j + 1) * LANES] = _rope(p_ref[rows, cols], cos, sin).astype(BF16)
        vt_ref[0, :, rows] = p_ref[rows, nq + nk:].T.astype(BF16)

    _staggered(nsub, [modulated, project, rotate])


def _qkv(stream, n_tiles, mod, ng, w, cos, sin):
    bsz = mod.shape[0]
    rows = stream.nsub * TOKEN_TILE
    assert n_tiles % stream.nsub == 0
    t = n_tiles * TOKEN_TILE
    nq = C_HEADS * C_HEAD_DIM
    nk = C_KV_HEADS * C_HEAD_DIM
    specs, args = stream.operands(mod)
    return pl.pallas_call(
        functools.partial(_qkv_kernel, stream=stream),
        grid=(bsz, n_tiles // stream.nsub),
        in_specs=specs + [
            _const_spec(ng.shape), _const_spec(w.shape),
            pl.BlockSpec((rows, LANES), lambda b, i: (i, 0)),
            pl.BlockSpec((rows, LANES), lambda b, i: (i, 0)),
        ],
        out_specs=[
            pl.BlockSpec((1, rows, nq), lambda b, i: (b, i, 0)),
            pl.BlockSpec((1, rows, nk), lambda b, i: (b, i, 0)),
            pl.BlockSpec((1, nk, rows), lambda b, i: (b, 0, i)),
        ],
        out_shape=[jax.ShapeDtypeStruct((bsz, t, nq), BF16),
                   jax.ShapeDtypeStruct((bsz, t, nk), BF16),
                   jax.ShapeDtypeStruct((bsz, nk, t), BF16)],
        scratch_shapes=[pltpu.VMEM((rows, w.shape[1]), F32)],
        compiler_params=pltpu.CompilerParams(vmem_limit_bytes=VMEM_LIMIT),
        name="qkv",
    )(*args, ng, w, cos, sin)


def _attn_kernel(sink_ref, bias_ref, *refs, n_ctx, nsub):
    blk = ATT_BLOCK
    band = 3 * blk
    q_refs = refs[:nsub]
    k_ref, vt_ref, o_ref = refs[nsub:]
    t = k_ref.shape[1]
    group = C_HEADS // C_KV_HEADS
    lane = lax.broadcasted_iota(jnp.int32, (blk, LANES), 1)
    low = (lane // (C_HEAD_DIM // 2)) % 2 == 0
    zero = jnp.zeros((blk, LANES), BF16)

    def band_start(sb):
        n = pl.program_id(1) * nsub + sb
        return pl.multiple_of(jnp.minimum(n_ctx + (n - 1) * blk, t - band), blk)

    def scores(sb, kg, part):
        a, high = divmod(kg, 2)
        cols = slice(a * LANES, (a + 1) * LANES)
        r_band = band_start(sb)
        parts, sinks = [], []
        for hh in range(part * ATT_UNIT_HEADS, (part + 1) * ATT_UNIT_HEADS):
            pair = q_refs[sb][0, :, (a * group + hh) * LANES:(a * group + hh + 1) * LANES]
            parts.append(jnp.where(~low if high else low, pair, zero))
            sinks.append(jnp.full((1, blk), sink_ref[kg * group + hh], F32))
        qu = jnp.concatenate(parts, axis=0)
        sink = jnp.concatenate(sinks, axis=1)
        kk = jnp.concatenate([k_ref[0, pl.ds(r_band, band), cols], k_ref[0, 0:n_ctx, cols]], axis=0)
        s = lax.dot_general(kk, qu, NT, preferred_element_type=F32)
        bias = jnp.concatenate([bias_ref[sb]] * ATT_UNIT_HEADS, axis=1)
        s = jnp.concatenate([s[:band] + bias, s[band:]], axis=0)
        mx = jnp.maximum(jnp.max(s, axis=0, keepdims=True), sink)
        return s, mx, sink

    def probs(s, mx, sink):
        return jnp.exp((s - mx).astype(BF16)), jnp.exp(sink - mx)

    def finish(sb, kg, part, p, p_sink):
        r_band = band_start(sb)
        rows = slice(kg * C_HEAD_DIM, (kg + 1) * C_HEAD_DIM)
        vt = jnp.concatenate([vt_ref[0, rows, pl.ds(r_band, band)], vt_ref[0, rows, 0:n_ctx]], axis=1)
        vt = jnp.concatenate([vt, jnp.ones((V_ROWS - C_HEAD_DIM, band + n_ctx), BF16)], axis=0)
        o3 = jnp.dot(vt, p, preferred_element_type=F32)
        o = o3[:C_HEAD_DIM] * (1.0 / (o3[C_HEAD_DIM:C_HEAD_DIM + 1] + p_sink))
        for pp in range(ATT_UNIT_HEADS // 2):
            y = jnp.concatenate([o[:, (2 * pp) * blk:(2 * pp + 1) * blk],
                                 o[:, (2 * pp + 1) * blk:(2 * pp + 2) * blk]], axis=0)
            c0 = ((kg * group + part * ATT_UNIT_HEADS) // 2 + pp) * LANES
            o_ref[0, sb * blk:(sb + 1) * blk, c0:c0 + LANES] = y.T.astype(o_ref.dtype)

    units = [(sb, kg, part) for sb in range(nsub) for kg in range(C_KV_HEADS)
             for part in range(group // ATT_UNIT_HEADS)]
    nu = len(units)
    sa, pa = ATT_AHEAD
    s_q, p_q = {}, {}
    for i in range(-sa, nu):
        if 0 <= i + sa < nu:
            s_q[i + sa] = scores(*units[i + sa])
        if 0 <= i + pa < nu:
            p_q[i + pa] = probs(*s_q.pop(i + pa))
        if i >= 0:
            finish(*units[i], *p_q.pop(i))


def _attn_bias(n_blocks, n_ctx, t):
    blk, band = ATT_BLOCK, 3 * ATT_BLOCK
    n = np.arange(n_blocks)[:, None, None]
    r_band = np.minimum(n_ctx + (n - 1) * blk, t - band)
    kp = np.arange(band)[None, :, None] + (r_band - n_ctx)
    qi = np.arange(blk)[None, None, :] + n * blk
    ok = (np.abs(kp - qi) <= ATT_WINDOW) & (kp >= 0)
    return jnp.asarray(np.where(ok, 0.0, -1e30).astype(np.float32))


def _attention(q, k2, vt, sink, n_ctx):
    bsz, t, nq = q.shape
    nk = k2.shape[2]
    n_blocks = (t - n_ctx) // ATT_BLOCK
    assert n_ctx % ATT_BLOCK == 0 and (t - n_ctx) % ATT_BLOCK == 0 and n_ctx >= ATT_BLOCK and n_blocks >= 2
    ctx_blocks = n_ctx // ATT_BLOCK
    nqb = next(n for n in (ATT_QUERY_BLOCKS, 4, 2, 1) if n_blocks % n == 0)
    rows = nqb * ATT_BLOCK
    bias = _attn_bias(n_blocks, n_ctx, t)
    return pl.pallas_call(
        functools.partial(_attn_kernel, n_ctx=n_ctx, nsub=nqb),
        grid=(bsz, n_blocks // nqb),
        in_specs=[
            pl.BlockSpec(memory_space=pltpu.SMEM),
            pl.BlockSpec((nqb,) + bias.shape[1:], lambda b, n: (n, 0, 0)),
        ] + [
            pl.BlockSpec((1, ATT_BLOCK, nq), lambda b, n, s=s: (b, n * nqb + s + ctx_blocks, 0)) for s in range(nqb)
        ] + [
            pl.BlockSpec((1, t, nk), lambda b, n: (b, 0, 0)),
            pl.BlockSpec((1, nk, t), lambda b, n: (b, 0, 0)),
        ],
        out_specs=pl.BlockSpec((1, rows, nq), lambda b, n: (b, n, 0)),
        out_shape=jax.ShapeDtypeStruct((bsz, t - n_ctx, nq), BF16),
        compiler_params=pltpu.CompilerParams(vmem_limit_bytes=VMEM_LIMIT),
        name="attention",
    )(sink, bias, *([q] * nqb), k2, vt)


def _rec_weights(w_in, w_g2, b_g2):
    widths = (A_HEADS * A_KEY,) * 3 + (A_HEADS * A_VAL,) * 2 + (B_HEADS * B_KEY,) * 2 + (
        B_HEADS * B_VAL, B_GATE_RANK, B_GATE_RANK, B_HEADS * B_VAL)
    off = np.concatenate([[0], np.cumsum(widths)])
    seg = lambda i: w_in[:, off[i]:off[i + 1]]
    d = w_in.shape[0]
    pad = jnp.zeros((d, LANES - 2 * B_GATE_RANK), w_in.dtype)
    w = jnp.concatenate([seg(i) for i in (0, 1, 2, 5, 8, 9)] + [pad] + [seg(i) for i in (3, 4, 6, 7, 10)],
                        axis=1).astype(BF16)
    nk = B_HEADS * B_KEY
    w2 = jnp.zeros((LANES, 2 * nk), F32)
    w2 = w2.at[:B_GATE_RANK, :nk].set(w_g2[0]).at[B_GATE_RANK:2 * B_GATE_RANK, nk:].set(w_g2[1])
    return w, w2.astype(BF16), b_g2.reshape(1, 2 * nk)


def _att_weights(w_qkv):
    d = w_qkv.shape[0]
    nq, nkv = C_HEADS * C_HEAD_DIM, C_KV_HEADS * C_HEAD_DIM
    group = C_HEADS // C_KV_HEADS
    pairs = C_HEAD_DIM // 2
    order = [(2 * a + r) * group + i for a in range(C_KV_HEADS // 2) for i in range(group) for r in range(2)]

    def blocks(w, heads):
        w = w.reshape(d, len(heads), pairs, 2)[:, heads]
        return w.reshape(d, len(heads) // 2, 2, pairs, 2).transpose(0, 1, 4, 2, 3).reshape(d, -1)

    wq = blocks(w_qkv[:, :nq], order)
    wk = blocks(w_qkv[:, nq:nq + nkv], list(range(C_KV_HEADS)))
    return jnp.concatenate([wq, wk, w_qkv[:, nq + nkv:]], axis=1).astype(BF16)


def _rope_tables(seq, n_ctx):
    n_rows = seq // GRID_W
    row = jnp.repeat(jnp.arange(n_rows), GRID_W).astype(F32)
    col = jnp.tile(jnp.arange(GRID_W), n_rows).astype(F32)
    half = C_HEAD_DIM // 2
    inv = ROPE_BASE ** (-jnp.arange(0, half, 2, dtype=F32) / half)
    ang = jnp.concatenate([row[:, None] * inv, col[:, None] * inv], axis=-1)
    cos, sin = jnp.cos(ang), jnp.sin(ang)
    cos = jnp.concatenate([jnp.ones((n_ctx, half), F32), cos], axis=0)
    sin = jnp.concatenate([jnp.zeros((n_ctx, half), F32), sin], axis=0)
    return jnp.tile(cos, (1, 4)), jnp.concatenate([-sin, -sin, sin, sin], axis=1)


def kernel(x, c, ctx, c_ctx, ada_w, ada_b, norm_g, rec_w_in, rec_w_out, rec_lb_logits, rec_w_g2, rec_b_g2,
           rec_gn_a, rec_gn_b, att_w_qkv, att_w_o, att_sink, ffn_w_in, ffn_w_out):
    bsz, seq, d = x.shape
    n_ctx = ctx.shape[1]
    depth = ada_w.shape[0]
    tm = TOKEN_TILE
    assert n_ctx % tm == 0 and seq % tm == 0 and n_ctx % SCAN_BLOCK == 0 and seq % SCAN_BLOCK == 0
    n_ctx_tiles = n_ctx // tm
    n_lat_tiles = seq // tm

    rows = -(-(bsz + 1) // 8) * 8
    cond = jnp.concatenate([c, c_ctx[None], jnp.zeros((rows - bsz - 1, d), F32)], axis=0)
    mod_all = _ada(cond, ada_w, ada_b)
    mods = []
    for l in range(depth):
        lat = mod_all[l, :bsz].reshape(bsz, 1, 6, d)
        cx = jnp.broadcast_to(mod_all[l, bsz].reshape(1, 1, 6, d), (bsz, 1, 6, d))
        mods.append(jnp.concatenate([cx, lat], axis=1))

    cos, sin = _rope_tables(seq, n_ctx)
    n_tiles = n_ctx_tiles + n_lat_tiles
    src = (ctx, x, 0)
    win_all, wout_all = ffn_w_in.astype(BF16), ffn_w_out.astype(BF16)

    for l in range(depth):
        need_ctx = l < depth - 1
        j = l // 2
        ng = norm_g[l]
        stream = _Stream(src[0], src[1], n_ctx_tiles, 0, src[2], _sub_tiles(n_tiles))
        if l % 2 == 0:
            w, w2, b2 = _rec_weights(rec_w_in[j], rec_w_g2[j], rec_b_g2[j])
            p = _rec_in(stream, n_tiles, mods[l], ng, w, w2, b2, rec_lb_logits, j)
            ya = _scan_hgrn(p, rec_gn_a[j].reshape(1, -1), n_ctx // SCAN_BLOCK)
            yb = _scan_gla(p, rec_gn_b[j].reshape(1, -1), n_ctx // SCAN_BLOCK)
            y0, y1, c0, c1, wo = ya, yb, 0, 0, rec_w_out[j]
            y_has_ctx = True
        else:
            q, k2, vt = _qkv(stream, n_tiles, mods[l], ng, _att_weights(att_w_qkv[j]), cos, sin)
            y = _attention(q, k2, vt, att_sink[j], n_ctx)
            y0, y1, c0, c1, wo = y, y, 0, 1, att_w_o[j]
            y_has_ctx = False
        wo = wo.astype(BF16)
        if need_ctx:
            assert y_has_ctx
            xcat = _post(y0, y1, c0, c1, stream, n_tiles, mods[l], ng, wo, win_all, wout_all, l)
            src = (xcat, xcat, n_ctx_tiles)
        else:
            if y_has_ctx:
                y0 = y0[:, :, n_ctx:]
                y1 = y1[:, :, n_ctx:]
            lat_only = _Stream(src[0], src[1], n_ctx_tiles, n_ctx_tiles, src[2], _sub_tiles(n_lat_tiles))
            x_lat = _post(y0, y1, c0, c1, lat_only, n_lat_tiles, mods[l], ng, wo, win_all, wout_all, l)
    return x_lat
```

```python
import functools

import numpy as np
import jax
import jax.numpy as jnp
from jax import lax
from jax.experimental import pallas as pl
from jax.experimental.pallas import tpu as pltpu

F32 = jnp.float32
BF16 = jnp.bfloat16

EPS = 1e-6
LANES = 128
TOKEN_TILE = 256
SCAN_BLOCK = 256
SCAN_SLOTS = 2
LA_CHUNK = 32
SCAN_GROUP = 4 * LA_CHUNK
ATT_BLOCK = 128
ATT_WINDOW = 128
ATT_QUERY_BLOCKS = 16
ATT_UNIT_HEADS = 2
ATT_AHEAD = (4, 2)
V_ROWS = 80
GRID_W = 64
ROPE_BASE = 10000.0
GLA_GATE_NORM = 16.0
A_HEADS, A_KEY, A_VAL = 4, 128, 128
B_HEADS, B_KEY, B_VAL = 4, 64, 128
B_GATE_RANK = 16
C_HEADS, C_KV_HEADS, C_HEAD_DIM = 16, 4, 64
VMEM_LIMIT = 56 * 1024 * 1024

NT = (((1,), (1,)), ((), ()))
TN = (((0,), (0,)), ((), ()))


def _silu(x):
    h = 0.5 * x
    return h * (jnp.tanh(h) + 1.0)


def _log_sigmoid(x):
    return jnp.minimum(x, 0.0) - jnp.log(1.0 + jnp.exp(-jnp.abs(x)))


def _rms(x, g):
    ms = jnp.mean(x * x, axis=-1, keepdims=True)
    return (x * lax.rsqrt(ms + EPS)) * g


def _modulate(x, g, shift, scale):
    return _rms(x, g * (1.0 + scale)) + shift


def _const_spec(shape):
    nd = len(shape)
    return pl.BlockSpec(shape, lambda *_: (0,) * nd, pipeline_mode=pl.Buffered(1))


class _Stream:
    def __init__(self, ctx_arr, lat_arr, n_ctx_tiles, tile0=0, lat_off=0, nsub=1):
        self.arrays = (ctx_arr, lat_arr)
        self.n_ctx_tiles, self.tile0, self.lat_off, self.nsub = n_ctx_tiles, tile0, lat_off, nsub
        self.d = ctx_arr.shape[-1]

    def specs_for(self, s):
        nct, off, lat_off, nsub = self.n_ctx_tiles, self.tile0 + s, self.lat_off, self.nsub
        ctx_rows, ctx_tile = (8, lambda b, i: 0) if self.tile0 >= nct else (
            TOKEN_TILE, lambda b, i: jnp.minimum(i * nsub + off, nct - 1))
        return [
            pl.BlockSpec((1, ctx_rows, self.d), lambda b, i: (b, ctx_tile(b, i), 0)),
            pl.BlockSpec((1, TOKEN_TILE, self.d),
                         lambda b, i: (b, jnp.maximum(i * nsub + off - nct, 0) + lat_off, 0)),
        ]

    def mod_spec_for(self, s):
        nct, off, nsub = self.n_ctx_tiles, self.tile0 + s, self.nsub
        return pl.BlockSpec((1, 1, 6, self.d), lambda b, i: (b, (i * nsub + off >= nct).astype(jnp.int32), 0, 0))

    def tile(self, ctx_ref, lat_ref, s):
        if self.tile0 >= self.n_ctx_tiles:
            return lat_ref[0]
        g = pl.program_id(1) * self.nsub + s + self.tile0
        return jnp.where(g < self.n_ctx_tiles, ctx_ref[0], lat_ref[0])

    def operands(self, mod):
        specs, args = [], []
        for s in range(self.nsub):
            specs += self.specs_for(s) + [self.mod_spec_for(s)]
            args += [*self.arrays, mod]
        return specs, args


def _sub_tiles(n_tiles):
    return next(n for n in (4, 3, 2, 1) if n_tiles % n == 0)


def _staggered(nsub, stages):
    live = {}
    for step in range(nsub + len(stages) - 1):
        for k in range(len(stages)):
            s = step - k
            if 0 <= s < nsub:
                live[s] = stages[k](s, live.get(s))


def _ada_kernel(c_ref, w_ref, b_ref, o_ref):
    s = _silu(c_ref[...])
    o_ref[0] = jnp.dot(s.astype(BF16), w_ref[0].astype(BF16), preferred_element_type=F32) + b_ref[0]


def _ada(cond, ada_w, ada_b):
    depth, d, n = ada_w.shape
    rows = cond.shape[0]
    tn = n // 4
    return pl.pallas_call(
        _ada_kernel,
        grid=(depth, n // tn),
        in_specs=[
            pl.BlockSpec((rows, d), lambda l, j: (0, 0)),
            pl.BlockSpec((1, d, tn), lambda l, j: (l, 0, j)),
            pl.BlockSpec((1, 1, tn), lambda l, j: (l, 0, j)),
        ],
        out_specs=pl.BlockSpec((1, rows, tn), lambda l, j: (l, 0, j)),
        out_shape=jax.ShapeDtypeStruct((depth, rows, n), F32),
        compiler_params=pltpu.CompilerParams(vmem_limit_bytes=VMEM_LIMIT),
        name="ada",
    )(cond, ada_w, ada_b.reshape(depth, 1, n))


def _recin_kernel(*refs, stream, layer_slot):
    nsub = stream.nsub
    ng_ref, w_ref, w2_ref, b2_ref, lb_ref, o_ref = refs[3 * nsub:]
    tm = TOKEN_TILE
    lg = lb_ref[...]
    e = jnp.exp(lg - jnp.max(lg, axis=0, keepdims=True))
    lb = jnp.sum((e / jnp.sum(e, axis=0, keepdims=True))[:layer_slot + 1], axis=0)
    na, nb = A_HEADS * A_KEY, B_HEADS * B_KEY
    n_act = 3 * na + nb

    def modulated(s, _):
        xc_ref, xl_ref, mod_ref = refs[3 * s:3 * s + 3]
        m = mod_ref[0, 0]
        return _modulate(stream.tile(xc_ref, xl_ref, s), ng_ref[0:1], m[0:1], m[1:2]).astype(BF16)

    def project(s, u):
        rows = slice(s * tm, (s + 1) * tm)

        def put(c0, val):
            for j in range(val.shape[1] // LANES):
                o_ref[c0 // LANES + j, rows, :] = val[:, j * LANES:(j + 1) * LANES]

        p = jnp.dot(u, w_ref[:, :n_act + LANES], preferred_element_type=F32)
        put(0, p[:, :n_act])
        pre = jnp.dot(p[:, n_act:].astype(BF16), w2_ref[...], preferred_element_type=F32)
        put(n_act, pre + b2_ref[...])
        put(n_act + 2 * nb, jnp.dot(u, w_ref[:, n_act + LANES:], preferred_element_type=F32))

    def activate(s, _):
        rows = slice(s * tm, (s + 1) * tm)
        for c0 in range(0, n_act + 2 * nb, LANES):
            v = o_ref[c0 // LANES, rows, :]
            if c0 < na:
                v = _silu(v) * (A_KEY ** -0.5)
            elif c0 < 3 * na:
                d, h0 = divmod(c0 - na, na)
                lbd = lb[d:d + 1, h0:h0 + LANES]
                v = jnp.log(0.5 * (1.0 + lbd) + (0.5 * (1.0 - lbd)) * jnp.tanh(0.5 * v))
            elif c0 < n_act:
                v = v * (B_KEY ** -0.5)
            else:
                v = _log_sigmoid(v) * (1.0 / GLA_GATE_NORM)
            o_ref[c0 // LANES, rows, :] = v

    _staggered(nsub, [modulated, project, activate])


def _rec_in(stream, n_tiles, mod, ng, w, w2, b2, lb_logits, layer_slot):
    bsz = mod.shape[0]
    n_out = w.shape[1] - LANES + w2.shape[1]
    rows = stream.nsub * TOKEN_TILE
    assert n_tiles % stream.nsub == 0
    specs, args = stream.operands(mod)
    consts = [ng, w, w2, b2, lb_logits]
    return pl.pallas_call(
        functools.partial(_recin_kernel, stream=stream, layer_slot=layer_slot),
        grid=(bsz, n_tiles // stream.nsub),
        in_specs=specs + [_const_spec(a.shape) for a in consts],
        out_specs=pl.BlockSpec((None, n_out // LANES, rows, LANES), lambda b, i: (b, 0, i, 0)),
        out_shape=jax.ShapeDtypeStruct((bsz, n_out // LANES, n_tiles * TOKEN_TILE, LANES), F32),
        compiler_params=pltpu.CompilerParams(vmem_limit_bytes=VMEM_LIMIT),
        name="rec_in",
    )(*args, *consts)


def _chunk_cumsum(g, reverse):
    r, width = g.shape
    sl = 8
    x = g.reshape(r // sl, sl, width)
    sub = lax.broadcasted_iota(jnp.int32, x.shape, 1)
    for s in (1, 2, 4):
        if reverse:
            x = x + jnp.where(sub < sl - s, pltpu.roll(x, sl - s, 1), 0.0)
        else:
            x = x + jnp.where(sub >= s, pltpu.roll(x, s, 1), 0.0)
    per = LA_CHUNK // sl
    x = x.reshape(r // LA_CHUNK, per, sl, width)
    parts = [None] * per
    carry = None
    for v in (range(per - 1, -1, -1) if reverse else range(per)):
        xv = x[:, v:v + 1]
        parts[v] = xv if carry is None else xv + carry
        edge = xv[:, :, 0:1] if reverse else xv[:, :, sl - 1:sl]
        carry = edge if carry is None else carry + edge
    return jnp.concatenate(parts, axis=1).reshape(r, width)


def _group_masks(reverse):
    n = SCAN_GROUP
    ri = lax.broadcasted_iota(jnp.int32, (n, n), 0)
    ci = lax.broadcasted_iota(jnp.int32, (n, n), 1)
    rc, cc = ri // LA_CHUNK, ci // LA_CHUNK
    if reverse:
        return (rc == cc) & (ci >= ri), (rc + 1 == cc) & (rc % 2 == 0), (rc < 2) & (cc >= 2)
    return (rc == cc) & (ci <= ri), (rc == cc + 1) & (rc % 2 == 1), (rc >= 2) & (cc < 2)


def _scale_chunks(x, factors):
    parts = []
    for c, f in enumerate(factors):
        xc = x[c * LA_CHUNK:(c + 1) * LA_CHUNK]
        parts.append(xc if f is None else xc * f)
    return jnp.concatenate(parts, axis=0)


def _scan_prep(q, k, g, reverse):
    bc = _chunk_cumsum(g, reverse)
    e = jnp.exp(bc)
    return bc, q * e, k / e


def _scan_group(gi, prep, v_bf, states, masks, lane_masks, reverse):
    bc, q_dec, k_inv = prep
    nsub = len(states)
    nch = SCAN_GROUP // LA_CHUNK
    assert nch == 4
    m_diag, m_adj, m_far = masks
    pos = (lambda c: nch - 1 - c) if reverse else (lambda c: c)
    edge = 0 if reverse else LA_CHUNK - 1
    r0 = gi * SCAN_GROUP
    rs = slice(r0, r0 + SCAN_GROUP)
    tot = [bc[r0 + c * LA_CHUNK + edge:r0 + c * LA_CHUNK + edge + 1] for c in range(nch)]
    ts = [tot[pos(s)] for s in range(nch)]
    e_ts = [jnp.exp(t) for t in ts]
    e_pre = [None, e_ts[0], jnp.exp(ts[0] + ts[1]), jnp.exp(ts[0] + ts[1] + ts[2])]
    e_post = [jnp.exp(ts[1] + ts[2] + ts[3]), jnp.exp(ts[2] + ts[3]), e_ts[3], None]
    e_total = jnp.exp(ts[0] + ts[1] + ts[2] + ts[3])
    ki = k_inv[rs]
    ke = _scale_chunks(ki, [e_ts[pos(c)] for c in range(nch)])
    kh = _scale_chunks(ke, [e_post[pos(c)] for c in range(nch)])
    kb = _scale_chunks(ke, [e_ts[1] if pos(c) == 0 else None for c in range(nch)])
    kcat = jnp.concatenate([ki.astype(BF16), ke.astype(BF16)], axis=0)
    kb, kh = kb.astype(BF16), kh.astype(BF16)
    outs, new_states = [], []
    for j in range(nsub):
        qd = q_dec[rs]
        if lane_masks is not None:
            qd = jnp.where(lane_masks[j], qd, 0.0)
        qt = _scale_chunks(qd, [e_pre[pos(c)] for c in range(nch)]).astype(BF16)
        qb = _scale_chunks(qd, [e_ts[2] if pos(c) == 3 else None for c in range(nch)]).astype(BF16)
        qd = qd.astype(BF16)
        v_g = v_bf[j][rs]
        s12 = lax.dot_general(qd, kcat, NT, preferred_element_type=F32)
        s3 = lax.dot_general(qb, kb, NT, preferred_element_type=F32)
        o_state = lax.dot_general(qt, states[j].astype(BF16), NT, preferred_element_type=F32)
        upd = lax.dot_general(v_g, kh, TN, preferred_element_type=F32)
        sc = jnp.where(m_diag, s12[:, :SCAN_GROUP],
                       jnp.where(m_adj, s12[:, SCAN_GROUP:], jnp.where(m_far, s3, 0.0)))
        outs.append(o_state + jnp.dot(sc.astype(BF16), v_g, preferred_element_type=F32))
        new_states.append(states[j] * e_total + upd)
    return outs, tuple(new_states)


def _scan_kernel(*refs, mode, n_ctx_blocks, nslot):
    if mode == "hgrn":
        (q_ref, gf_ref, gb_ref, v_ref, og_ref, gn_ref, o_ref, of_ref, ob_ref) = refs
        nsub = 1
    else:
        (q_ref, k_ref, gf_ref, gb_ref, v_ref, og_ref, gn_ref, o_ref, of_ref, ob_ref) = refs
        nsub = 2
    t = q_ref.shape[1]
    blk = SCAN_BLOCK
    nblk = t // blk
    masks_f, masks_b = _group_masks(False), _group_masks(True)
    if nsub == 2:
        lane = lax.broadcasted_iota(jnp.int32, (SCAN_GROUP, LANES), 1)
        lane_masks = [lane < B_KEY, lane >= B_KEY]
    else:
        lane_masks = None

    def features(slot, rows, d):
        g = (gf_ref, gb_ref)[d][slot, rows, :]
        if mode == "hgrn":
            return q_ref[slot, rows, :], 1.0 - jnp.exp(g), g
        return q_ref[slot, rows, :], k_ref[slot, rows, :], g

    ngroup = blk // SCAN_GROUP

    def prep(i):
        jb = n_ctx_blocks - 1 - i if i < n_ctx_blocks else nblk - 1 - (i - n_ctx_blocks)
        rows_f = slice(i * blk, (i + 1) * blk)
        rows_b = slice(jb * blk, (jb + 1) * blk)
        per_slot = []
        for slot in range(nslot):
            per_slot.append((_scan_prep(*features(slot, rows_f, 0), False),
                             _scan_prep(*features(slot, rows_b, 1), True),
                             [v_ref[slot * nsub + j, rows_f, :].astype(BF16) for j in range(nsub)],
                             [v_ref[slot * nsub + j, rows_b, :].astype(BF16) for j in range(nsub)]))
        return jb, per_slot

    def finish(blk_i):
        rows = slice(blk_i * blk, (blk_i + 1) * blk)
        for hj in range(nslot * nsub):
            cols = slice(hj * LANES, (hj + 1) * LANES)
            o = of_ref[rows, cols] + ob_ref[rows, cols]
            y = _rms(o, gn_ref[...]) * _silu(og_ref[hj, rows, :])
            o_ref[hj, rows, :] = y.astype(o_ref.dtype)

    zero = tuple(jnp.zeros((LANES, LANES), F32) for _ in range(nsub))
    st_f, st_b = [zero] * nslot, [zero] * nslot
    seen = {}
    nxt = prep(0)
    for i in range(nblk):
        jb, per_slot = nxt
        if i + 1 < nblk:
            nxt = prep(i + 1)
        for s in range(ngroup):
            gf, gb = s, ngroup - 1 - s
            for slot in range(nslot):
                prep_f, prep_b, v_f, v_b = per_slot[slot]
                o, st_f[slot] = _scan_group(gf, prep_f, v_f, st_f[slot], masks_f, lane_masks, False)
                r0 = i * blk + gf * SCAN_GROUP
                for j in range(nsub):
                    c0 = (slot * nsub + j) * LANES
                    of_ref[r0:r0 + SCAN_GROUP, c0:c0 + LANES] = o[j]
                o, st_b[slot] = _scan_group(gb, prep_b, v_b, st_b[slot], masks_b, lane_masks, True)
                r0 = jb * blk + gb * SCAN_GROUP
                for j in range(nsub):
                    c0 = (slot * nsub + j) * LANES
                    ob_ref[r0:r0 + SCAN_GROUP, c0:c0 + LANES] = o[j]
        for done in (i, jb):
            seen[done] = seen.get(done, 0) + 1
            if seen[done] == 2:
                finish(done)


def _group_spec(t, first, per_step):
    assert first % per_step == 0
    return pl.BlockSpec((None, per_step, t, LANES), lambda b, h: (b, first // per_step + h, 0, 0))


def _scan_hgrn(p, gn, n_ctx_blocks):
    bsz, _, t, _ = p.shape
    ns = SCAN_SLOTS
    spec = lambda first: _group_spec(t, first, ns)
    return pl.pallas_call(
        functools.partial(_scan_kernel, mode="hgrn", n_ctx_blocks=n_ctx_blocks, nslot=ns),
        grid=(bsz, A_HEADS // ns),
        in_specs=[spec(0), spec(4), spec(8), spec(18), spec(22), _const_spec(gn.shape)],
        out_specs=spec(0),
        out_shape=jax.ShapeDtypeStruct((bsz, A_HEADS, t, A_VAL), BF16),
        scratch_shapes=[pltpu.VMEM((t, ns * LANES), F32), pltpu.VMEM((t, ns * LANES), F32)],
        compiler_params=pltpu.CompilerParams(vmem_limit_bytes=VMEM_LIMIT),
        name="scan_hgrn",
    )(p, p, p, p, p, gn)


def _scan_gla(p, gn, n_ctx_blocks):
    bsz, _, t, _ = p.shape
    ns = SCAN_SLOTS
    one, two = (lambda first: _group_spec(t, first, ns)), (lambda first: _group_spec(t, first, 2 * ns))
    return pl.pallas_call(
        functools.partial(_scan_kernel, mode="gla", n_ctx_blocks=n_ctx_blocks, nslot=ns),
        grid=(bsz, B_HEADS // 2 // ns),
        in_specs=[one(12), one(26), one(14), one(16), two(28), two(32), _const_spec(gn.shape)],
        out_specs=two(0),
        out_shape=jax.ShapeDtypeStruct((bsz, B_HEADS, t, B_VAL), BF16),
        scratch_shapes=[pltpu.VMEM((t, 2 * ns * LANES), F32), pltpu.VMEM((t, 2 * ns * LANES), F32)],
        compiler_params=pltpu.CompilerParams(vmem_limit_bytes=VMEM_LIMIT),
        name="scan_gla",
    )(p, p, p, p, p, p, gn)


def _post_kernel(*refs, stream):
    nsub = stream.nsub
    y_refs = refs[3 * nsub:5 * nsub]
    ng_ref, wo_ref, win_ref, wout_ref, o_ref = refs[5 * nsub:]
    hid = wout_ref.shape[0]
    tm = TOKEN_TILE

    def mix(s, _):
        xc_ref, xl_ref, mod_ref = refs[3 * s:3 * s + 3]
        y0_ref, y1_ref = y_refs[2 * s:2 * s + 2]
        m = mod_ref[0, 0]
        half = wo_ref.shape[0] // 2

        def rows_of(y_ref):
            if len(y_ref.shape) == 2:
                return y_ref[...]
            return jnp.concatenate([y_ref[h] for h in range(y_ref.shape[0])], axis=1)

        t = jnp.dot(rows_of(y0_ref), wo_ref[:half, :], preferred_element_type=F32)
        t = t + jnp.dot(rows_of(y1_ref), wo_ref[half:, :], preferred_element_type=F32)
        x1 = stream.tile(xc_ref, xl_ref, s) + _rms(t, ng_ref[1:2] * m[2:3])
        return x1, _modulate(x1, ng_ref[2:3], m[3:4], m[4:5]).astype(BF16)

    def ffn_in(s, carry):
        x1, u = carry
        h = jnp.dot(u, win_ref[...], preferred_element_type=F32)
        return x1, (_silu(h[:, :hid]) * h[:, hid:]).astype(BF16)

    def ffn_out(s, carry):
        x1, act = carry
        m = refs[3 * s + 2][0, 0]
        h2 = jnp.dot(act, wout_ref[...], preferred_element_type=F32)
        o_ref[0, s * tm:(s + 1) * tm, :] = x1 + _rms(h2, ng_ref[3:4] * m[5:6])

    _staggered(nsub, [mix, ffn_in, ffn_out])


def _layer_spec(stacked, layer):
    nd = stacked.ndim - 1
    return pl.BlockSpec((None,) + stacked.shape[1:], lambda *_: (layer,) + (0,) * nd, pipeline_mode=pl.Buffered(1))


def _post(y0, y1, y_col0, y_col1, stream, n_tiles, mod, ng, wo, win_all, wout_all, layer):
    bsz, d = mod.shape[0], mod.shape[-1]
    tm = TOKEN_TILE
    half = d // 2
    nsub = stream.nsub
    assert n_tiles % nsub == 0
    specs, args = stream.operands(mod)
    def y_spec(y, col, s):
        if y.ndim == 4:
            return pl.BlockSpec((None, y.shape[1], tm, LANES), lambda b, i: (b, 0, i * nsub + s, 0))
        return pl.BlockSpec((None, tm, half), lambda b, i: (b, i * nsub + s, col))

    for s in range(nsub):
        specs += [y_spec(y0, y_col0, s), y_spec(y1, y_col1, s)]
        args += [y0, y1]
    consts = [ng, wo, win_all, wout_all]
    return pl.pallas_call(
        functools.partial(_post_kernel, stream=stream),
        grid=(bsz, n_tiles // nsub),
        in_specs=specs + [_const_spec(ng.shape), _const_spec(wo.shape),
                          _layer_spec(win_all, layer), _layer_spec(wout_all, layer)],
        out_specs=pl.BlockSpec((1, nsub * tm, d), lambda b, i: (b, i, 0)),
        out_shape=jax.ShapeDtypeStruct((bsz, n_tiles * tm, d), F32),
        compiler_params=pltpu.CompilerParams(vmem_limit_bytes=VMEM_LIMIT),
        name="post",
    )(*args, *consts)


def _rope(x, cos, sin_signed):
    return x * cos + pltpu.roll(x, LANES // 2, 1) * sin_signed


def _qkv_kernel(*refs, stream):
    nsub = stream.nsub
    ng_ref, w_ref, cos_ref, sin_ref, q_ref, k_ref, vt_ref, p_ref = refs[3 * nsub:]
    tm = TOKEN_TILE
    nq, nk = q_ref.shape[2], k_ref.shape[2]

    def modulated(s, _):
        xc_ref, xl_ref, mod_ref = refs[3 * s:3 * s + 3]
        m = mod_ref[0, 0]
        return _modulate(stream.tile(xc_ref, xl_ref, s), ng_ref[0:1], m[0:1], m[1:2]).astype(BF16)

    def project(s, u):
        p_ref[s * tm:(s + 1) * tm, :] = jnp.dot(u, w_ref[...], preferred_element_type=F32)

    def rotate(s, _):
        rows = slice(s * tm, (s + 1) * tm)
        cos, sin = cos_ref[rows, :], sin_ref[rows, :]
        for j in range(nq // LANES):
            cols = slice(j * LANES, (j + 1) * LANES)
            q_ref[0, rows, cols] = (_rope(p_ref[rows, cols], cos, sin)
                                    * (C_HEAD_DIM ** -0.5)).astype(BF16)
        for j in range(nk // LANES):
            cols = slice(nq + j * LANES, nq + (j + 1) * LANES)
            k_ref[0, rows, j * LANES:(j + 1) * LANES] = _rope(p_ref[rows, cols], cos, sin).astype(BF16)
        vt_ref[0, :, rows] = p_ref[rows, nq + nk:].T.astype(BF16)

    _staggered(nsub, [modulated, project, rotate])


def _qkv(stream, n_tiles, mod, ng, w, cos, sin):
    bsz = mod.shape[0]
    rows = stream.nsub * TOKEN_TILE
    assert n_tiles % stream.nsub == 0
    t = n_tiles * TOKEN_TILE
    nq = C_HEADS * C_HEAD_DIM
    nk = C_KV_HEADS * C_HEAD_DIM
    specs, args = stream.operands(mod)
    return pl.pallas_call(
        functools.partial(_qkv_kernel, stream=stream),
        grid=(bsz, n_tiles // stream.nsub),
        in_specs=specs + [
            _const_spec(ng.shape), _const_spec(w.shape),
            pl.BlockSpec((rows, LANES), lambda b, i: (i, 0)),
            pl.BlockSpec((rows, LANES), lambda b, i: (i, 0)),
        ],
        out_specs=[
            pl.BlockSpec((1, rows, nq), lambda b, i: (b, i, 0)),
            pl.BlockSpec((1, rows, nk), lambda b, i: (b, i, 0)),
            pl.BlockSpec((1, nk, rows), lambda b, i: (b, 0, i)),
        ],
        out_shape=[jax.ShapeDtypeStruct((bsz, t, nq), BF16),
                   jax.ShapeDtypeStruct((bsz, t, nk), BF16),
                   jax.ShapeDtypeStruct((bsz, nk, t), BF16)],
        scratch_shapes=[pltpu.VMEM((rows, w.shape[1]), F32)],
        compiler_params=pltpu.CompilerParams(vmem_limit_bytes=VMEM_LIMIT),
        name="qkv",
    )(*args, ng, w, cos, sin)


def _attn_kernel(sink_ref, bias_ref, *refs, n_ctx, nsub):
    blk = ATT_BLOCK
    band = 3 * blk
    q_refs = refs[:nsub]
    k_ref, vt_ref, o_ref = refs[nsub:]
    t = k_ref.shape[1]
    group = C_HEADS // C_KV_HEADS
    lane = lax.broadcasted_iota(jnp.int32, (blk, LANES), 1)
    low = (lane // (C_HEAD_DIM // 2)) % 2 == 0
    zero = jnp.zeros((blk, LANES), BF16)

    def band_start(sb):
        n = pl.program_id(1) * nsub + sb
        return pl.multiple_of(jnp.minimum(n_ctx + (n - 1) * blk, t - band), blk)

    def scores(sb, kg, part):
        a, high = divmod(kg, 2)
        cols = slice(a * LANES, (a + 1) * LANES)
        r_band = band_start(sb)
        parts, sinks = [], []
        for hh in range(part * ATT_UNIT_HEADS, (part + 1) * ATT_UNIT_HEADS):
            pair = q_refs[sb][0, :, (a * group + hh) * LANES:(a * group + hh + 1) * LANES]
            parts.append(jnp.where(~low if high else low, pair, zero))
            sinks.append(jnp.full((1, blk), sink_ref[kg * group + hh], F32))
        qu = jnp.concatenate(parts, axis=0)
        sink = jnp.concatenate(sinks, axis=1)
        kk = jnp.concatenate([k_ref[0, pl.ds(r_band, band), cols], k_ref[0, 0:n_ctx, cols]], axis=0)
        s = lax.dot_general(kk, qu, NT, preferred_element_type=F32)
        bias = jnp.concatenate([bias_ref[sb]] * ATT_UNIT_HEADS, axis=1)
        s = jnp.concatenate([s[:band] + bias, s[band:]], axis=0)
        mx = jnp.maximum(jnp.max(s, axis=0, keepdims=True), sink)
        return s, mx, sink

    def probs(s, mx, sink):
        return jnp.exp((s - mx).astype(BF16)), jnp.exp(sink - mx)

    def finish(sb, kg, part, p, p_sink):
        r_band = band_start(sb)
        rows = slice(kg * C_HEAD_DIM, (kg + 1) * C_HEAD_DIM)
        vt = jnp.concatenate([vt_ref[0, rows, pl.ds(r_band, band)], vt_ref[0, rows, 0:n_ctx]], axis=1)
        vt = jnp.concatenate([vt, jnp.ones((V_ROWS - C_HEAD_DIM, band + n_ctx), BF16)], axis=0)
        o3 = jnp.dot(vt, p, preferred_element_type=F32)
        o = o3[:C_HEAD_DIM] * (1.0 / (o3[C_HEAD_DIM:C_HEAD_DIM + 1] + p_sink))
        for pp in range(ATT_UNIT_HEADS // 2):
            y = jnp.concatenate([o[:, (2 * pp) * blk:(2 * pp + 1) * blk],
                                 o[:, (2 * pp + 1) * blk:(2 * pp + 2) * blk]], axis=0)
            c0 = ((kg * group + part * ATT_UNIT_HEADS) // 2 + pp) * LANES
            o_ref[0, sb * blk:(sb + 1) * blk, c0:c0 + LANES] = y.T.astype(o_ref.dtype)

    units = [(sb, kg, part) for sb in range(nsub) for kg in range(C_KV_HEADS)
             for part in range(group // ATT_UNIT_HEADS)]
    nu = len(units)
    sa, pa = ATT_AHEAD
    s_q, p_q = {}, {}
    for i in range(-sa, nu):
        if 0 <= i + sa < nu:
            s_q[i + sa] = scores(*units[i + sa])
        if 0 <= i + pa < nu:
            p_q[i + pa] = probs(*s_q.pop(i + pa))
        if i >= 0:
            finish(*units[i], *p_q.pop(i))


def _attn_bias(n_blocks, n_ctx, t):
    blk, band = ATT_BLOCK, 3 * ATT_BLOCK
    n = np.arange(n_blocks)[:, None, None]
    r_band = np.minimum(n_ctx + (n - 1) * blk, t - band)
    kp = np.arange(band)[None, :, None] + (r_band - n_ctx)
    qi = np.arange(blk)[None, None, :] + n * blk
    ok = (np.abs(kp - qi) <= ATT_WINDOW) & (kp >= 0)
    return jnp.asarray(np.where(ok, 0.0, -1e30).astype(np.float32))


def _attention(q, k2, vt, sink, n_ctx):
    bsz, t, nq = q.shape
    nk = k2.shape[2]
    n_blocks = (t - n_ctx) // ATT_BLOCK
    assert n_ctx % ATT_BLOCK == 0 and (t - n_ctx) % ATT_BLOCK == 0 and n_ctx >= ATT_BLOCK and n_blocks >= 2
    ctx_blocks = n_ctx // ATT_BLOCK
    nqb = next(n for n in (ATT_QUERY_BLOCKS, 4, 2, 1) if n_blocks % n == 0)
    rows = nqb * ATT_BLOCK
    bias = _attn_bias(n_blocks, n_ctx, t)
    return pl.pallas_call(
        functools.partial(_attn_kernel, n_ctx=n_ctx, nsub=nqb),
        grid=(bsz, n_blocks // nqb),
        in_specs=[
            pl.BlockSpec(memory_space=pltpu.SMEM),
            pl.BlockSpec((nqb,) + bias.shape[1:], lambda b, n: (n, 0, 0)),
        ] + [
            pl.BlockSpec((1, ATT_BLOCK, nq), lambda b, n, s=s: (b, n * nqb + s + ctx_blocks, 0)) for s in range(nqb)
        ] + [
            pl.BlockSpec((1, t, nk), lambda b, n: (b, 0, 0)),
            pl.BlockSpec((1, nk, t), lambda b, n: (b, 0, 0)),
        ],
        out_specs=pl.BlockSpec((1, rows, nq), lambda b, n: (b, n, 0)),
        out_shape=jax.ShapeDtypeStruct((bsz, t - n_ctx, nq), BF16),
        compiler_params=pltpu.CompilerParams(vmem_limit_bytes=VMEM_LIMIT),
        name="attention",
    )(sink, bias, *([q] * nqb), k2, vt)


def _rec_weights(w_in, w_g2, b_g2):
    widths = (A_HEADS * A_KEY,) * 3 + (A_HEADS * A_VAL,) * 2 + (B_HEADS * B_KEY,) * 2 + (
        B_HEADS * B_VAL, B_GATE_RANK, B_GATE_RANK, B_HEADS * B_VAL)
    off = np.concatenate([[0], np.cumsum(widths)])
    seg = lambda i: w_in[:, off[i]:off[i + 1]]
    d = w_in.shape[0]
    pad = jnp.zeros((d, LANES - 2 * B_GATE_RANK), w_in.dtype)
    w = jnp.concatenate([seg(i) for i in (0, 1, 2, 5, 8, 9)] + [pad] + [seg(i) for i in (3, 4, 6, 7, 10)],
                        axis=1).astype(BF16)
    nk = B_HEADS * B_KEY
    w2 = jnp.zeros((LANES, 2 * nk), F32)
    w2 = w2.at[:B_GATE_RANK, :nk].set(w_g2[0]).at[B_GATE_RANK:2 * B_GATE_RANK, nk:].set(w_g2[1])
    return w, w2.astype(BF16), b_g2.reshape(1, 2 * nk)


def _att_weights(w_qkv):
    d = w_qkv.shape[0]
    nq, nkv = C_HEADS * C_HEAD_DIM, C_KV_HEADS * C_HEAD_DIM
    group = C_HEADS // C_KV_HEADS
    pairs = C_HEAD_DIM // 2
    order = [(2 * a + r) * group + i for a in range(C_KV_HEADS // 2) for i in range(group) for r in range(2)]

    def blocks(w, heads):
        w = w.reshape(d, len(heads), pairs, 2)[:, heads]
        return w.reshape(d, len(heads) // 2, 2, pairs, 2).transpose(0, 1, 4, 2, 3).reshape(d, -1)

    wq = blocks(w_qkv[:, :nq], order)
    wk = blocks(w_qkv[:, nq:nq + nkv], list(range(C_KV_HEADS)))
    return jnp.concatenate([wq, wk, w_qkv[:, nq + nkv:]], axis=1).astype(BF16)


def _rope_tables(seq, n_ctx):
    n_rows = seq // GRID_W
    row = jnp.repeat(jnp.arange(n_rows), GRID_W).astype(F32)
    col = jnp.tile(jnp.arange(GRID_W), n_rows).astype(F32)
    half = C_HEAD_DIM // 2
    inv = ROPE_BASE ** (-jnp.arange(0, half, 2, dtype=F32) / half)
    ang = jnp.concatenate([row[:, None] * inv, col[:, None] * inv], axis=-1)
    cos, sin = jnp.cos(ang), jnp.sin(ang)
    cos = jnp.concatenate([jnp.ones((n_ctx, half), F32), cos], axis=0)
    sin = jnp.concatenate([jnp.zeros((n_ctx, half), F32), sin], axis=0)
    return jnp.tile(cos, (1, 4)), jnp.concatenate([-sin, -sin, sin, sin], axis=1)


def kernel(x, c, ctx, c_ctx, ada_w, ada_b, norm_g, rec_w_in, rec_w_out, rec_lb_logits, rec_w_g2, rec_b_g2,
           rec_gn_a, rec_gn_b, att_w_qkv, att_w_o, att_sink, ffn_w_in, ffn_w_out):
    bsz, seq, d = x.shape
    n_ctx = ctx.shape[1]
    depth = ada_w.shape[0]
    tm = TOKEN_TILE
    assert n_ctx % tm == 0 and seq % tm == 0 and n_ctx % SCAN_BLOCK == 0 and seq % SCAN_BLOCK == 0
    n_ctx_tiles = n_ctx // tm
    n_lat_tiles = seq // tm

    rows = -(-(bsz + 1) // 8) * 8
    cond = jnp.concatenate([c, c_ctx[None], jnp.zeros((rows - bsz - 1, d), F32)], axis=0)
    mod_all = _ada(cond, ada_w, ada_b)
    mods = []
    for l in range(depth):
        lat = mod_all[l, :bsz].reshape(bsz, 1, 6, d)
        cx = jnp.broadcast_to(mod_all[l, bsz].reshape(1, 1, 6, d), (bsz, 1, 6, d))
        mods.append(jnp.concatenate([cx, lat], axis=1))

    cos, sin = _rope_tables(seq, n_ctx)
    n_tiles = n_ctx_tiles + n_lat_tiles
    src = (ctx, x, 0)
    win_all, wout_all = ffn_w_in.astype(BF16), ffn_w_out.astype(BF16)

    for l in range(depth):
        need_ctx = l < depth - 1
        j = l // 2
        ng = norm_g[l]
        stream = _Stream(src[0], src[1], n_ctx_tiles, 0, src[2], _sub_tiles(n_tiles))
        if l % 2 == 0:
            w, w2, b2 = _rec_weights(rec_w_in[j], rec_w_g2[j], rec_b_g2[j])
            p = _rec_in(stream, n_tiles, mods[l], ng, w, w2, b2, rec_lb_logits, j)
            ya = _scan_hgrn(p, rec_gn_a[j].reshape(1, -1), n_ctx // SCAN_BLOCK)
            yb = _scan_gla(p, rec_gn_b[j].reshape(1, -1), n_ctx // SCAN_BLOCK)
            y0, y1, c0, c1, wo = ya, yb, 0, 0, rec_w_out[j]
            y_has_ctx = True
        else:
            q, k2, vt = _qkv(stream, n_tiles, mods[l], ng, _att_weights(att_w_qkv[j]), cos, sin)
            y = _attention(q, k2, vt, att_sink[j], n_ctx)
            y0, y1, c0, c1, wo = y, y, 0, 1, att_w_o[j]
            y_has_ctx = False
        wo = wo.astype(BF16)
        if need_ctx:
            assert y_has_ctx
            xcat = _post(y0, y1, c0, c1, stream, n_tiles, mods[l], ng, wo, win_all, wout_all, l)
            src = (xcat, xcat, n_ctx_tiles)
        else:
            if y_has_ctx:
                y0 = y0[:, :, n_ctx:]
                y1 = y1[:, :, n_ctx:]
            lat_only = _Stream(src[0], src[1], n_ctx_tiles, n_ctx_tiles, src[2], _sub_tiles(n_lat_tiles))
            x_lat = _post(y0, y1, c0, c1, lat_only, n_lat_tiles, mods[l], ng, wo, win_all, wout_all, l)
    return x_lat
```

```python
import functools

import numpy as np
import jax
import jax.numpy as jnp
from jax import lax
from jax.experimental import pallas as pl
from jax.experimental.pallas import tpu as pltpu

F32 = jnp.float32
BF16 = jnp.bfloat16

EPS = 1e-6
LANES = 128
TOKEN_TILE = 256
SCAN_BLOCK = 256
SCAN_SLOTS = 2
LA_CHUNK = 32
SCAN_GROUP = 4 * LA_CHUNK
ATT_BLOCK = 128
ATT_WINDOW = 128
ATT_QUERY_BLOCKS = 16
ATT_UNIT_HEADS = 2
ATT_AHEAD = (4, 2)
V_ROWS = 80
GRID_W = 64
ROPE_BASE = 10000.0
GLA_GATE_NORM = 16.0
A_HEADS, A_KEY, A_VAL = 4, 128, 128
B_HEADS, B_KEY, B_VAL = 4, 64, 128
B_GATE_RANK = 16
C_HEADS, C_KV_HEADS, C_HEAD_DIM = 16, 4, 64
REC_OG_A, REC_OG_B = 24, 32
VMEM_LIMIT = 58 * 1024 * 1024

NT = (((1,), (1,)), ((), ()))
TN = (((0,), (0,)), ((), ()))


def _silu(x):
    h = 0.5 * x
    return h * (jnp.tanh(h) + 1.0)


def _log_sigmoid(x):
    return jnp.minimum(x, 0.0) - jnp.log(1.0 + jnp.exp(-jnp.abs(x)))


def _rms(x, g):
    ms = jnp.mean(x * x, axis=-1, keepdims=True)
    return (x * lax.rsqrt(ms + EPS)) * g


def _modulate(x, g, shift, scale):
    return _rms(x, g * (1.0 + scale)) + shift


def _const_spec(shape):
    nd = len(shape)
    return pl.BlockSpec(shape, lambda *_: (0,) * nd, pipeline_mode=pl.Buffered(1))


class _Stream:
    def __init__(self, ctx_arr, lat_arr, n_ctx_tiles, tile0=0, lat_off=0, nsub=1):
        self.arrays = (ctx_arr, lat_arr)
        self.n_ctx_tiles, self.tile0, self.lat_off, self.nsub = n_ctx_tiles, tile0, lat_off, nsub
        self.d = ctx_arr.shape[-1]

    def specs_for(self, s):
        nct, off, lat_off, nsub = self.n_ctx_tiles, self.tile0 + s, self.lat_off, self.nsub
        ctx_rows, ctx_tile = (8, lambda b, i: 0) if off >= nct else (
            TOKEN_TILE, lambda b, i: jnp.minimum(i * nsub + off, nct - 1))
        return [
            pl.BlockSpec((1, ctx_rows, self.d), lambda b, i: (b, ctx_tile(b, i), 0)),
            pl.BlockSpec((1, TOKEN_TILE, self.d),
                         lambda b, i: (b, jnp.maximum(i * nsub + off - nct, 0) + lat_off, 0)),
        ]

    def mod_spec_for(self, s):
        nct, off, nsub = self.n_ctx_tiles, self.tile0 + s, self.nsub
        return pl.BlockSpec((1, 1, 6, self.d), lambda b, i: (b, (i * nsub + off >= nct).astype(jnp.int32), 0, 0))

    def tile(self, ctx_ref, lat_ref, s):
        if self.tile0 + s >= self.n_ctx_tiles:
            return lat_ref[0]
        g = pl.program_id(1) * self.nsub + s + self.tile0
        return jnp.where(g < self.n_ctx_tiles, ctx_ref[0], lat_ref[0])

    def operands(self, mod):
        specs, args = [], []
        for s in range(self.nsub):
            specs += self.specs_for(s) + [self.mod_spec_for(s)]
            args += [*self.arrays, mod]
        return specs, args


def _sub_tiles(n_tiles):
    return next(n for n in (4, 3, 2, 1) if n_tiles % n == 0)


def _staggered(nsub, stages):
    live = {}
    for step in range(nsub + len(stages) - 1):
        for k in range(len(stages)):
            s = step - k
            if 0 <= s < nsub:
                live[s] = stages[k](s, live.get(s))


def _ada_kernel(c_ref, w_ref, b_ref, o_ref):
    s = _silu(c_ref[...])
    o_ref[0] = jnp.dot(s.astype(BF16), w_ref[0].astype(BF16), preferred_element_type=F32) + b_ref[0]


def _ada(cond, ada_w, ada_b):
    depth, d, n = ada_w.shape
    rows = cond.shape[0]
    tn = n // 4
    return pl.pallas_call(
        _ada_kernel,
        grid=(depth, n // tn),
        in_specs=[
            pl.BlockSpec((rows, d), lambda l, j: (0, 0)),
            pl.BlockSpec((1, d, tn), lambda l, j: (l, 0, j)),
            pl.BlockSpec((1, 1, tn), lambda l, j: (l, 0, j)),
        ],
        out_specs=pl.BlockSpec((1, rows, tn), lambda l, j: (l, 0, j)),
        out_shape=jax.ShapeDtypeStruct((depth, rows, n), F32),
        compiler_params=pltpu.CompilerParams(vmem_limit_bytes=VMEM_LIMIT),
        name="ada",
    )(cond, ada_w, ada_b.reshape(depth, 1, n))


def _recin_kernel(*refs, stream, layer_slot):
    nsub = stream.nsub
    ng_ref, w_ref, w2_ref, b2_ref, lb_ref, o_ref = refs[3 * nsub:]
    tm = TOKEN_TILE
    lg = lb_ref[...]
    e = jnp.exp(lg - jnp.max(lg, axis=0, keepdims=True))
    lb = jnp.sum((e / jnp.sum(e, axis=0, keepdims=True))[:layer_slot + 1], axis=0)
    na, nb = A_HEADS * A_KEY, B_HEADS * B_KEY
    n_act = 3 * na + nb

    def modulated(s, _):
        xc_ref, xl_ref, mod_ref = refs[3 * s:3 * s + 3]
        m = mod_ref[0, 0]
        return _modulate(stream.tile(xc_ref, xl_ref, s), ng_ref[0:1], m[0:1], m[1:2]).astype(BF16)

    def project(s, u):
        rows = slice(s * tm, (s + 1) * tm)

        def put(c0, val):
            for j in range(val.shape[1] // LANES):
                o_ref[c0 // LANES + j, rows, :] = val[:, j * LANES:(j + 1) * LANES]

        p = jnp.dot(u, w_ref[:, :n_act + LANES], preferred_element_type=F32)
        put(0, p[:, :n_act])
        pre = jnp.dot(p[:, n_act:].astype(BF16), w2_ref[...], preferred_element_type=F32)
        put(n_act, pre + b2_ref[...])
        put(n_act + 2 * nb, jnp.dot(u, w_ref[:, n_act + LANES:], preferred_element_type=F32))

    def activate(s, _):
        rows = slice(s * tm, (s + 1) * tm)
        for c0 in range(0, n_act + 2 * nb, LANES):
            v = o_ref[c0 // LANES, rows, :]
            if c0 < na:
                v = _silu(v) * (A_KEY ** -0.5)
            elif c0 < 3 * na:
                d, h0 = divmod(c0 - na, na)
                lbd = lb[d:d + 1, h0:h0 + LANES]
                v = jnp.log(0.5 * (1.0 + lbd) + (0.5 * (1.0 - lbd)) * jnp.tanh(0.5 * v))
            elif c0 < n_act:
                v = v * (B_KEY ** -0.5)
            else:
                v = _log_sigmoid(v) * (1.0 / GLA_GATE_NORM)
            o_ref[c0 // LANES, rows, :] = v

    _staggered(nsub, [modulated, project, activate])


def _rec_in(stream, n_tiles, mod, ng, w, w2, b2, lb_logits, layer_slot):
    bsz = mod.shape[0]
    n_out = w.shape[1] - LANES + w2.shape[1]
    rows = stream.nsub * TOKEN_TILE
    assert n_tiles % stream.nsub == 0
    specs, args = stream.operands(mod)
    consts = [ng, w, w2, b2, lb_logits]
    return pl.pallas_call(
        functools.partial(_recin_kernel, stream=stream, layer_slot=layer_slot),
        grid=(bsz, n_tiles // stream.nsub),
        in_specs=specs + [_const_spec(a.shape) for a in consts],
        out_specs=pl.BlockSpec((None, n_out // LANES, rows, LANES), lambda b, i: (b, 0, i, 0)),
        out_shape=jax.ShapeDtypeStruct((bsz, n_out // LANES, n_tiles * TOKEN_TILE, LANES), F32),
        compiler_params=pltpu.CompilerParams(vmem_limit_bytes=VMEM_LIMIT),
        name="rec_in",
    )(*args, *consts)


def _chunk_cumsum(g, reverse):
    r, width = g.shape
    sl = 8
    x = g.reshape(r // sl, sl, width)
    sub = lax.broadcasted_iota(jnp.int32, x.shape, 1)
    for s in (1, 2, 4):
        if reverse:
            x = x + jnp.where(sub < sl - s, pltpu.roll(x, sl - s, 1), 0.0)
        else:
            x = x + jnp.where(sub >= s, pltpu.roll(x, s, 1), 0.0)
    per = LA_CHUNK // sl
    x = x.reshape(r // LA_CHUNK, per, sl, width)
    parts = [None] * per
    carry = None
    for v in (range(per - 1, -1, -1) if reverse else range(per)):
        xv = x[:, v:v + 1]
        parts[v] = xv if carry is None else xv + carry
        edge = xv[:, :, 0:1] if reverse else xv[:, :, sl - 1:sl]
        carry = edge if carry is None else carry + edge
    return jnp.concatenate(parts, axis=1).reshape(r, width)


def _group_masks(reverse):
    n = SCAN_GROUP
    ri = lax.broadcasted_iota(jnp.int32, (n, n), 0)
    ci = lax.broadcasted_iota(jnp.int32, (n, n), 1)
    rc, cc = ri // LA_CHUNK, ci // LA_CHUNK
    if reverse:
        return (rc == cc) & (ci >= ri), (rc + 1 == cc) & (rc % 2 == 0), (rc < 2) & (cc >= 2)
    return (rc == cc) & (ci <= ri), (rc == cc + 1) & (rc % 2 == 1), (rc >= 2) & (cc < 2)


def _scale_chunks(x, factors):
    parts = []
    for c, f in enumerate(factors):
        xc = x[c * LA_CHUNK:(c + 1) * LA_CHUNK]
        parts.append(xc if f is None else xc * f)
    return jnp.concatenate(parts, axis=0)


def _scan_prep(q, k, g, reverse):
    bc = _chunk_cumsum(g, reverse)
    e = jnp.exp(bc)
    return bc, q * e, k / e


def _scan_group(gi, prep, v_bf, states, masks, lane_masks, reverse):
    bc, q_dec, k_inv = prep
    nsub = len(states)
    nch = SCAN_GROUP // LA_CHUNK
    assert nch == 4
    m_diag, m_adj, m_far = masks
    pos = (lambda c: nch - 1 - c) if reverse else (lambda c: c)
    edge = 0 if reverse else LA_CHUNK - 1
    r0 = gi * SCAN_GROUP
    rs = slice(r0, r0 + SCAN_GROUP)
    tot = [bc[r0 + c * LA_CHUNK + edge:r0 + c * LA_CHUNK + edge + 1] for c in range(nch)]
    ts = [tot[pos(s)] for s in range(nch)]
    e_ts = [jnp.exp(t) for t in ts]
    e_pre = [None, e_ts[0], jnp.exp(ts[0] + ts[1]), jnp.exp(ts[0] + ts[1] + ts[2])]
    e_post = [jnp.exp(ts[1] + ts[2] + ts[3]), jnp.exp(ts[2] + ts[3]), e_ts[3], None]
    e_total = jnp.exp(ts[0] + ts[1] + ts[2] + ts[3])
    ki = k_inv[rs]
    ke = _scale_chunks(ki, [e_ts[pos(c)] for c in range(nch)])
    kh = _scale_chunks(ke, [e_post[pos(c)] for c in range(nch)])
    kb = _scale_chunks(ke, [e_ts[1] if pos(c) == 0 else None for c in range(nch)])
    kcat = jnp.concatenate([ki.astype(BF16), ke.astype(BF16)], axis=0)
    kb, kh = kb.astype(BF16), kh.astype(BF16)
    outs, new_states = [], []
    for j in range(nsub):
        qd = q_dec[rs]
        if lane_masks is not None:
            qd = jnp.where(lane_masks[j], qd, 0.0)
        qt = _scale_chunks(qd, [e_pre[pos(c)] for c in range(nch)]).astype(BF16)
        qb = _scale_chunks(qd, [e_ts[2] if pos(c) == 3 else None for c in range(nch)]).astype(BF16)
        qd = qd.astype(BF16)
        v_g = v_bf[j][rs]
        s12 = lax.dot_general(qd, kcat, NT, preferred_element_type=F32)
        s3 = lax.dot_general(qb, kb, NT, preferred_element_type=F32)
        o_state = lax.dot_general(qt, states[j].astype(BF16), NT, preferred_element_type=F32)
        upd = lax.dot_general(v_g, kh, TN, preferred_element_type=F32)
        sc = jnp.where(m_diag, s12[:, :SCAN_GROUP],
                       jnp.where(m_adj, s12[:, SCAN_GROUP:], jnp.where(m_far, s3, 0.0)))
        outs.append(o_state + jnp.dot(sc.astype(BF16), v_g, preferred_element_type=F32))
        new_states.append(states[j] * e_total + upd)
    return outs, tuple(new_states)


def _scan_kernel(*refs, mode, n_ctx_blocks, nslot):
    if mode == "hgrn":
        (q_ref, gf_ref, gb_ref, v_ref, o_ref, of_ref, ob_ref) = refs
        nsub = 1
    else:
        (q_ref, k_ref, gf_ref, gb_ref, v_ref, o_ref, of_ref, ob_ref) = refs
        nsub = 2
    t = q_ref.shape[1]
    blk = SCAN_BLOCK
    nblk = t // blk
    masks_f, masks_b = _group_masks(False), _group_masks(True)
    if nsub == 2:
        lane = lax.broadcasted_iota(jnp.int32, (SCAN_GROUP, LANES), 1)
        lane_masks = [lane < B_KEY, lane >= B_KEY]
    else:
        lane_masks = None

    def features(slot, rows, d):
        g = (gf_ref, gb_ref)[d][slot, rows, :]
        if mode == "hgrn":
            return q_ref[slot, rows, :], 1.0 - jnp.exp(g), g
        return q_ref[slot, rows, :], k_ref[slot, rows, :], g

    ngroup = blk // SCAN_GROUP

    def prep(i):
        jb = n_ctx_blocks - 1 - i if i < n_ctx_blocks else nblk - 1 - (i - n_ctx_blocks)
        rows_f = slice(i * blk, (i + 1) * blk)
        rows_b = slice(jb * blk, (jb + 1) * blk)
        per_slot = []
        for slot in range(nslot):
            per_slot.append((_scan_prep(*features(slot, rows_f, 0), False),
                             _scan_prep(*features(slot, rows_b, 1), True),
                             [v_ref[slot * nsub + j, rows_f, :].astype(BF16) for j in range(nsub)],
                             [v_ref[slot * nsub + j, rows_b, :].astype(BF16) for j in range(nsub)]))
        return jb, per_slot

    def finish(blk_i):
        rows = slice(blk_i * blk, (blk_i + 1) * blk)
        for hj in range(nslot * nsub):
            cols = slice(hj * LANES, (hj + 1) * LANES)
            o_ref[hj, rows, :] = of_ref[rows, cols] + ob_ref[rows, cols]

    zero = tuple(jnp.zeros((LANES, LANES), F32) for _ in range(nsub))
    st_f, st_b = [zero] * nslot, [zero] * nslot
    seen = {}
    nxt = prep(0)
    for i in range(nblk):
        jb, per_slot = nxt
        if i + 1 < nblk:
            nxt = prep(i + 1)
        for s in range(ngroup):
            gf, gb = s, ngroup - 1 - s
            for slot in range(nslot):
                prep_f, prep_b, v_f, v_b = per_slot[slot]
                o, st_f[slot] = _scan_group(gf, prep_f, v_f, st_f[slot], masks_f, lane_masks, False)
                r0 = i * blk + gf * SCAN_GROUP
                for j in range(nsub):
                    c0 = (slot * nsub + j) * LANES
                    of_ref[r0:r0 + SCAN_GROUP, c0:c0 + LANES] = o[j]
                o, st_b[slot] = _scan_group(gb, prep_b, v_b, st_b[slot], masks_b, lane_masks, True)
                r0 = jb * blk + gb * SCAN_GROUP
                for j in range(nsub):
                    c0 = (slot * nsub + j) * LANES
                    ob_ref[r0:r0 + SCAN_GROUP, c0:c0 + LANES] = o[j]
        for done in (i, jb):
            seen[done] = seen.get(done, 0) + 1
            if seen[done] == 2:
                finish(done)


def _group_spec(t, first, per_step):
    assert first % per_step == 0
    return pl.BlockSpec((None, per_step, t, LANES), lambda b, h: (b, first // per_step + h, 0, 0))


def _scan_hgrn(p, n_ctx_blocks):
    bsz, _, t, _ = p.shape
    ns = SCAN_SLOTS
    spec = lambda first: _group_spec(t, first, ns)
    return pl.pallas_call(
        functools.partial(_scan_kernel, mode="hgrn", n_ctx_blocks=n_ctx_blocks, nslot=ns),
        grid=(bsz, A_HEADS // ns),
        in_specs=[spec(0), spec(4), spec(8), spec(20)],
        out_specs=spec(0),
        out_shape=jax.ShapeDtypeStruct((bsz, A_HEADS, t, A_VAL), F32),
        scratch_shapes=[pltpu.VMEM((t, ns * LANES), F32), pltpu.VMEM((t, ns * LANES), F32)],
        compiler_params=pltpu.CompilerParams(vmem_limit_bytes=VMEM_LIMIT),
        name="scan_hgrn",
    )(p, p, p, p)


def _scan_gla(p, n_ctx_blocks):
    bsz, _, t, _ = p.shape
    ns = SCAN_SLOTS
    one, two = (lambda first: _group_spec(t, first, ns)), (lambda first: _group_spec(t, first, 2 * ns))
    return pl.pallas_call(
        functools.partial(_scan_kernel, mode="gla", n_ctx_blocks=n_ctx_blocks, nslot=ns),
        grid=(bsz, B_HEADS // 2 // ns),
        in_specs=[one(12), one(18), one(14), one(16), two(28)],
        out_specs=two(0),
        out_shape=jax.ShapeDtypeStruct((bsz, B_HEADS, t, B_VAL), F32),
        scratch_shapes=[pltpu.VMEM((t, 2 * ns * LANES), F32), pltpu.VMEM((t, 2 * ns * LANES), F32)],
        compiler_params=pltpu.CompilerParams(vmem_limit_bytes=VMEM_LIMIT),
        name="scan_gla",
    )(p, p, p, p, p)


def _post_kernel(*refs, stream, gated):
    nsub = stream.nsub
    ny = 4 if gated else 2
    y_refs = refs[3 * nsub:(3 + ny) * nsub]
    tail = refs[(3 + ny) * nsub:]
    ng_ref, wo_ref, win_ref, wout_ref = tail[:4]
    gn_refs, o_ref = tail[4:-1], tail[-1]
    hid = wout_ref.shape[0]
    tm = TOKEN_TILE

    def mix(s, _):
        xc_ref, xl_ref, mod_ref = refs[3 * s:3 * s + 3]
        y0_ref, y1_ref = y_refs[ny * s:ny * s + 2]
        m = mod_ref[0, 0]
        half = wo_ref.shape[0] // 2

        def rows_of(y_ref, k):
            if len(y_ref.shape) == 2:
                return y_ref[...]
            gate_ref, gn = y_refs[ny * s + 2 + k], gn_refs[k][...]
            heads = [(_rms(y_ref[h], gn) * _silu(gate_ref[h])).astype(BF16) for h in range(y_ref.shape[0])]
            return jnp.concatenate(heads, axis=1)

        t = jnp.dot(rows_of(y0_ref, 0), wo_ref[:half, :], preferred_element_type=F32)
        t = t + jnp.dot(rows_of(y1_ref, 1), wo_ref[half:, :], preferred_element_type=F32)
        x1 = stream.tile(xc_ref, xl_ref, s) + _rms(t, ng_ref[1:2] * m[2:3])
        return x1, _modulate(x1, ng_ref[2:3], m[3:4], m[4:5]).astype(BF16)

    def ffn_in(s, carry):
        x1, u = carry
        h = jnp.dot(u, win_ref[...], preferred_element_type=F32)
        return x1, (_silu(h[:, :hid]) * h[:, hid:]).astype(BF16)

    def ffn_out(s, carry):
        x1, act = carry
        m = refs[3 * s + 2][0, 0]
        h2 = jnp.dot(act, wout_ref[...], preferred_element_type=F32)
        o_ref[0, s * tm:(s + 1) * tm, :] = x1 + _rms(h2, ng_ref[3:4] * m[5:6])

    _staggered(nsub, [mix, ffn_in, ffn_out])


def _layer_spec(stacked, layer):
    nd = stacked.ndim - 1
    return pl.BlockSpec((None,) + stacked.shape[1:], lambda *_: (layer,) + (0,) * nd, pipeline_mode=pl.Buffered(1))


def _post(y0, y1, y_col0, y_col1, stream, n_tiles, mod, ng, wo, win_all, wout_all, layer, gates=None):
    bsz, d = mod.shape[0], mod.shape[-1]
    tm = TOKEN_TILE
    half = d // 2
    nsub = stream.nsub
    assert n_tiles % nsub == 0
    specs, args = stream.operands(mod)

    def y_spec(y, col, s, first=0):
        if y.ndim == 4:
            heads = half // LANES
            assert first % heads == 0
            return pl.BlockSpec((None, heads, tm, LANES), lambda b, i: (b, first // heads, i * nsub + s, 0))
        return pl.BlockSpec((None, tm, half), lambda b, i: (b, i * nsub + s, col))

    for s in range(nsub):
        specs += [y_spec(y0, y_col0, s), y_spec(y1, y_col1, s)]
        args += [y0, y1]
        if gates is not None:
            specs += [y_spec(gates[0], 0, s, gates[1]), y_spec(gates[0], 0, s, gates[2])]
            args += [gates[0], gates[0]]
    consts = [ng, wo, win_all, wout_all] + ([] if gates is None else [gates[3], gates[4]])
    return pl.pallas_call(
        functools.partial(_post_kernel, stream=stream, gated=gates is not None),
        grid=(bsz, n_tiles // nsub),
        in_specs=specs + [_const_spec(ng.shape), _const_spec(wo.shape),
                          _layer_spec(win_all, layer), _layer_spec(wout_all, layer)]
        + ([] if gates is None else [_const_spec(gates[3].shape), _const_spec(gates[4].shape)]),
        out_specs=pl.BlockSpec((1, nsub * tm, d), lambda b, i: (b, i, 0)),
        out_shape=jax.ShapeDtypeStruct((bsz, n_tiles * tm, d), F32),
        compiler_params=pltpu.CompilerParams(vmem_limit_bytes=VMEM_LIMIT),
        name="post",
    )(*args, *consts)


def _rope(x, cos, sin_signed):
    return x * cos + pltpu.roll(x, LANES // 2, 1) * sin_signed


def _qkv_kernel(*refs, stream):
    nsub = stream.nsub
    ng_ref, w_ref, cos_ref, sin_ref, q_ref, k_ref, vt_ref, p_ref = refs[3 * nsub:]
    tm = TOKEN_TILE
    nq, nk = q_ref.shape[2], k_ref.shape[2]

    def modulated(s, _):
        xc_ref, xl_ref, mod_ref = refs[3 * s:3 * s + 3]
        m = mod_ref[0, 0]
        return _modulate(stream.tile(xc_ref, xl_ref, s), ng_ref[0:1], m[0:1], m[1:2]).astype(BF16)

    def project(s, u):
        p_ref[s * tm:(s + 1) * tm, :] = jnp.dot(u, w_ref[...], preferred_element_type=F32)

    def rotate(s, _):
        rows = slice(s * tm, (s + 1) * tm)
        cos, sin = cos_ref[rows, :], sin_ref[rows, :]
        for j in range(nq // LANES):
            cols = slice(j * LANES, (j + 1) * LANES)
            q_ref[0, rows, cols] = (_rope(p_ref[rows, cols], cos, sin)
                                    * (C_HEAD_DIM ** -0.5)).astype(BF16)
        for j in range(nk // LANES):
            cols = slice(nq + j * LANES, nq + (j + 1) * LANES)
            k_ref[0, rows, j * LANES:(j + 1) * LANES] = _rope(p_ref[rows, cols], cos, sin).astype(BF16)
        vt_ref[0, :, rows] = p_ref[rows, nq + nk:].T.astype(BF16)

    _staggered(nsub, [modulated, project, rotate])


def _qkv(stream, n_tiles, mod, ng, w, cos, sin):
    bsz = mod.shape[0]
    rows = stream.nsub * TOKEN_TILE
    assert n_tiles % stream.nsub == 0
    t = n_tiles * TOKEN_TILE
    nq = C_HEADS * C_HEAD_DIM
    nk = C_KV_HEADS * C_HEAD_DIM
    specs, args = stream.operands(mod)
    return pl.pallas_call(
        functools.partial(_qkv_kernel, stream=stream),
        grid=(bsz, n_tiles // stream.nsub),
        in_specs=specs + [
            _const_spec(ng.shape), _const_spec(w.shape),
            pl.BlockSpec((rows, LANES), lambda b, i: (i, 0)),
            pl.BlockSpec((rows, LANES), lambda b, i: (i, 0)),
        ],
        out_specs=[
            pl.BlockSpec((1, rows, nq), lambda b, i: (b, i, 0)),
            pl.BlockSpec((1, rows, nk), lambda b, i: (b, i, 0)),
            pl.BlockSpec((1, nk, rows), lambda b, i: (b, 0, i)),
        ],
        out_shape=[jax.ShapeDtypeStruct((bsz, t, nq), BF16),
                   jax.ShapeDtypeStruct((bsz, t, nk), BF16),
                   jax.ShapeDtypeStruct((bsz, nk, t), BF16)],
        scratch_shapes=[pltpu.VMEM((rows, w.shape[1]), F32)],
        compiler_params=pltpu.CompilerParams(vmem_limit_bytes=VMEM_LIMIT),
        name="qkv",
    )(*args, ng, w, cos, sin)


def _attn_kernel(sink_ref, bias_ref, *refs, n_ctx, nsub):
    blk = ATT_BLOCK
    band = 3 * blk
    q_refs = refs[:nsub]
    k_ref, vt_ref, o_ref = refs[nsub:]
    t = k_ref.shape[1]
    group = C_HEADS // C_KV_HEADS
    lane = lax.broadcasted_iota(jnp.int32, (blk, LANES), 1)
    low = (lane // (C_HEAD_DIM // 2)) % 2 == 0
    zero = jnp.zeros((blk, LANES), BF16)

    def band_start(sb):
        n = pl.program_id(1) * nsub + sb
        return pl.multiple_of(jnp.minimum(n_ctx + (n - 1) * blk, t - band), blk)

    def scores(sb, kg, part):
        a, high = divmod(kg, 2)
        cols = slice(a * LANES, (a + 1) * LANES)
        r_band = band_start(sb)
        parts, sinks = [], []
        for hh in range(part * ATT_UNIT_HEADS, (part + 1) * ATT_UNIT_HEADS):
            pair = q_refs[sb][0, :, (a * group + hh) * LANES:(a * group + hh + 1) * LANES]
            parts.append(jnp.where(~low if high else low, pair, zero))
            sinks.append(jnp.full((1, blk), sink_ref[kg * group + hh], F32))
        qu = jnp.concatenate(parts, axis=0)
        sink = jnp.concatenate(sinks, axis=1)
        kk = jnp.concatenate([k_ref[0, pl.ds(r_band, band), cols], k_ref[0, 0:n_ctx, cols]], axis=0)
        s = lax.dot_general(kk, qu, NT, preferred_element_type=F32)
        bias = jnp.concatenate([bias_ref[sb]] * ATT_UNIT_HEADS, axis=1)
        s = jnp.concatenate([s[:band] + bias, s[band:]], axis=0)
        mx = jnp.maximum(jnp.max(s, axis=0, keepdims=True), sink)
        return s, mx, sink

    def probs(s, mx, sink):
        return jnp.exp((s - mx).astype(BF16)), jnp.exp(sink - mx)

    def finish(sb, kg, part, p, p_sink):
        r_band = band_start(sb)
        rows = slice(kg * C_HEAD_DIM, (kg + 1) * C_HEAD_DIM)
        vt = jnp.concatenate([vt_ref[0, rows, pl.ds(r_band, band)], vt_ref[0, rows, 0:n_ctx]], axis=1)
        vt = jnp.concatenate([vt, jnp.ones((V_ROWS - C_HEAD_DIM, band + n_ctx), BF16)], axis=0)
        o3 = jnp.dot(vt, p, preferred_element_type=F32)
        o = o3[:C_HEAD_DIM] * (1.0 / (o3[C_HEAD_DIM:C_HEAD_DIM + 1] + p_sink))
        for pp in range(ATT_UNIT_HEADS // 2):
            y = jnp.concatenate([o[:, (2 * pp) * blk:(2 * pp + 1) * blk],
                                 o[:, (2 * pp + 1) * blk:(2 * pp + 2) * blk]], axis=0)
            c0 = ((kg * group + part * ATT_UNIT_HEADS) // 2 + pp) * LANES
            o_ref[0, sb * blk:(sb + 1) * blk, c0:c0 + LANES] = y.T.astype(o_ref.dtype)

    units = [(sb, kg, part) for sb in range(nsub) for kg in range(C_KV_HEADS)
             for part in range(group // ATT_UNIT_HEADS)]
    nu = len(units)
    sa, pa = ATT_AHEAD
    s_q, p_q = {}, {}
    for i in range(-sa, nu):
        if 0 <= i + sa < nu:
            s_q[i + sa] = scores(*units[i + sa])
        if 0 <= i + pa < nu:
            p_q[i + pa] = probs(*s_q.pop(i + pa))
        if i >= 0:
            finish(*units[i], *p_q.pop(i))


def _attn_bias(n_blocks, n_ctx, t):
    blk, band = ATT_BLOCK, 3 * ATT_BLOCK
    n = np.arange(n_blocks)[:, None, None]
    r_band = np.minimum(n_ctx + (n - 1) * blk, t - band)
    kp = np.arange(band)[None, :, None] + (r_band - n_ctx)
    qi = np.arange(blk)[None, None, :] + n * blk
    ok = (np.abs(kp - qi) <= ATT_WINDOW) & (kp >= 0)
    return jnp.asarray(np.where(ok, 0.0, -1e30).astype(np.float32))


def _attention(q, k2, vt, sink, n_ctx):
    bsz, t, nq = q.shape
    nk = k2.shape[2]
    n_blocks = (t - n_ctx) // ATT_BLOCK
    assert n_ctx % ATT_BLOCK == 0 and (t - n_ctx) % ATT_BLOCK == 0 and n_ctx >= ATT_BLOCK and n_blocks >= 2
    ctx_blocks = n_ctx // ATT_BLOCK
    nqb = next(n for n in (ATT_QUERY_BLOCKS, 4, 2, 1) if n_blocks % n == 0)
    rows = nqb * ATT_BLOCK
    bias = _attn_bias(n_blocks, n_ctx, t)
    return pl.pallas_call(
        functools.partial(_attn_kernel, n_ctx=n_ctx, nsub=nqb),
        grid=(bsz, n_blocks // nqb),
        in_specs=[
            pl.BlockSpec(memory_space=pltpu.SMEM),
            pl.BlockSpec((nqb,) + bias.shape[1:], lambda b, n: (n, 0, 0)),
        ] + [
            pl.BlockSpec((1, ATT_BLOCK, nq), lambda b, n, s=s: (b, n * nqb + s + ctx_blocks, 0)) for s in range(nqb)
        ] + [
            pl.BlockSpec((1, t, nk), lambda b, n: (b, 0, 0)),
            pl.BlockSpec((1, nk, t), lambda b, n: (b, 0, 0)),
        ],
        out_specs=pl.BlockSpec((1, rows, nq), lambda b, n: (b, n, 0)),
        out_shape=jax.ShapeDtypeStruct((bsz, t - n_ctx, nq), BF16),
        compiler_params=pltpu.CompilerParams(vmem_limit_bytes=VMEM_LIMIT),
        name="attention",
    )(sink, bias, *([q] * nqb), k2, vt)


def _rec_weights(w_in, w_g2, b_g2):
    widths = (A_HEADS * A_KEY,) * 3 + (A_HEADS * A_VAL,) * 2 + (B_HEADS * B_KEY,) * 2 + (
        B_HEADS * B_VAL, B_GATE_RANK, B_GATE_RANK, B_HEADS * B_VAL)
    off = np.concatenate([[0], np.cumsum(widths)])
    seg = lambda i: w_in[:, off[i]:off[i + 1]]
    d = w_in.shape[0]
    pad = jnp.zeros((d, LANES - 2 * B_GATE_RANK), w_in.dtype)
    w = jnp.concatenate([seg(i) for i in (0, 1, 2, 5, 8, 9)] + [pad] + [seg(i) for i in (6, 3, 4, 7, 10)],
                        axis=1).astype(BF16)
    nk = B_HEADS * B_KEY
    w2 = jnp.zeros((LANES, 2 * nk), F32)
    w2 = w2.at[:B_GATE_RANK, :nk].set(w_g2[0]).at[B_GATE_RANK:2 * B_GATE_RANK, nk:].set(w_g2[1])
    return w, w2.astype(BF16), b_g2.reshape(1, 2 * nk)


def _att_weights(w_qkv):
    d = w_qkv.shape[0]
    nq, nkv = C_HEADS * C_HEAD_DIM, C_KV_HEADS * C_HEAD_DIM
    group = C_HEADS // C_KV_HEADS
    pairs = C_HEAD_DIM // 2
    order = [(2 * a + r) * group + i for a in range(C_KV_HEADS // 2) for i in range(group) for r in range(2)]

    def blocks(w, heads):
        w = w.reshape(d, len(heads), pairs, 2)[:, heads]
        return w.reshape(d, len(heads) // 2, 2, pairs, 2).transpose(0, 1, 4, 2, 3).reshape(d, -1)

    wq = blocks(w_qkv[:, :nq], order)
    wk = blocks(w_qkv[:, nq:nq + nkv], list(range(C_KV_HEADS)))
    return jnp.concatenate([wq, wk, w_qkv[:, nq + nkv:]], axis=1).astype(BF16)


def _rope_tables(seq, n_ctx):
    n_rows = seq // GRID_W
    row = jnp.repeat(jnp.arange(n_rows), GRID_W).astype(F32)
    col = jnp.tile(jnp.arange(GRID_W), n_rows).astype(F32)
    half = C_HEAD_DIM // 2
    inv = ROPE_BASE ** (-jnp.arange(0, half, 2, dtype=F32) / half)
    ang = jnp.concatenate([row[:, None] * inv, col[:, None] * inv], axis=-1)
    cos, sin = jnp.cos(ang), jnp.sin(ang)
    cos = jnp.concatenate([jnp.ones((n_ctx, half), F32), cos], axis=0)
    sin = jnp.concatenate([jnp.zeros((n_ctx, half), F32), sin], axis=0)
    return jnp.tile(cos, (1, 4)), jnp.concatenate([-sin, -sin, sin, sin], axis=1)


def kernel(x, c, ctx, c_ctx, ada_w, ada_b, norm_g, rec_w_in, rec_w_out, rec_lb_logits, rec_w_g2, rec_b_g2,
           rec_gn_a, rec_gn_b, att_w_qkv, att_w_o, att_sink, ffn_w_in, ffn_w_out):
    bsz, seq, d = x.shape
    n_ctx = ctx.shape[1]
    depth = ada_w.shape[0]
    tm = TOKEN_TILE
    assert n_ctx % tm == 0 and seq % tm == 0 and n_ctx % SCAN_BLOCK == 0 and seq % SCAN_BLOCK == 0
    n_ctx_tiles = n_ctx // tm
    n_lat_tiles = seq // tm

    rows = -(-(bsz + 1) // 8) * 8
    cond = jnp.concatenate([c, c_ctx[None], jnp.zeros((rows - bsz - 1, d), F32)], axis=0)
    mod_all = _ada(cond, ada_w, ada_b)
    mods = []
    for l in range(depth):
        lat = mod_all[l, :bsz].reshape(bsz, 1, 6, d)
        cx = jnp.broadcast_to(mod_all[l, bsz].reshape(1, 1, 6, d), (bsz, 1, 6, d))
        mods.append(jnp.concatenate([cx, lat], axis=1))

    cos, sin = _rope_tables(seq, n_ctx)
    n_tiles = n_ctx_tiles + n_lat_tiles
    src = (ctx, x, 0)
    win_all, wout_all = ffn_w_in.astype(BF16), ffn_w_out.astype(BF16)

    for l in range(depth):
        need_ctx = l < depth - 1
        j = l // 2
        ng = norm_g[l]
        stream = _Stream(src[0], src[1], n_ctx_tiles, 0, src[2], _sub_tiles(n_tiles))
        if l % 2 == 0:
            w, w2, b2 = _rec_weights(rec_w_in[j], rec_w_g2[j], rec_b_g2[j])
            p = _rec_in(stream, n_tiles, mods[l], ng, w, w2, b2, rec_lb_logits, j)
            ya = _scan_hgrn(p, n_ctx // SCAN_BLOCK)
            yb = _scan_gla(p, n_ctx // SCAN_BLOCK)
            y0, y1, c0, c1, wo = ya, yb, 0, 0, rec_w_out[j]
            y_has_ctx = True
            gates = [p, REC_OG_A, REC_OG_B, rec_gn_a[j].reshape(1, -1), rec_gn_b[j].reshape(1, -1)]
        else:
            gates = None
            q, k2, vt = _qkv(stream, n_tiles, mods[l], ng, _att_weights(att_w_qkv[j]), cos, sin)
            y = _attention(q, k2, vt, att_sink[j], n_ctx)
            y0, y1, c0, c1, wo = y, y, 0, 1, att_w_o[j]
            y_has_ctx = False
        wo = wo.astype(BF16)
        if need_ctx:
            assert y_has_ctx
            xcat = _post(y0, y1, c0, c1, stream, n_tiles, mods[l], ng, wo, win_all, wout_all, l, gates)
            src = (xcat, xcat, n_ctx_tiles)
        else:
            if y_has_ctx:
                y0 = y0[:, :, n_ctx:]
                y1 = y1[:, :, n_ctx:]
                gates[0] = gates[0][:, :, n_ctx:]
            lat_only = _Stream(src[0], src[1], n_ctx_tiles, n_ctx_tiles, src[2], _sub_tiles(n_lat_tiles))
            x_lat = _post(y0, y1, c0, c1, lat_only, n_lat_tiles, mods[l], ng, wo, win_all, wout_all, l, gates)
    return x_lat
```
